```python
import math
import jax, jax.numpy as jnp
from jax import lax
import numpy as np

D_MODEL = 2048
BATCH = 4
SEQ = 4096
DEPTH = 4

GRID_W = 64
CTX_LEN = 256
N_MIXERS = 2
HEAD_DIM = 128
N_HEADS = D_MODEL // HEAD_DIM
N_KV_HEADS = 4
WINDOW = 128
BLOCK = WINDOW
ROPE_BASE = 10000.0
D_RNN = ((4 * D_MODEL // 3 + 255) // 256) * 256
RNN_BLOCKS = 16
RNN_BW = D_RNN // RNN_BLOCKS
CONV_W = 4
CONV_LEFT = CONV_W // 2
RG_C = 8.0
FFN_HIDDEN = ((8 * D_MODEL // 3 + 255) // 256) * 256
N_REC_LAYERS = (DEPTH + 1) // 2
N_ATT_LAYERS = DEPTH // 2
LN_EPS = 1e-5
NEG_INF = -1e30
MOD_INIT = 0.5

kernel_name = 'hybrid_rglru_swa_deepnorm_prefix_ctx'


def layer_norm(h, g, b):
    hf = h.astype(jnp.float32)
    mu = hf.mean(-1, keepdims=True)
    var = jnp.square(hf - mu).mean(-1, keepdims=True)
    return ((hf - mu) * lax.rsqrt(var + LN_EPS) * g + b).astype(h.dtype)


def modulate(h, shift, scale):
    return h * (1.0 + scale) + shift


def swiglu(u, w_gate, w_up, w_down):
    return (jax.nn.silu(u @ w_gate) * (u @ w_up)) @ w_down


def axial_rope_tables(n_tokens):
    rows = n_tokens // GRID_W
    row = jnp.repeat(jnp.arange(rows, dtype=jnp.float32), GRID_W)
    col = jnp.tile(jnp.arange(GRID_W, dtype=jnp.float32), rows)
    axis_dim = HEAD_DIM // 2
    inv_freq = ROPE_BASE ** (-jnp.arange(0, axis_dim, 2, dtype=jnp.float32) / axis_dim)
    ang = jnp.concatenate([row[:, None] * inv_freq, col[:, None] * inv_freq], axis=-1)
    return jnp.cos(ang), jnp.sin(ang)


def apply_rope(t, cos, sin):
    half = t.shape[-1] // 2
    t1, t2 = t[..., :half], t[..., half:]
    c, s = cos[:, None, :], sin[:, None, :]
    return jnp.concatenate([t1 * c - t2 * s, t2 * c + t1 * s], axis=-1).astype(t.dtype)


def sink_softmax(scores, sink):
    m = sink
    for s in scores:
        m = jnp.maximum(m, s.max(-1, keepdims=True))
    ps = [jnp.exp(s - m) for s in scores]
    denom = jnp.exp(sink - m) + sum(p.sum(-1, keepdims=True) for p in ps)
    return [p / denom for p in ps]


def windowed_latent_attention(q, k, v, kc, vc, sink):
    B, S, H, Dh = q.shape
    NB = S // BLOCK
    G = H // N_KV_HEADS
    scale = Dh ** -0.5
    qb = q.reshape(B, NB, BLOCK, N_KV_HEADS, G, Dh)

    def band(t):
        tp = jnp.pad(t, ((0, 0), (BLOCK, BLOCK), (0, 0), (0, 0)))
        return jnp.concatenate([tp[:, o * BLOCK:o * BLOCK + S].reshape(B, NB, BLOCK, N_KV_HEADS, Dh) for o in range(3)], axis=2)

    kb, vb = band(k), band(v)
    qi = jnp.arange(BLOCK)[:, None]
    si = jnp.arange(3 * BLOCK)[None, :]
    in_window = jnp.abs(si - BLOCK - qi) <= WINDOW
    key_pos = jnp.arange(NB)[:, None] * BLOCK - BLOCK + jnp.arange(3 * BLOCK)[None, :]
    in_range = (key_pos >= 0) & (key_pos < S)
    mask = in_window[None] & in_range[:, None, :]
    s_loc = jnp.einsum('bnqkgd,bnskd->bnkgqs', qb, kb).astype(jnp.float32) * scale
    s_loc = jnp.where(mask[None, :, None, None], s_loc, NEG_INF)
    s_ctx = jnp.einsum('bnqkgd,bckd->bnkgqc', qb, kc).astype(jnp.float32) * scale
    sink_b = sink.astype(jnp.float32).reshape(1, 1, N_KV_HEADS, G, 1, 1)
    p_loc, p_ctx = sink_softmax([s_loc, s_ctx], sink_b)
    o = (jnp.einsum('bnkgqs,bnskd->bnqkgd', p_loc.astype(v.dtype), vb)
         + jnp.einsum('bnkgqc,bckd->bnqkgd', p_ctx.astype(v.dtype), vc))
    return o.reshape(B, S, H * Dh)


def context_attention(qc, kc, vc, sink):
    B, C, H, Dh = qc.shape
    G = H // N_KV_HEADS
    qg = qc.reshape(B, C, N_KV_HEADS, G, Dh)
    s = jnp.einsum('bqkgd,bskd->bkgqs', qg, kc).astype(jnp.float32) * (Dh ** -0.5)
    (p,) = sink_softmax([s], sink.astype(jnp.float32).reshape(1, N_KV_HEADS, G, 1, 1))
    o = jnp.einsum('bkgqs,bskd->bqkgd', p.astype(vc.dtype), vc)
    return o.reshape(B, C, H * Dh)


def attention_mixer(uc, ux, w_qkv, sink, w_o, cos, sin, ctx_out):
    B, S, _ = ux.shape
    C = uc.shape[1]
    nq = N_HEADS * HEAD_DIM
    nkv = N_KV_HEADS * HEAD_DIM
    qkv = ux @ w_qkv
    q = apply_rope(qkv[..., :nq].reshape(B, S, N_HEADS, HEAD_DIM), cos, sin)
    k = apply_rope(qkv[..., nq:nq + nkv].reshape(B, S, N_KV_HEADS, HEAD_DIM), cos, sin)
    v = qkv[..., nq + nkv:].reshape(B, S, N_KV_HEADS, HEAD_DIM)
    if ctx_out:
        qkv_c = uc @ w_qkv
        qc = qkv_c[..., :nq].reshape(B, C, N_HEADS, HEAD_DIM)
        kvc = qkv_c[..., nq:]
    else:
        kvc = uc @ w_qkv[:, nq:]
    kc = kvc[..., :nkv].reshape(B, C, N_KV_HEADS, HEAD_DIM)
    vc = kvc[..., nkv:].reshape(B, C, N_KV_HEADS, HEAD_DIM)
    out_x = windowed_latent_attention(q, k, v, kc, vc, sink) @ w_o
    if not ctx_out:
        return None, out_x
    out_c = context_attention(qc, kc, vc, sink) @ w_o
    return out_c, out_x


def centred_dwconv(z, w, b):
    L = z.shape[1]
    zp = jnp.pad(z, ((0, 0), (CONV_LEFT, CONV_W - 1 - CONV_LEFT), (0, 0)))
    return sum(zp[:, j:j + L] * w[j] for j in range(CONV_W)) + b


def rglru_coeffs(z, wa, ba, wx, bx, lam):
    B, L, _ = z.shape
    zb = z.reshape(B, L, RNN_BLOCKS, RNN_BW)
    r = jax.nn.sigmoid((jnp.einsum('blhi,hij->blhj', zb, wa).reshape(B, L, D_RNN) + ba).astype(jnp.float32))
    ig = jax.nn.sigmoid((jnp.einsum('blhi,hij->blhj', zb, wx).reshape(B, L, D_RNN) + bx).astype(jnp.float32))
    log_a = -RG_C * r * jax.nn.softplus(-lam.astype(jnp.float32))
    a = jnp.exp(log_a)
    b = jnp.sqrt(-jnp.expm1(2.0 * log_a)) * ig * z.astype(jnp.float32)
    return a, b


def _linear_combine(left, right):
    a_l, b_l = left
    a_r, b_r = right
    return a_l * a_r, a_r * b_l + b_r


def linear_scan(a, b, h0, reverse):
    if h0 is not None:
        edge = -1 if reverse else 0
        b = b.at[:, edge].add(a[:, edge] * h0)
    return lax.associative_scan(_linear_combine, (a, b), reverse=reverse, axis=1)[1]


def recurrent_mixer(uc, ux, w_in, conv_w, conv_b, ga_w, ga_b, gx_w, gx_b, lam, w_out, ctx_out):
    px = ux @ w_in
    yx = jax.nn.gelu(px[..., :D_RNN])
    zx = centred_dwconv(px[..., D_RNN:], conv_w, conv_b)
    if ctx_out:
        pc = uc @ w_in
        yc = jax.nn.gelu(pc[..., :D_RNN])
        zc = centred_dwconv(pc[..., D_RNN:], conv_w, conv_b)
    else:
        zc = centred_dwconv(uc @ w_in[:, D_RNN:], conv_w, conv_b)
    hx = 0.0
    hc = 0.0
    for d, reverse in enumerate((False, True)):
        ac, bc = rglru_coeffs(zc, ga_w[d], ga_b[d], gx_w[d], gx_b[d], lam[d])
        h_ctx = linear_scan(ac, bc, None, reverse)
        h_end = h_ctx[:, 0] if reverse else h_ctx[:, -1]
        ax, bx_ = rglru_coeffs(zx, ga_w[d], ga_b[d], gx_w[d], gx_b[d], lam[d])
        hx = hx + linear_scan(ax, bx_, h_end, reverse)
        if ctx_out:
            hc = hc + h_ctx
    out_x = (yx * hx.astype(yx.dtype)) @ w_out
    if not ctx_out:
        return None, out_x
    out_c = (yc * hc.astype(yc.dtype)) @ w_out
    return out_c, out_x


def setup_inputs(seed: int = 0) -> dict:
    key = jax.random.key(seed)
    ks = iter(jax.random.split(key, 32))
    f32 = jnp.float32

    def nrm(shape, scale):
        return jax.random.normal(next(ks), shape, f32) * scale

    beta = (8.0 * DEPTH) ** -0.25
    D = D_MODEL
    u = jax.random.uniform(next(ks), (N_REC_LAYERS, 2, D_RNN), f32, minval=0.9, maxval=0.999)
    return {
        'x': nrm((BATCH, SEQ, D), 1.0),
        'c': nrm((BATCH, D), 1.0),
        'ctx': nrm((BATCH, CTX_LEN, D), 1.0),
        'c_ctx': nrm((D,), 1.0),
        'mod_w': nrm((DEPTH, D, 6 * D), MOD_INIT * D ** -0.5),
        'mod_b': nrm((DEPTH, 6 * D), 0.02),
        'ln_mix_g': 1.0 + nrm((DEPTH, D), 0.02),
        'ln_mix_b': nrm((DEPTH, D), 0.02),
        'ln_ffn_g': 1.0 + nrm((DEPTH, D), 0.02),
        'ln_ffn_b': nrm((DEPTH, D), 0.02),
        'ffn_w_gate': nrm((DEPTH, D, FFN_HIDDEN), D ** -0.5),
        'ffn_w_up': nrm((DEPTH, D, FFN_HIDDEN), D ** -0.5),
        'ffn_w_down': nrm((DEPTH, FFN_HIDDEN, D), beta * FFN_HIDDEN ** -0.5),
        'rec_w_in': nrm((N_REC_LAYERS, D, 2 * D_RNN), D ** -0.5),
        'rec_conv_w': nrm((N_REC_LAYERS, CONV_W, D_RNN), CONV_W ** -0.5),
        'rec_conv_b': nrm((N_REC_LAYERS, D_RNN), 0.02),
        'rec_gate_a_w': nrm((N_REC_LAYERS, 2, RNN_BLOCKS, RNN_BW, RNN_BW), RNN_BW ** -0.5),
        'rec_gate_a_b': nrm((N_REC_LAYERS, 2, D_RNN), 0.02),
        'rec_gate_x_w': nrm((N_REC_LAYERS, 2, RNN_BLOCKS, RNN_BW, RNN_BW), RNN_BW ** -0.5),
        'rec_gate_x_b': nrm((N_REC_LAYERS, 2, D_RNN), 0.02),
        'rec_lambda': jnp.log(u) - jnp.log1p(-u),
        'rec_w_out': nrm((N_REC_LAYERS, D_RNN, D), beta * D_RNN ** -0.5),
        'att_w_qkv': nrm((N_ATT_LAYERS, D, (N_HEADS + 2 * N_KV_HEADS) * HEAD_DIM), D ** -0.5),
        'att_sink': nrm((N_ATT_LAYERS, N_HEADS), 0.5),
        'att_w_o': nrm((N_ATT_LAYERS, N_HEADS * HEAD_DIM, D), beta * (N_HEADS * HEAD_DIM) ** -0.5),
    }


def reference(x, c, ctx, c_ctx, mod_w, mod_b, ln_mix_g, ln_mix_b, ln_ffn_g, ln_ffn_b,
              ffn_w_gate, ffn_w_up, ffn_w_down, rec_w_in, rec_conv_w, rec_conv_b,
              rec_gate_a_w, rec_gate_a_b, rec_gate_x_w, rec_gate_x_b, rec_lambda, rec_w_out,
              att_w_qkv, att_sink, att_w_o):
    S = x.shape[1]
    cos, sin = axial_rope_tables(S)
    alpha = (2.0 * DEPTH) ** 0.25
    hx, hc = x, ctx
    for i in range(DEPTH):
        last = i == DEPTH - 1
        j = i // N_MIXERS
        mod_x = jax.nn.silu(c) @ mod_w[i] + mod_b[i]
        mod_c = jax.nn.silu(c_ctx) @ mod_w[i] + mod_b[i]
        sh1x, sc1x, g1x, sh2x, sc2x, g2x = jnp.split(mod_x[:, None, :], 6, axis=-1)
        sh1c, sc1c, g1c, sh2c, sc2c, g2c = jnp.split(mod_c, 6)
        ux = modulate(hx, sh1x, sc1x)
        uc = modulate(hc, sh1c, sc1c)
        if i % N_MIXERS == 0:
            oc, ox = recurrent_mixer(uc, ux, rec_w_in[j], rec_conv_w[j], rec_conv_b[j],
                                     rec_gate_a_w[j], rec_gate_a_b[j], rec_gate_x_w[j], rec_gate_x_b[j],
                                     rec_lambda[j], rec_w_out[j], not last)
        else:
            oc, ox = attention_mixer(uc, ux, att_w_qkv[j], att_sink[j], att_w_o[j], cos, sin, not last)
        hx = layer_norm(alpha * hx + g1x * ox, ln_mix_g[i], ln_mix_b[i])
        hx = layer_norm(alpha * hx + g2x * swiglu(modulate(hx, sh2x, sc2x), ffn_w_gate[i], ffn_w_up[i], ffn_w_down[i]),
                        ln_ffn_g[i], ln_ffn_b[i])
        if not last:
            hc = layer_norm(alpha * hc + g1c * oc, ln_mix_g[i], ln_mix_b[i])
            hc = layer_norm(alpha * hc + g2c * swiglu(modulate(hc, sh2c, sc2c), ffn_w_gate[i], ffn_w_up[i], ffn_w_down[i]),
                            ln_ffn_g[i], ln_ffn_b[i])
    return hx
```

```python
import functools

import jax
import jax.numpy as jnp
from jax import lax
from jax.experimental import pallas as pl
from jax.experimental.pallas import tpu as pltpu

HEAD_DIM = 128
N_KV_HEADS = 4
WINDOW = 128
GRID_W = 64
ROPE_BASE = 10000.0
RNN_BLOCKS = 16
CONV_W = 4
CONV_LEFT = 2
RG_C = 8.0
LN_EPS = 1e-5
NEG_INF = -1e30
N_MIXERS = 2
N_MOD = 6

V7X_LANES = 128
V7X_SUBLANES = 8
V7X_MXU_WIDTH = 256
V7X_VMEM_BYTES = 64 * 1024 * 1024
VMEM_LIMIT_BYTES = V7X_VMEM_BYTES - 8 * 1024 * 1024

F32 = jnp.float32
BF16 = jnp.bfloat16
EPILOGUE_ROWS = 128


def _pick(n, cands):
    for c in cands:
        if n % c == 0:
            return c
    raise ValueError(f"no tile in {cands} divides {n}")


def _params(*sem):
    return pltpu.CompilerParams(dimension_semantics=sem, vmem_limit_bytes=VMEM_LIMIT_BYTES)


def _mod_spec(layer, chunk, row_fn):
    return lambda d: pl.BlockSpec((None, None, None, 1, d), lambda *g: (layer, row_fn(*g), chunk, 0, 0))


def _layer_norm(v, g, b):
    mu = jnp.mean(v, axis=-1, keepdims=True)
    d = v - mu
    var = jnp.mean(d * d, axis=-1, keepdims=True)
    return d * lax.rsqrt(var + LN_EPS) * g + b


def _deepnorm_epilogue(acc_ref, res_ref, gate_ref, lg_ref, lb_ref, h_ref, alpha, u_ref=None, sh_ref=None, sc_ref=None):
    tm = acc_ref.shape[0]
    ch = min(tm, EPILOGUE_ROWS)
    gate, lg, lb = gate_ref[...], lg_ref[...], lb_ref[...]
    if u_ref is not None:
        sh, sc1 = sh_ref[...], 1.0 + sc_ref[...]

    def body(k, carry):
        rs = pl.ds(pl.multiple_of(k * ch, ch), ch)
        hn = _layer_norm(alpha * res_ref[rs, :] + gate * acc_ref[rs, :], lg, lb)
        h_ref[rs, :] = hn
        if u_ref is not None:
            u_ref[rs, :] = (hn * sc1 + sh).astype(BF16)
        return carry

    lax.fori_loop(0, tm // ch, body, 0)


def _mods_kernel(cs_ref, w_ref, b_ref, o_ref):
    s = cs_ref[...]
    s = (s * jax.nn.sigmoid(s)).astype(BF16)
    o_ref[...] = jnp.dot(s, w_ref[...].astype(BF16), preferred_element_type=F32) + b_ref[...]


def _mods_call(cs, mod_w, mod_b):
    depth, d, n = mod_w.shape
    mr = cs.shape[0]
    tn = _pick(n, (1024, 512, 256, 128))
    return pl.pallas_call(
        _mods_kernel,
        grid=(depth, n // tn),
        in_specs=[
            pl.BlockSpec((mr, d), lambda l, j: (0, 0)),
            pl.BlockSpec((None, d, tn), lambda l, j: (l, 0, j)),
            pl.BlockSpec((None, 1, tn), lambda l, j: (l, 0, j)),
        ],
        out_specs=pl.BlockSpec((None, mr, tn), lambda l, j: (l, 0, j)),
        out_shape=jax.ShapeDtypeStruct((depth, mr, n), F32),
        compiler_params=_params("arbitrary", "arbitrary"),
        name="mods",
    )(cs, mod_w, mod_b.reshape(depth, 1, n))


def _modulate_kernel(h_ref, sh_ref, sc_ref, u_ref):
    u_ref[...] = (h_ref[...] * (1.0 + sc_ref[...]) + sh_ref[...]).astype(BF16)


def _modulate_call(h, mods5, layer, row_fn, tm):
    r, d = h.shape
    return pl.pallas_call(
        _modulate_kernel,
        grid=(r // tm,),
        in_specs=[
            pl.BlockSpec((tm, d), lambda i: (i, 0)),
            _mod_spec(layer, 0, row_fn)(d),
            _mod_spec(layer, 1, row_fn)(d),
        ],
        out_specs=pl.BlockSpec((tm, d), lambda i: (i, 0)),
        out_shape=jax.ShapeDtypeStruct((r, d), BF16),
        compiler_params=_params("arbitrary"),
        name="modulate0",
    )(h, mods5, mods5)


def _proj_ln_kernel(a_ref, w_ref, res_ref, gate_ref, sh_ref, sc_ref, lg_ref, lb_ref, h_ref, u_ref, acc, *, alpha):
    acc[...] = jnp.dot(a_ref[...], w_ref[...], preferred_element_type=F32)
    _deepnorm_epilogue(acc, res_ref, gate_ref, lg_ref, lb_ref, h_ref, alpha, u_ref, sh_ref, sc_ref)


def _proj_ln_call(a, w, res, mods5, layer, ln_g, ln_b, row_fn, tm, alpha, name):
    r, k = a.shape
    d = w.shape[1]
    vec = pl.BlockSpec((1, d), lambda i: (0, 0))
    return pl.pallas_call(
        functools.partial(_proj_ln_kernel, alpha=alpha),
        grid=(r // tm,),
        in_specs=[
            pl.BlockSpec((tm, k), lambda i: (i, 0)),
            pl.BlockSpec((k, d), lambda i: (0, 0), pipeline_mode=pl.Buffered(1)),
            pl.BlockSpec((tm, d), lambda i: (i, 0)),
            _mod_spec(layer, 2, row_fn)(d),
            _mod_spec(layer, 3, row_fn)(d),
            _mod_spec(layer, 4, row_fn)(d),
            vec,
            vec,
        ],
        out_specs=[pl.BlockSpec((tm, d), lambda i: (i, 0)), pl.BlockSpec((tm, d), lambda i: (i, 0))],
        out_shape=[jax.ShapeDtypeStruct((r, d), F32), jax.ShapeDtypeStruct((r, d), BF16)],
        scratch_shapes=[pltpu.VMEM((tm, d), F32)],
        compiler_params=_params("arbitrary"),
        name=name,
    )(a, w, res, mods5, mods5, mods5, ln_g.reshape(1, d), ln_b.reshape(1, d))


def _ffn_kernel(*refs, alpha, with_u):
    if with_u:
        u_ref, wg_ref, wu_ref, wd_ref, res_ref, gate_ref, sh_ref, sc_ref, lg_ref, lb_ref, h_ref, un_ref, acc = refs
    else:
        u_ref, wg_ref, wu_ref, wd_ref, res_ref, gate_ref, lg_ref, lb_ref, h_ref, acc = refs
    j = pl.program_id(1)

    @pl.when(j == 0)
    def _():
        acc[...] = jnp.zeros_like(acc)

    u = u_ref[...]
    g = jnp.dot(u, wg_ref[...], preferred_element_type=F32)
    up = jnp.dot(u, wu_ref[...], preferred_element_type=F32)
    hid = (g * jax.nn.sigmoid(g) * up).astype(BF16)
    acc[...] += jnp.dot(hid, wd_ref[...], preferred_element_type=F32)

    @pl.when(j == pl.num_programs(1) - 1)
    def _():
        if with_u:
            _deepnorm_epilogue(acc, res_ref, gate_ref, lg_ref, lb_ref, h_ref, alpha, un_ref, sh_ref, sc_ref)
        else:
            _deepnorm_epilogue(acc, res_ref, gate_ref, lg_ref, lb_ref, h_ref, alpha)


def _ffn_call(u, wg, wu, wd, res, mods5, layer, next_layer, ln_g, ln_b, row_fn, tm, alpha):
    r, d = u.shape
    hid = wg.shape[1]
    th = _pick(hid, (512, 256, 128))
    with_u = next_layer is not None
    row2 = lambda i, j: row_fn(i)
    vec = pl.BlockSpec((1, d), lambda i, j: (0, 0))
    tile = pl.BlockSpec((tm, d), lambda i, j: (i, 0))
    in_specs = [
        tile,
        pl.BlockSpec((d, th), lambda i, j: (0, j)),
        pl.BlockSpec((d, th), lambda i, j: (0, j)),
        pl.BlockSpec((th, d), lambda i, j: (j, 0)),
        tile,
        _mod_spec(layer, 5, row2)(d),
    ]
    args = [u, wg, wu, wd, res, mods5]
    if with_u:
        in_specs += [_mod_spec(next_layer, 0, row2)(d), _mod_spec(next_layer, 1, row2)(d)]
        args += [mods5, mods5]
    in_specs += [vec, vec]
    args += [ln_g.reshape(1, d), ln_b.reshape(1, d)]
    out_specs = [tile]
    out_shape = [jax.ShapeDtypeStruct((r, d), F32)]
    if with_u:
        out_specs.append(tile)
        out_shape.append(jax.ShapeDtypeStruct((r, d), BF16))
    out = pl.pallas_call(
        functools.partial(_ffn_kernel, alpha=alpha, with_u=with_u),
        grid=(r // tm, hid // th),
        in_specs=in_specs,
        out_specs=out_specs,
        out_shape=out_shape,
        scratch_shapes=[pltpu.VMEM((tm, d), F32)],
        compiler_params=_params("arbitrary", "arbitrary"),
        name="ffn",
    )(*args)
    return (out[0], out[1]) if with_u else (out[0], None)


def _gelu_tanh(x):
    return x * (0.5 * (1.0 + jnp.tanh(0.7978845608028654 * (x + 0.044715 * (x * x * x)))))


def _win_kernel(u_ref, wy_ref, wz_ref, y_ref, z_ref):
    u = u_ref[...]
    y_ref[...] = _gelu_tanh(jnp.dot(u, wy_ref[...], preferred_element_type=F32)).astype(BF16)
    z_ref[...] = jnp.dot(u, wz_ref[...], preferred_element_type=F32)


def _win_call(u, w_in, tm):
    r, d = u.shape
    dr = w_in.shape[1] // 2
    tn = dr // 2 if (dr // 2) % V7X_LANES == 0 else dr
    nt = dr // tn
    return pl.pallas_call(
        _win_kernel,
        grid=(nt, r // tm),
        in_specs=[
            pl.BlockSpec((tm, d), lambda j, i: (i, 0)),
            pl.BlockSpec((d, tn), lambda j, i: (0, j)),
            pl.BlockSpec((d, tn), lambda j, i: (0, nt + j)),
        ],
        out_specs=[pl.BlockSpec((tm, tn), lambda j, i: (i, j)), pl.BlockSpec((tm, tn), lambda j, i: (i, j))],
        out_shape=[jax.ShapeDtypeStruct((r, dr), BF16), jax.ShapeDtypeStruct((r, dr), F32)],
        compiler_params=_params("arbitrary", "arbitrary"),
        name="rec_in",
    )(u, w_in, w_in)


def _gate_window_plan(dr, bw):
    tn = V7X_MXU_WIDTH
    spans = []
    for n in range(dr // tn):
        c0 = n * tn
        lo = (c0 // bw) * bw
        hi = ((c0 + tn - 1) // bw + 1) * bw
        spans.append(((lo // V7X_LANES) * V7X_LANES, -(-hi // V7X_LANES) * V7X_LANES))
    kw = max(h - l for l, h in spans)
    return kw, tuple(min(l, dr - kw) for l, _ in spans)


def _gate_windows(w, kw, k0s):
    nb, bw, _ = w.shape
    dr = nb * bw
    dense = jnp.einsum("hij,hg->higj", w.astype(BF16), jnp.eye(nb, dtype=BF16)).reshape(dr, dr)
    tn = V7X_MXU_WIDTH
    return jnp.stack([dense[k0:k0 + kw, n * tn:(n + 1) * tn] for n, k0 in enumerate(k0s)])


def _scan_kernel(*refs, reverse, tt, kw, k0s, ns):
    if reverse:
        (zc_ref, zp_ref, zn_ref, cw_ref, cb_ref, wa_ref, wx_ref, ba_ref, bx_ref, lam_ref, hf_ref, y_ref,
         o_ref, xpad, zs, zb, a_s, b_s, h_s, carry) = refs
    else:
        (zc_ref, zp_ref, zn_ref, cw_ref, cb_ref, wa_ref, wx_ref, ba_ref, bx_ref, lam_ref,
         o_ref, xpad, zs, zb, a_s, b_s, carry) = refs
    tn = V7X_MXU_WIDTH
    sub = V7X_SUBLANES
    t = pl.program_id(1)

    @pl.when(t == 0)
    def _():
        carry[...] = jnp.zeros_like(carry)

    st = (ns - t) if reverse else (t - 1)
    first = jnp.logical_or(t == 0, st == 0)
    last = jnp.logical_or(t == 0, st == ns - 1)
    xpad[0:sub, :] = jnp.where(first, 0.0, zp_ref[...])
    xpad[sub:sub + tt, :] = zc_ref[...]
    xpad[sub + tt:2 * sub + tt, :] = jnp.where(last, 0.0, zn_ref[...])

    nt = len(k0s)
    for n in range(nt):
        cs = slice(n * tn, (n + 1) * tn)
        acc = None
        for j in range(CONV_W):
            term = xpad[sub - CONV_LEFT + j:sub - CONV_LEFT + j + tt, cs] * cw_ref[j:j + 1, cs]
            acc = term if acc is None else acc + term
        z = acc + cb_ref[:, cs]
        zs[:, cs] = z
        zb[:, cs] = z.astype(BF16)

    row = lax.broadcasted_iota(jnp.int32, (sub, tn), 0)
    ng = tt // sub
    for n in range(nt):
        cs = slice(n * tn, (n + 1) * tn)
        zw = zb[:, k0s[n]:k0s[n] + kw]
        r = jax.nn.sigmoid(jnp.dot(zw, wa_ref[n], preferred_element_type=F32) + ba_ref[:, cs])
        ig = jax.nn.sigmoid(jnp.dot(zw, wx_ref[n], preferred_element_type=F32) + bx_ref[:, cs])
        nl = -lam_ref[:, cs]
        softplus = jnp.maximum(nl, 0.0) + jnp.log1p(jnp.exp(-jnp.abs(nl)))
        log_a = -RG_C * r * softplus
        a = jnp.exp(log_a)
        a_s[...] = a
        b_s[...] = jnp.sqrt(-jnp.tanh(log_a) * (1.0 + a * a)) * ig * zs[:, cs]

        def body(i, h, cs=cs):
            g = (ng - 1 - i) if reverse else i
            off = pl.multiple_of(g * sub, sub)
            a8 = a_s[pl.ds(off, sub), :]
            b8 = b_s[pl.ds(off, sub), :]
            for s in (1, 2, 4):
                if reverse:
                    ok = row < sub - s
                    a_sh = pltpu.roll(a8, sub - s, 0)
                    b_sh = pltpu.roll(b8, sub - s, 0)
                else:
                    ok = row >= s
                    a_sh = pltpu.roll(a8, s, 0)
                    b_sh = pltpu.roll(b8, s, 0)
                b8 = a8 * jnp.where(ok, b_sh, 0.0) + b8
                a8 = a8 * jnp.where(ok, a_sh, 1.0)
            h8 = a8 * h + b8
            if reverse:
                h_s[pl.ds(off, sub), :] = h8
                edge = h8[0:1, :]
            else:
                o_ref[pl.ds(off, sub), cs] = h8
                edge = h8[sub - 1:sub, :]
            return jnp.broadcast_to(edge, (sub, tn))

        carry[:, cs] = lax.fori_loop(0, ng, body, carry[:, cs], unroll=4)
        if reverse:
            o_ref[:, cs] = (y_ref[:, cs].astype(F32) * (hf_ref[:, cs] + h_s[...])).astype(BF16)


def _scan_call(z, conv_w, conv_b, wa, wx, ba, bx, lam, kw, k0s, batch, seq, ctx_len, hf=None, y=None):
    reverse = hf is not None
    r, dr = z.shape
    tt = ctx_len
    ns = seq // tt
    nx = batch * seq // tt
    g8 = tt // V7X_SUBLANES
    nblk8 = r // V7X_SUBLANES

    def cur(b, t):
        st = (ns - t) if reverse else (t - 1)
        return jnp.where(t == 0, nx + b, b * ns + st)

    tile = pl.BlockSpec((tt, dr), lambda b, t: (cur(b, t), 0))
    halo_p = pl.BlockSpec((V7X_SUBLANES, dr), lambda b, t: (jnp.maximum(cur(b, t) * g8 - 1, 0), 0))
    halo_n = pl.BlockSpec((V7X_SUBLANES, dr), lambda b, t: (jnp.minimum(cur(b, t) * g8 + g8, nblk8 - 1), 0))
    vec = pl.BlockSpec((1, dr), lambda b, t: (0, 0))
    nt = len(k0s)
    wspec = pl.BlockSpec((nt, kw, V7X_MXU_WIDTH), lambda b, t: (0, 0, 0), pipeline_mode=pl.Buffered(1))
    in_specs = [tile, halo_p, halo_n, pl.BlockSpec((CONV_W, dr), lambda b, t: (0, 0)), vec, wspec, wspec, vec, vec, vec]
    args = [z, z, z, conv_w, conv_b.reshape(1, dr), wa, wx, ba.reshape(1, dr), bx.reshape(1, dr), lam.reshape(1, dr)]
    scratch = [
        pltpu.VMEM((tt + 2 * V7X_SUBLANES, dr), F32),
        pltpu.VMEM((tt, dr), F32),
        pltpu.VMEM((tt, dr), BF16),
        pltpu.VMEM((tt, V7X_MXU_WIDTH), F32),
        pltpu.VMEM((tt, V7X_MXU_WIDTH), F32),
    ]
    if reverse:
        in_specs += [tile, tile]
        args += [hf, y]
        scratch.append(pltpu.VMEM((tt, V7X_MXU_WIDTH), F32))
    scratch.append(pltpu.VMEM((V7X_SUBLANES, dr), F32))
    return pl.pallas_call(
        functools.partial(_scan_kernel, reverse=reverse, tt=tt, kw=kw, k0s=k0s, ns=ns),
        grid=(batch, ns + 1),
        in_specs=in_specs,
        out_specs=tile,
        out_shape=jax.ShapeDtypeStruct((r, dr), BF16 if reverse else F32),
        scratch_shapes=scratch,
        compiler_params=_params("arbitrary", "arbitrary"),
        name="rec_scan_bwd" if reverse else "rec_scan_fwd",
    )(*args)


def _rope_tables(seq, pad_rows):
    rows = seq // GRID_W
    row = jnp.repeat(jnp.arange(rows, dtype=F32), GRID_W)
    col = jnp.tile(jnp.arange(GRID_W, dtype=F32), rows)
    axis_dim = HEAD_DIM // 2
    inv_freq = ROPE_BASE ** (-jnp.arange(0, axis_dim, 2, dtype=F32) / axis_dim)
    ang = jnp.concatenate([row[:, None] * inv_freq, col[:, None] * inv_freq], axis=-1)
    cos, sin = jnp.cos(ang), jnp.sin(ang)
    cos2 = jnp.concatenate([cos, cos], axis=-1)
    sin2 = jnp.concatenate([-sin, sin], axis=-1)
    cos2 = jnp.concatenate([cos2, jnp.ones((pad_rows, HEAD_DIM), F32)], axis=0)
    sin2 = jnp.concatenate([sin2, jnp.zeros((pad_rows, HEAD_DIM), F32)], axis=0)
    return cos2, sin2


def _qkv_kernel(u_ref, w_ref, cos_ref, sin_ref, o_ref, *, n_q_tiles, n_rope_tiles, scale):
    j = pl.program_id(1)
    p = jnp.dot(u_ref[...], w_ref[...], preferred_element_type=F32)

    @pl.when(j < n_rope_tiles)
    def _():
        c = cos_ref[...]
        s = sin_ref[...]
        mul = jnp.where(j < n_q_tiles, scale, 1.0)
        for h in range(p.shape[1] // HEAD_DIM):
            hs = slice(h * HEAD_DIM, (h + 1) * HEAD_DIM)
            t = p[:, hs]
            o_ref[:, hs] = ((t * c + pltpu.roll(t, HEAD_DIM // 2, 1) * s) * mul).astype(BF16)

    @pl.when(j >= n_rope_tiles)
    def _():
        o_ref[...] = p.astype(BF16)


def _qkv_call(u, w, cos2, sin2, seq, rows_x, tm):
    r, d = u.shape
    n = w.shape[1]
    nkv = N_KV_HEADS * HEAD_DIM
    tn = nkv
    nq = n - 2 * nkv
    n_seq_tiles = seq // tm
    tab = pl.BlockSpec((tm, HEAD_DIM), lambda i, j: (jnp.where(i < rows_x // tm, i % n_seq_tiles, n_seq_tiles), 0))
    return pl.pallas_call(
        functools.partial(_qkv_kernel, n_q_tiles=nq // tn, n_rope_tiles=(nq + nkv) // tn, scale=HEAD_DIM ** -0.5),
        grid=(r // tm, n // tn),
        in_specs=[pl.BlockSpec((tm, d), lambda i, j: (i, 0)), pl.BlockSpec((d, tn), lambda i, j: (0, j)), tab, tab],
        out_specs=pl.BlockSpec((tm, tn), lambda i, j: (i, j)),
        out_shape=jax.ShapeDtypeStruct((r, n), BF16),
        compiler_params=_params("arbitrary", "arbitrary"),
        name="att_qkv",
    )(u, w, cos2, sin2)


def _attn_kernel(sink_ref, q_ref, kp_ref, kc_ref, kn_ref, vp_ref, vc_ref, vn_ref, kx_ref, vx_ref, o_ref, *, nb, g):
    j = pl.program_id(1)
    blk = WINDOW
    rows = g * blk
    is_lat = j < nb
    pen_p = jnp.where(jnp.logical_and(is_lat, j > 0), 0.0, NEG_INF)
    pen_c = jnp.where(is_lat, 0.0, NEG_INF)
    pen_n = jnp.where(jnp.logical_and(is_lat, j < nb - 1), 0.0, NEG_INF)
    qi = lax.broadcasted_iota(jnp.int32, (rows, blk), 0) & (blk - 1)
    ki = lax.broadcasted_iota(jnp.int32, (rows, blk), 1)
    bias_p = jnp.where(ki >= qi, pen_p, NEG_INF)
    bias_n = jnp.where(ki <= qi, pen_n, NEG_INF)
    nt_dims = (((1,), (1,)), ((), ()))
    for kh in range(N_KV_HEADS):
        ks = slice(kh * HEAD_DIM, (kh + 1) * HEAD_DIM)
        heads = [kh * g + gi for gi in range(g)]
        qs = jnp.concatenate([q_ref[:, h * HEAD_DIM:(h + 1) * HEAD_DIM] for h in heads], axis=0)
        sk = jnp.concatenate([jnp.full((blk, 1), sink_ref[h], F32) for h in heads], axis=0)
        s_p = lax.dot_general(qs, kp_ref[:, ks], nt_dims, preferred_element_type=F32) + bias_p
        s_c = lax.dot_general(qs, kc_ref[:, ks], nt_dims, preferred_element_type=F32) + pen_c
        s_n = lax.dot_general(qs, kn_ref[:, ks], nt_dims, preferred_element_type=F32) + bias_n
        s_x = lax.dot_general(qs, kx_ref[:, ks], nt_dims, preferred_element_type=F32)
        m = jnp.maximum(jnp.maximum(jnp.max(s_p, -1, keepdims=True), jnp.max(s_c, -1, keepdims=True)),
                        jnp.maximum(jnp.max(s_n, -1, keepdims=True), jnp.max(s_x, -1, keepdims=True)))
        m = jnp.maximum(m, sk)
        p_p = jnp.exp(s_p - m)
        p_c = jnp.exp(s_c - m)
        p_n = jnp.exp(s_n - m)
        p_x = jnp.exp(s_x - m)
        denom = (jnp.exp(sk - m) + jnp.sum(p_p, -1, keepdims=True) + jnp.sum(p_c, -1, keepdims=True)
                 + jnp.sum(p_n, -1, keepdims=True) + jnp.sum(p_x, -1, keepdims=True))
        o = (jnp.dot(p_p.astype(BF16), vp_ref[:, ks], preferred_element_type=F32)
             + jnp.dot(p_c.astype(BF16), vc_ref[:, ks], preferred_element_type=F32)
             + jnp.dot(p_n.astype(BF16), vn_ref[:, ks], preferred_element_type=F32)
             + jnp.dot(p_x.astype(BF16), vx_ref[:, ks], preferred_element_type=F32))
        o = o / denom
        for gi, h in enumerate(heads):
            o_ref[:, h * HEAD_DIM:(h + 1) * HEAD_DIM] = o[gi * blk:(gi + 1) * blk].astype(BF16)


def _attn_call(qkv, sink, batch, seq, ctx_len):
    r, n = qkv.shape
    nkv = N_KV_HEADS * HEAD_DIM
    d = n - 2 * nkv
    g = d // HEAD_DIM // N_KV_HEADS
    blk = WINDOW
    nb = seq // blk
    ncb = ctx_len // blk
    kcol = d // nkv
    x0 = batch * seq

    def qrow(b, j):
        return jnp.where(j < nb, b * nb + j, x0 // blk + b * ncb + (j - nb))

    def krow(b, j, off):
        return b * nb + jnp.clip(jnp.minimum(j, nb - 1) + off, 0, nb - 1)

    qspec = pl.BlockSpec((blk, d), lambda b, j: (qrow(b, j), 0))
    band = lambda off, col: pl.BlockSpec((blk, nkv), lambda b, j: (krow(b, j, off), col))
    ctxs = lambda col: pl.BlockSpec((ctx_len, nkv), lambda b, j: (x0 // ctx_len + b, col))
    return pl.pallas_call(
        functools.partial(_attn_kernel, nb=nb, g=g),
        grid=(batch, nb + ncb),
        in_specs=[
            pl.BlockSpec(memory_space=pltpu.SMEM),
            qspec,
            band(-1, kcol), band(0, kcol), band(1, kcol),
            band(-1, kcol + 1), band(0, kcol + 1), band(1, kcol + 1),
            ctxs(kcol), ctxs(kcol + 1),
        ],
        out_specs=qspec,
        out_shape=jax.ShapeDtypeStruct((r, d), BF16),
        compiler_params=_params("arbitrary", "arbitrary"),
        name="att_core",
    )(sink, qkv, qkv, qkv, qkv, qkv, qkv, qkv, qkv, qkv)


def kernel(x, c, ctx, c_ctx, mod_w, mod_b, ln_mix_g, ln_mix_b, ln_ffn_g, ln_ffn_b, ffn_w_gate, ffn_w_up, ffn_w_down,
           rec_w_in, rec_conv_w, rec_conv_b, rec_gate_a_w, rec_gate_a_b, rec_gate_x_w, rec_gate_x_b, rec_lambda,
           rec_w_out, att_w_qkv, att_sink, att_w_o):
    batch, seq, d = x.shape
    ctx_len = ctx.shape[1]
    depth = mod_w.shape[0]
    rows_x, rows_c = batch * seq, batch * ctx_len
    assert seq % ctx_len == 0 and ctx_len % WINDOW == 0 and seq % GRID_W == 0
    assert d % (N_KV_HEADS * HEAD_DIM) == 0
    alpha = (2.0 * depth) ** 0.25

    tm = _pick(seq, (512, 256, 128))
    while rows_c % tm:
        tm //= 2
    tm_big = 2 * tm if (seq % (2 * tm) == 0 and rows_c % (2 * tm) == 0) else tm

    def row_fn_for(t):
        return lambda i: jnp.minimum((i * t) // seq, batch)

    row_fn, row_fn_big = row_fn_for(tm), row_fn_for(tm_big)

    h = jnp.concatenate([x.reshape(rows_x, d), ctx.reshape(rows_c, d)], axis=0)
    mr = -(-(batch + 1) // V7X_SUBLANES) * V7X_SUBLANES
    cs = jnp.zeros((mr, d), F32).at[:batch].set(c).at[batch].set(c_ctx)
    mods5 = _mods_call(cs, mod_w, mod_b).reshape(depth, mr, N_MOD, 1, d)

    cos2, sin2 = _rope_tables(seq, tm_big)
    u = _modulate_call(h, mods5, 0, row_fn_big, tm_big)
    for i in range(depth):
        j = i // N_MIXERS
        if i % N_MIXERS == 0:
            dr = rec_w_out.shape[1]
            kw, k0s = _gate_window_plan(dr, dr // RNN_BLOCKS)
            y, z = _win_call(u, rec_w_in[j].astype(BF16), tm)
            hf = None
            for dirn in range(2):
                wa = _gate_windows(rec_gate_a_w[j, dirn], kw, k0s)
                wx = _gate_windows(rec_gate_x_w[j, dirn], kw, k0s)
                out = _scan_call(z, rec_conv_w[j], rec_conv_b[j], wa, wx, rec_gate_a_b[j, dirn], rec_gate_x_b[j, dirn],
                                 rec_lambda[j, dirn], kw, k0s, batch, seq, ctx_len, hf=hf, y=y if dirn else None)
                hf = out
            h, u = _proj_ln_call(hf, rec_w_out[j].astype(BF16), h, mods5, i, ln_mix_g[i], ln_mix_b[i], row_fn, tm,
                                 alpha, "rec_out")
        else:
            qkv = _qkv_call(u, att_w_qkv[j].astype(BF16), cos2, sin2, seq, rows_x, tm_big)
            ao = _attn_call(qkv, att_sink[j], batch, seq, ctx_len)
            h, u = _proj_ln_call(ao, att_w_o[j].astype(BF16), h, mods5, i, ln_mix_g[i], ln_mix_b[i], row_fn, tm,
                                 alpha, "att_out")
        h, u = _ffn_call(u, ffn_w_gate[i].astype(BF16), ffn_w_up[i].astype(BF16), ffn_w_down[i].astype(BF16), h, mods5,
                         i, i + 1 if i + 1 < depth else None, ln_ffn_g[i], ln_ffn_b[i], row_fn, tm, alpha)
    return h[:rows_x].reshape(batch, seq, d)
```

```python
import functools

import jax
import jax.numpy as jnp
from jax import lax
from jax.experimental import pallas as pl
from jax.experimental.pallas import tpu as pltpu

HEAD_DIM = 128
N_KV_HEADS = 4
WINDOW = 128
GRID_W = 64
ROPE_BASE = 10000.0
RNN_BLOCKS = 16
CONV_W = 4
CONV_LEFT = 2
RG_C = 8.0
LN_EPS = 1e-5
NEG_INF = -1e30
N_MIXERS = 2
N_MOD = 6

V7X_LANES = 128
V7X_SUBLANES = 8
V7X_MXU_WIDTH = 256
V7X_VMEM_BYTES = 64 * 1024 * 1024
VMEM_LIMIT_BYTES = V7X_VMEM_BYTES - 8 * 1024 * 1024

F32 = jnp.float32
BF16 = jnp.bfloat16
EPILOGUE_ROWS = 128


def _pick(n, cands):
    for c in cands:
        if n % c == 0:
            return c
    raise ValueError(f"no tile in {cands} divides {n}")


def _params(*sem):
    return pltpu.CompilerParams(dimension_semantics=sem, vmem_limit_bytes=VMEM_LIMIT_BYTES)


def _mod_spec(layer, chunk, row_fn):
    return lambda d: pl.BlockSpec((None, None, None, 1, d), lambda *g: (layer, row_fn(*g), chunk, 0, 0))


def _layer_norm(v, g, b):
    mu = jnp.mean(v, axis=-1, keepdims=True)
    d = v - mu
    var = jnp.mean(d * d, axis=-1, keepdims=True)
    return d * lax.rsqrt(var + LN_EPS) * g + b


def _deepnorm_epilogue(acc_ref, res_ref, gate_ref, lg_ref, lb_ref, h_ref, alpha, u_ref=None, sh_ref=None, sc_ref=None):
    tm = acc_ref.shape[0]
    ch = min(tm, EPILOGUE_ROWS)
    gate, lg, lb = gate_ref[...], lg_ref[...], lb_ref[...]
    if u_ref is not None:
        sh, sc1 = sh_ref[...], 1.0 + sc_ref[...]

    def body(k, carry):
        rs = pl.ds(pl.multiple_of(k * ch, ch), ch)
        hn = _layer_norm(alpha * res_ref[rs, :] + gate * acc_ref[rs, :], lg, lb)
        h_ref[rs, :] = hn
        if u_ref is not None:
            u_ref[rs, :] = (hn * sc1 + sh).astype(BF16)
        return carry

    lax.fori_loop(0, tm // ch, body, 0)


def _mods_kernel(cs_ref, w_ref, b_ref, o_ref):
    s = cs_ref[...]
    s = (s * jax.nn.sigmoid(s)).astype(BF16)
    o_ref[...] = jnp.dot(s, w_ref[...].astype(BF16), preferred_element_type=F32) + b_ref[...]


def _mods_call(cs, mod_w, mod_b):
    depth, d, n = mod_w.shape
    mr = cs.shape[0]
    tn = _pick(n, (1024, 512, 256, 128))
    return pl.pallas_call(
        _mods_kernel,
        grid=(depth, n // tn),
        in_specs=[
            pl.BlockSpec((mr, d), lambda l, j: (0, 0)),
            pl.BlockSpec((None, d, tn), lambda l, j: (l, 0, j)),
            pl.BlockSpec((None, 1, tn), lambda l, j: (l, 0, j)),
        ],
        out_specs=pl.BlockSpec((None, mr, tn), lambda l, j: (l, 0, j)),
        out_shape=jax.ShapeDtypeStruct((depth, mr, n), F32),
        compiler_params=_params("arbitrary", "arbitrary"),
        name="mods",
    )(cs, mod_w, mod_b.reshape(depth, 1, n))


def _modulate_kernel(x_ref, c_ref, sh_ref, sc_ref, h_ref, u_ref, *, n_lat):
    def emit(src_ref):
        v = src_ref[...]
        h_ref[...] = v
        u_ref[...] = (v * (1.0 + sc_ref[...]) + sh_ref[...]).astype(BF16)

    pl.when(pl.program_id(0) < n_lat)(lambda: emit(x_ref))
    pl.when(pl.program_id(0) >= n_lat)(lambda: emit(c_ref))


def _modulate_call(x2, c2, mods5, layer, row_fn, tm):
    d = x2.shape[1]
    n_lat, n_ctx = x2.shape[0] // tm, c2.shape[0] // tm
    r = x2.shape[0] + c2.shape[0]
    tile = pl.BlockSpec((tm, d), lambda i: (i, 0))
    return pl.pallas_call(
        functools.partial(_modulate_kernel, n_lat=n_lat),
        grid=(n_lat + n_ctx,),
        in_specs=[
            pl.BlockSpec((tm, d), lambda i: (jnp.minimum(i, n_lat - 1), 0)),
            pl.BlockSpec((tm, d), lambda i: (jnp.maximum(i - n_lat, 0), 0)),
            _mod_spec(layer, 0, row_fn)(d),
            _mod_spec(layer, 1, row_fn)(d),
        ],
        out_specs=[tile, tile],
        out_shape=[jax.ShapeDtypeStruct((r, d), F32), jax.ShapeDtypeStruct((r, d), BF16)],
        compiler_params=_params("arbitrary"),
        name="modulate0",
    )(x2, c2, mods5, mods5)


def _proj_ln_kernel(a_ref, w_ref, res_ref, gate_ref, sh_ref, sc_ref, lg_ref, lb_ref, h_ref, u_ref, acc, *, alpha):
    acc[...] = jnp.dot(a_ref[...], w_ref[...], preferred_element_type=F32)
    _deepnorm_epilogue(acc, res_ref, gate_ref, lg_ref, lb_ref, h_ref, alpha, u_ref, sh_ref, sc_ref)


def _proj_ln_call(a, w, res, rows, mods5, layer, ln_g, ln_b, row_fn, tm, alpha, name):
    r, k = rows, a.shape[1]
    d = w.shape[1]
    vec = pl.BlockSpec((1, d), lambda i: (0, 0))
    return pl.pallas_call(
        functools.partial(_proj_ln_kernel, alpha=alpha),
        grid=(r // tm,),
        in_specs=[
            pl.BlockSpec((tm, k), lambda i: (i, 0)),
            pl.BlockSpec((k, d), lambda i: (0, 0), pipeline_mode=pl.Buffered(1)),
            pl.BlockSpec((tm, d), lambda i: (i, 0)),
            _mod_spec(layer, 2, row_fn)(d),
            _mod_spec(layer, 3, row_fn)(d),
            _mod_spec(layer, 4, row_fn)(d),
            vec,
            vec,
        ],
        out_specs=[pl.BlockSpec((tm, d), lambda i: (i, 0)), pl.BlockSpec((tm, d), lambda i: (i, 0))],
        out_shape=[jax.ShapeDtypeStruct((r, d), F32), jax.ShapeDtypeStruct((r, d), BF16)],
        scratch_shapes=[pltpu.VMEM((tm, d), F32)],
        compiler_params=_params("arbitrary"),
        name=name,
    )(a, w, res, mods5, mods5, mods5, ln_g.reshape(1, d), ln_b.reshape(1, d))


def _ffn_kernel(*refs, alpha, with_u):
    if with_u:
        u_ref, wg_ref, wu_ref, wd_ref, res_ref, gate_ref, sh_ref, sc_ref, lg_ref, lb_ref, h_ref, un_ref, acc = refs
    else:
        u_ref, wg_ref, wu_ref, wd_ref, res_ref, gate_ref, lg_ref, lb_ref, h_ref, acc = refs
    j = pl.program_id(1)

    @pl.when(j == 0)
    def _():
        acc[...] = jnp.zeros_like(acc)

    u = u_ref[...]
    g = jnp.dot(u, wg_ref[...], preferred_element_type=F32)
    up = jnp.dot(u, wu_ref[...], preferred_element_type=F32)
    hid = (g * jax.nn.sigmoid(g) * up).astype(BF16)
    acc[...] += jnp.dot(hid, wd_ref[...], preferred_element_type=F32)

    @pl.when(j == pl.num_programs(1) - 1)
    def _():
        if with_u:
            _deepnorm_epilogue(acc, res_ref, gate_ref, lg_ref, lb_ref, h_ref, alpha, un_ref, sh_ref, sc_ref)
        else:
            _deepnorm_epilogue(acc, res_ref, gate_ref, lg_ref, lb_ref, h_ref, alpha)


def _ffn_call(u, wg, wu, wd, res, rows, mods5, layer, next_layer, ln_g, ln_b, row_fn, tm, alpha):
    r, d = rows, u.shape[1]
    hid = wg.shape[1]
    th = _pick(hid, (512, 256, 128))
    with_u = next_layer is not None
    row2 = lambda i, j: row_fn(i)
    vec = pl.BlockSpec((1, d), lambda i, j: (0, 0))
    tile = pl.BlockSpec((tm, d), lambda i, j: (i, 0))
    in_specs = [
        tile,
        pl.BlockSpec((d, th), lambda i, j: (0, j)),
        pl.BlockSpec((d, th), lambda i, j: (0, j)),
        pl.BlockSpec((th, d), lambda i, j: (j, 0)),
        tile,
        _mod_spec(layer, 5, row2)(d),
    ]
    args = [u, wg, wu, wd, res, mods5]
    if with_u:
        in_specs += [_mod_spec(next_layer, 0, row2)(d), _mod_spec(next_layer, 1, row2)(d)]
        args += [mods5, mods5]
    in_specs += [vec, vec]
    args += [ln_g.reshape(1, d), ln_b.reshape(1, d)]
    out_specs = [tile]
    out_shape = [jax.ShapeDtypeStruct((r, d), F32)]
    if with_u:
        out_specs.append(tile)
        out_shape.append(jax.ShapeDtypeStruct((r, d), BF16))
    out = pl.pallas_call(
        functools.partial(_ffn_kernel, alpha=alpha, with_u=with_u),
        grid=(r // tm, hid // th),
        in_specs=in_specs,
        out_specs=out_specs,
        out_shape=out_shape,
        scratch_shapes=[pltpu.VMEM((tm, d), F32)],
        compiler_params=_params("arbitrary", "arbitrary"),
        name="ffn",
    )(*args)
    return (out[0], out[1]) if with_u else (out[0], None)


def _gelu_tanh(x):
    return x * (0.5 * (1.0 + jnp.tanh(0.7978845608028654 * (x + 0.044715 * (x * x * x)))))


def _win_kernel(u_ref, wy_ref, wz_ref, y_ref, z_ref):
    u = u_ref[...]
    y_ref[...] = _gelu_tanh(jnp.dot(u, wy_ref[...], preferred_element_type=F32)).astype(BF16)
    z_ref[...] = jnp.dot(u, wz_ref[...], preferred_element_type=F32)


def _win_call(u, w_in, tm):
    r, d = u.shape
    dr = w_in.shape[1] // 2
    tn = dr // 2 if (dr // 2) % V7X_LANES == 0 else dr
    nt = dr // tn
    return pl.pallas_call(
        _win_kernel,
        grid=(nt, r // tm),
        in_specs=[
            pl.BlockSpec((tm, d), lambda j, i: (i, 0)),
            pl.BlockSpec((d, tn), lambda j, i: (0, j)),
            pl.BlockSpec((d, tn), lambda j, i: (0, nt + j)),
        ],
        out_specs=[pl.BlockSpec((tm, tn), lambda j, i: (i, j)), pl.BlockSpec((tm, tn), lambda j, i: (i, j))],
        out_shape=[jax.ShapeDtypeStruct((r, dr), BF16), jax.ShapeDtypeStruct((r, dr), F32)],
        compiler_params=_params("arbitrary", "arbitrary"),
        name="rec_in",
    )(u, w_in, w_in)


def _gate_window_plan(dr, bw):
    tn = V7X_MXU_WIDTH
    spans = []
    for n in range(dr // tn):
        c0 = n * tn
        lo = (c0 // bw) * bw
        hi = ((c0 + tn - 1) // bw + 1) * bw
        spans.append(((lo // V7X_LANES) * V7X_LANES, -(-hi // V7X_LANES) * V7X_LANES))
    kw = max(h - l for l, h in spans)
    return kw, tuple(min(l, dr - kw) for l, _ in spans)


def _gate_windows(w, kw, k0s):
    lead, bw = w.shape[:-3], w.shape[-1]
    tn = V7X_MXU_WIDTH
    w = w.astype(BF16)
    tiles = []
    for n, k0 in enumerate(k0s):
        c0 = n * tn
        h0, h1 = c0 // bw, (c0 + tn - 1) // bw
        nbk = h1 - h0 + 1
        diag = jnp.einsum("...hij,hg->...higj", w[..., h0:h1 + 1, :, :], jnp.eye(nbk, dtype=BF16))
        diag = diag.reshape(*lead, nbk * bw, nbk * bw)
        r_off, c_off = h0 * bw - k0, h0 * bw - c0
        cfg = [(0, 0, 0)] * len(lead) + [(r_off, kw - r_off - nbk * bw, 0), (c_off, tn - c_off - nbk * bw, 0)]
        tiles.append(lax.pad(diag, jnp.zeros((), BF16), cfg))
    return jnp.stack(tiles, axis=-3)


def _gates(zb, z_ref, wa_ref, wx_ref, ba_ref, bx_ref, lam_ref, a_s, b_s, kw, k0s):
    tn = V7X_MXU_WIDTH
    for n, k0 in enumerate(k0s):
        cs = slice(n * tn, (n + 1) * tn)
        zw = zb[:, k0:k0 + kw]
        ta = jnp.tanh(0.5 * (jnp.dot(zw, wa_ref[n], preferred_element_type=F32) + ba_ref[:, cs]))
        tx = jnp.tanh(0.5 * (jnp.dot(zw, wx_ref[n], preferred_element_type=F32) + bx_ref[:, cs]))
        nl = -lam_ref[:, cs]
        half = (0.5 * RG_C) * (jnp.maximum(nl, 0.0) + jnp.log1p(jnp.exp(-jnp.abs(nl))))
        q = ta * half + half
        a = jnp.exp(-q)
        a_s[:, cs] = a
        b_s[:, cs] = jnp.sqrt(jnp.tanh(q) * (1.0 + a * a)) * (0.5 * tx + 0.5) * z_ref[:, cs]


def _scan(a_s, b_s, dst, carry, reverse):
    tt, dr = a_s.shape
    tn = V7X_MXU_WIDTH
    sub = V7X_SUBLANES
    ng = tt // sub
    row = lax.broadcasted_iota(jnp.int32, (sub, tn), 0)

    def body(i, c):
        g = (ng - 1 - i) if reverse else i
        rs = pl.ds(pl.multiple_of(g * sub, sub), sub)
        for n in range(dr // tn):
            cs = slice(n * tn, (n + 1) * tn)
            a8 = a_s[rs, cs]
            b8 = b_s[rs, cs]
            for s in (1, 2, 4):
                if reverse:
                    ok = row < sub - s
                    a_sh = pltpu.roll(a8, sub - s, 0)
                    b_sh = pltpu.roll(b8, sub - s, 0)
                else:
                    ok = row >= s
                    a_sh = pltpu.roll(a8, s, 0)
                    b_sh = pltpu.roll(b8, s, 0)
                b8 = a8 * jnp.where(ok, b_sh, 0.0) + b8
                a8 = a8 * jnp.where(ok, a_sh, 1.0)
            h8 = a8 * carry[:, cs] + b8
            dst[rs, cs] = h8
            edge = h8[0:1, :] if reverse else h8[sub - 1:sub, :]
            carry[:, cs] = jnp.broadcast_to(edge, (sub, tn))
        return c

    lax.fori_loop(0, ng, body, 0)


def _scan_fwd_kernel(zc_ref, zp_ref, zn_ref, cw_ref, cb_ref, wa_ref, wx_ref, ba_ref, bx_ref, lam_ref,
                     hf_ref, z_ref, xpad, zb, a_s, b_s, carry, *, kw, k0s, ns):
    tt = zc_ref.shape[0]
    tn = V7X_MXU_WIDTH
    sub = V7X_SUBLANES
    t = pl.program_id(1)

    @pl.when(t == 0)
    def _():
        carry[...] = jnp.zeros_like(carry)

    first = t <= 1
    last = jnp.logical_or(t == 0, t == ns)
    xpad[0:sub, :] = jnp.where(first, 0.0, zp_ref[...])
    xpad[sub:sub + tt, :] = zc_ref[...]
    xpad[sub + tt:2 * sub + tt, :] = jnp.where(last, 0.0, zn_ref[...])
    for n in range(len(k0s)):
        cs = slice(n * tn, (n + 1) * tn)
        acc = None
        for j in range(CONV_W):
            term = xpad[sub - CONV_LEFT + j:sub - CONV_LEFT + j + tt, cs] * cw_ref[j:j + 1, cs]
            acc = term if acc is None else acc + term
        z = acc + cb_ref[:, cs]
        z_ref[:, cs] = z
        zb[:, cs] = z.astype(BF16)

    _gates(zb, z_ref, wa_ref, wx_ref, ba_ref, bx_ref, lam_ref, a_s, b_s, kw, k0s)
    _scan(a_s, b_s, hf_ref, carry, reverse=False)


def _scan_bwd_kernel(z_ref, wa_ref, wx_ref, ba_ref, bx_ref, lam_ref, hf_ref, y_ref, o_ref, zb, a_s, b_s, h_s, carry,
                     *, kw, k0s):
    tn = V7X_MXU_WIDTH

    @pl.when(pl.program_id(1) == 0)
    def _():
        carry[...] = jnp.zeros_like(carry)

    zb[...] = z_ref[...].astype(BF16)
    _gates(zb, z_ref, wa_ref, wx_ref, ba_ref, bx_ref, lam_ref, a_s, b_s, kw, k0s)
    _scan(a_s, b_s, h_s, carry, reverse=True)
    for n in range(len(k0s)):
        cs = slice(n * tn, (n + 1) * tn)
        o_ref[:, cs] = (y_ref[:, cs].astype(F32) * (hf_ref[:, cs] + h_s[:, cs])).astype(BF16)


def _scan_calls(zpre, y, conv_w, conv_b, wa, wx, ba, bx, lam, kw, k0s, batch, seq, ctx_len):
    r, dr = zpre.shape
    tt = ctx_len
    ns = seq // tt
    nx = batch * seq // tt
    g8 = tt // V7X_SUBLANES
    nblk8 = r // V7X_SUBLANES
    nt = len(k0s)

    def cur(reverse):
        def f(b, t):
            st = (ns - t) if reverse else (t - 1)
            return jnp.where(t == 0, nx + b, b * ns + st)
        return f

    def tile(reverse):
        return pl.BlockSpec((tt, dr), lambda b, t: (cur(reverse)(b, t), 0))

    fw = cur(False)
    halo_p = pl.BlockSpec((V7X_SUBLANES, dr), lambda b, t: (jnp.maximum(fw(b, t) * g8 - 1, 0), 0))
    halo_n = pl.BlockSpec((V7X_SUBLANES, dr), lambda b, t: (jnp.minimum(fw(b, t) * g8 + g8, nblk8 - 1), 0))
    vec = lambda d: pl.BlockSpec((None, 1, dr), lambda b, t: (d, 0, 0))
    wspec = lambda d: pl.BlockSpec((None, nt, kw, V7X_MXU_WIDTH), lambda b, t: (d, 0, 0, 0),
                                   pipeline_mode=pl.Buffered(1))
    ba, bx, lam = (v.reshape(2, 1, dr) for v in (ba, bx, lam))
    gate_specs = lambda d: [wspec(d), wspec(d), vec(d), vec(d), vec(d)]
    gate_args = [wa, wx, ba, bx, lam]
    work = [
        pltpu.VMEM((tt, dr), BF16),
        pltpu.VMEM((tt, dr), F32),
        pltpu.VMEM((tt, dr), F32),
    ]
    state = pltpu.VMEM((V7X_SUBLANES, dr), F32)
    hf, z = pl.pallas_call(
        functools.partial(_scan_fwd_kernel, kw=kw, k0s=k0s, ns=ns),
        grid=(batch, ns + 1),
        in_specs=[tile(False), halo_p, halo_n, pl.BlockSpec((CONV_W, dr), lambda b, t: (0, 0)),
                  pl.BlockSpec((1, dr), lambda b, t: (0, 0))] + gate_specs(0),
        out_specs=[tile(False), tile(False)],
        out_shape=[jax.ShapeDtypeStruct((r, dr), F32), jax.ShapeDtypeStruct((r, dr), F32)],
        scratch_shapes=[pltpu.VMEM((tt + 2 * V7X_SUBLANES, dr), F32)] + work + [state],
        compiler_params=_params("arbitrary", "arbitrary"),
        name="rec_scan_fwd",
    )(zpre, zpre, zpre, conv_w, conv_b.reshape(1, dr), *gate_args)
    return pl.pallas_call(
        functools.partial(_scan_bwd_kernel, kw=kw, k0s=k0s),
        grid=(batch, ns + 1),
        in_specs=[tile(True)] + gate_specs(1) + [tile(True), tile(True)],
        out_specs=tile(True),
        out_shape=jax.ShapeDtypeStruct((r, dr), BF16),
        scratch_shapes=work + [pltpu.VMEM((tt, dr), F32), state],
        compiler_params=_params("arbitrary", "arbitrary"),
        name="rec_scan_bwd",
    )(z, *gate_args, hf, y)


def _rope_tables(seq, pad_rows):
    rows = seq // GRID_W
    row = jnp.repeat(jnp.arange(rows, dtype=F32), GRID_W)
    col = jnp.tile(jnp.arange(GRID_W, dtype=F32), rows)
    axis_dim = HEAD_DIM // 2
    inv_freq = ROPE_BASE ** (-jnp.arange(0, axis_dim, 2, dtype=F32) / axis_dim)
    ang = jnp.concatenate([row[:, None] * inv_freq, col[:, None] * inv_freq], axis=-1)
    cos, sin = jnp.cos(ang), jnp.sin(ang)
    cos2 = jnp.concatenate([cos, cos], axis=-1)
    sin2 = jnp.concatenate([-sin, sin], axis=-1)
    cos2 = jnp.concatenate([cos2, jnp.ones((pad_rows, HEAD_DIM), F32)], axis=0)
    sin2 = jnp.concatenate([sin2, jnp.zeros((pad_rows, HEAD_DIM), F32)], axis=0)
    return cos2, sin2


def _qkv_kernel(u_ref, w_ref, cos_ref, sin_ref, o_ref, *, n_q_tiles, n_rope_tiles, scale):
    j = pl.program_id(1)
    p = jnp.dot(u_ref[...], w_ref[...], preferred_element_type=F32)

    @pl.when(j < n_rope_tiles)
    def _():
        c = cos_ref[...]
        s = sin_ref[...]
        mul = jnp.where(j < n_q_tiles, scale, 1.0)
        for h in range(p.shape[1] // HEAD_DIM):
            hs = slice(h * HEAD_DIM, (h + 1) * HEAD_DIM)
            t = p[:, hs]
            o_ref[:, hs] = ((t * c + pltpu.roll(t, HEAD_DIM // 2, 1) * s) * mul).astype(BF16)

    @pl.when(j >= n_rope_tiles)
    def _():
        o_ref[...] = p.astype(BF16)


def _qkv_call(u, w, cos2, sin2, seq, rows_x, tm):
    r, d = u.shape
    n = w.shape[1]
    nkv = N_KV_HEADS * HEAD_DIM
    tn = nkv
    nq = n - 2 * nkv
    n_seq_tiles = seq // tm
    tab = pl.BlockSpec((tm, HEAD_DIM), lambda i, j: (jnp.where(i < rows_x // tm, i % n_seq_tiles, n_seq_tiles), 0))
    return pl.pallas_call(
        functools.partial(_qkv_kernel, n_q_tiles=nq // tn, n_rope_tiles=(nq + nkv) // tn, scale=HEAD_DIM ** -0.5),
        grid=(r // tm, n // tn),
        in_specs=[pl.BlockSpec((tm, d), lambda i, j: (i, 0)), pl.BlockSpec((d, tn), lambda i, j: (0, j)), tab, tab],
        out_specs=pl.BlockSpec((tm, tn), lambda i, j: (i, j)),
        out_shape=jax.ShapeDtypeStruct((r, n), BF16),
        compiler_params=_params("arbitrary", "arbitrary"),
        name="att_qkv",
    )(u, w, cos2, sin2)


def _attn_kernel(sink_ref, q_ref, kp_ref, kc_ref, kn_ref, vp_ref, vc_ref, vn_ref, kx_ref, vx_ref, o_ref, *, nb, g):
    j = pl.program_id(1)
    blk = WINDOW
    rows = g * blk
    is_lat = j < nb
    pen_p = jnp.where(jnp.logical_and(is_lat, j > 0), 0.0, NEG_INF)
    pen_c = jnp.where(is_lat, 0.0, NEG_INF)
    pen_n = jnp.where(jnp.logical_and(is_lat, j < nb - 1), 0.0, NEG_INF)
    qi = lax.broadcasted_iota(jnp.int32, (rows, blk), 0) & (blk - 1)
    ki = lax.broadcasted_iota(jnp.int32, (rows, blk), 1)
    bias_p = jnp.where(ki >= qi, pen_p, NEG_INF)
    bias_n = jnp.where(ki <= qi, pen_n, NEG_INF)
    nt_dims = (((1,), (1,)), ((), ()))
    for kh in range(N_KV_HEADS):
        ks = slice(kh * HEAD_DIM, (kh + 1) * HEAD_DIM)
        heads = [kh * g + gi for gi in range(g)]
        qs = jnp.concatenate([q_ref[:, h * HEAD_DIM:(h + 1) * HEAD_DIM] for h in heads], axis=0)
        sk = jnp.concatenate([jnp.full((blk, 1), sink_ref[h], F32) for h in heads], axis=0)
        s_p = lax.dot_general(qs, kp_ref[:, ks], nt_dims, preferred_element_type=F32) + bias_p
        s_c = lax.dot_general(qs, kc_ref[:, ks], nt_dims, preferred_element_type=F32) + pen_c
        s_n = lax.dot_general(qs, kn_ref[:, ks], nt_dims, preferred_element_type=F32) + bias_n
        s_x = lax.dot_general(qs, kx_ref[:, ks], nt_dims, preferred_element_type=F32)
        m = jnp.maximum(jnp.maximum(jnp.max(s_p, -1, keepdims=True), jnp.max(s_c, -1, keepdims=True)),
                        jnp.maximum(jnp.max(s_n, -1, keepdims=True), jnp.max(s_x, -1, keepdims=True)))
        m = jnp.maximum(m, sk)
        p_p = jnp.exp(s_p - m)
        p_c = jnp.exp(s_c - m)
        p_n = jnp.exp(s_n - m)
        p_x = jnp.exp(s_x - m)
        denom = (jnp.exp(sk - m) + jnp.sum(p_p, -1, keepdims=True) + jnp.sum(p_c, -1, keepdims=True)
                 + jnp.sum(p_n, -1, keepdims=True) + jnp.sum(p_x, -1, keepdims=True))
        o = (jnp.dot(p_p.astype(BF16), vp_ref[:, ks], preferred_element_type=F32)
             + jnp.dot(p_c.astype(BF16), vc_ref[:, ks], preferred_element_type=F32)
             + jnp.dot(p_n.astype(BF16), vn_ref[:, ks], preferred_element_type=F32)
             + jnp.dot(p_x.astype(BF16), vx_ref[:, ks], preferred_element_type=F32))
        o = o / denom
        for gi, h in enumerate(heads):
            o_ref[:, h * HEAD_DIM:(h + 1) * HEAD_DIM] = o[gi * blk:(gi + 1) * blk].astype(BF16)


def _attn_call(qkv, sink, batch, seq, ctx_len, ctx_queries):
    r, n = qkv.shape
    nkv = N_KV_HEADS * HEAD_DIM
    d = n - 2 * nkv
    g = d // HEAD_DIM // N_KV_HEADS
    blk = WINDOW
    nb = seq // blk
    ncb = ctx_len // blk
    kcol = d // nkv
    x0 = batch * seq

    def qrow(b, j):
        return jnp.where(j < nb, b * nb + j, x0 // blk + b * ncb + (j - nb))

    def krow(b, j, off):
        return b * nb + jnp.clip(jnp.minimum(j, nb - 1) + off, 0, nb - 1)

    qspec = pl.BlockSpec((blk, d), lambda b, j: (qrow(b, j), 0))
    band = lambda off, col: pl.BlockSpec((blk, nkv), lambda b, j: (krow(b, j, off), col))
    ctxs = lambda col: pl.BlockSpec((ctx_len, nkv), lambda b, j: (x0 // ctx_len + b, col))
    return pl.pallas_call(
        functools.partial(_attn_kernel, nb=nb, g=g),
        grid=(batch, nb + ncb if ctx_queries else nb),
        in_specs=[
            pl.BlockSpec(memory_space=pltpu.SMEM),
            qspec,
            band(-1, kcol), band(0, kcol), band(1, kcol),
            band(-1, kcol + 1), band(0, kcol + 1), band(1, kcol + 1),
            ctxs(kcol), ctxs(kcol + 1),
        ],
        out_specs=qspec,
        out_shape=jax.ShapeDtypeStruct((r if ctx_queries else x0, d), BF16),
        compiler_params=_params("arbitrary", "arbitrary"),
        name="att_core",
    )(sink, qkv, qkv, qkv, qkv, qkv, qkv, qkv, qkv, qkv)


def kernel(x, c, ctx, c_ctx, mod_w, mod_b, ln_mix_g, ln_mix_b, ln_ffn_g, ln_ffn_b, ffn_w_gate, ffn_w_up, ffn_w_down,
           rec_w_in, rec_conv_w, rec_conv_b, rec_gate_a_w, rec_gate_a_b, rec_gate_x_w, rec_gate_x_b, rec_lambda,
           rec_w_out, att_w_qkv, att_sink, att_w_o):
    batch, seq, d = x.shape
    ctx_len = ctx.shape[1]
    depth = mod_w.shape[0]
    rows_x, rows_c = batch * seq, batch * ctx_len
    assert seq % ctx_len == 0 and ctx_len % WINDOW == 0 and seq % GRID_W == 0
    assert d % (N_KV_HEADS * HEAD_DIM) == 0
    alpha = (2.0 * depth) ** 0.25

    tm = _pick(seq, (512, 256, 128))
    while rows_c % tm:
        tm //= 2
    tm_big = 2 * tm if (seq % (2 * tm) == 0 and rows_c % (2 * tm) == 0) else tm

    def row_fn_for(t):
        return lambda i: jnp.minimum((i * t) // seq, batch)

    row_fn, row_fn_big = row_fn_for(tm), row_fn_for(tm_big)

    mr = -(-(batch + 1) // V7X_SUBLANES) * V7X_SUBLANES
    cs = jnp.zeros((mr, d), F32).at[:batch].set(c).at[batch].set(c_ctx)
    mods5 = _mods_call(cs, mod_w, mod_b).reshape(depth, mr, N_MOD, 1, d)

    cos2, sin2 = _rope_tables(seq, tm_big)
    dr = rec_w_out.shape[1]
    kw, k0s = _gate_window_plan(dr, dr // RNN_BLOCKS)
    wa_all = _gate_windows(rec_gate_a_w, kw, k0s)
    wx_all = _gate_windows(rec_gate_x_w, kw, k0s)

    h, u = _modulate_call(x.reshape(rows_x, d), ctx.reshape(rows_c, d), mods5, 0, row_fn, tm)
    for i in range(depth):
        j = i // N_MIXERS
        last = i == depth - 1
        rows = rows_x if last else rows_x + rows_c
        if i % N_MIXERS == 0:
            y, zpre = _win_call(u, rec_w_in[j].astype(BF16), tm)
            yh = _scan_calls(zpre, y, rec_conv_w[j], rec_conv_b[j], wa_all[j], wx_all[j], rec_gate_a_b[j],
                             rec_gate_x_b[j], rec_lambda[j], kw, k0s, batch, seq, ctx_len)
            h, u = _proj_ln_call(yh, rec_w_out[j].astype(BF16), h, rows, mods5, i, ln_mix_g[i], ln_mix_b[i], row_fn, tm,
                                 alpha, "rec_out")
        else:
            qkv = _qkv_call(u, att_w_qkv[j].astype(BF16), cos2, sin2, seq, rows_x, tm_big)
            ao = _attn_call(qkv, att_sink[j], batch, seq, ctx_len, ctx_queries=not last)
            h, u = _proj_ln_call(ao, att_w_o[j].astype(BF16), h, rows, mods5, i, ln_mix_g[i], ln_mix_b[i], row_fn, tm,
                                 alpha, "att_out")
        h, u = _ffn_call(u, ffn_w_gate[i].astype(BF16), ffn_w_up[i].astype(BF16), ffn_w_down[i].astype(BF16), h, rows,
                         mods5, i, None if last else i + 1, ln_ffn_g[i], ln_ffn_b[i], row_fn, tm, alpha)
    return h.reshape(batch, seq, d)
```

```python
import functools

import jax
import jax.numpy as jnp
from jax import lax
from jax.experimental import pallas as pl
from jax.experimental.pallas import tpu as pltpu

HEAD_DIM = 128
N_KV_HEADS = 4
WINDOW = 128
GRID_W = 64
ROPE_BASE = 10000.0
RNN_BLOCKS = 16
CONV_W = 4
CONV_LEFT = 2
RG_C = 8.0
LN_EPS = 1e-5
NEG_INF = -1e30
N_MIXERS = 2
N_MOD = 6

V7X_LANES = 128
V7X_SUBLANES = 8
V7X_MXU_WIDTH = 256
V7X_VMEM_BYTES = 64 * 1024 * 1024
VMEM_LIMIT_BYTES = V7X_VMEM_BYTES - 8 * 1024 * 1024

F32 = jnp.float32
BF16 = jnp.bfloat16
EPILOGUE_ROWS = 128


def _pick(n, cands):
    for c in cands:
        if n % c == 0:
            return c
    raise ValueError(f"no tile in {cands} divides {n}")


def _params(*sem):
    return pltpu.CompilerParams(dimension_semantics=sem, vmem_limit_bytes=VMEM_LIMIT_BYTES)


def _mod_spec(layer, chunk, row_fn):
    return lambda d: pl.BlockSpec((None, None, None, 1, d), lambda *g: (layer, row_fn(*g), chunk, 0, 0))


def _layer_norm(v, g, b):
    mu = jnp.mean(v, axis=-1, keepdims=True)
    d = v - mu
    var = jnp.mean(d * d, axis=-1, keepdims=True)
    return d * lax.rsqrt(var + LN_EPS) * g + b


def _deepnorm_epilogue(acc_ref, res_ref, gate_ref, lg_ref, lb_ref, h_ref, alpha, u_ref=None, sh_ref=None, sc_ref=None):
    tm = acc_ref.shape[0]
    ch = min(tm, EPILOGUE_ROWS)
    gate, lg, lb = gate_ref[...], lg_ref[...], lb_ref[...]
    if u_ref is not None:
        sh, sc1 = sh_ref[...], 1.0 + sc_ref[...]

    def body(k, carry):
        rs = pl.ds(pl.multiple_of(k * ch, ch), ch)
        hn = _layer_norm(alpha * res_ref[rs, :] + gate * acc_ref[rs, :], lg, lb)
        h_ref[rs, :] = hn
        if u_ref is not None:
            u_ref[rs, :] = (hn * sc1 + sh).astype(BF16)
        return carry

    lax.fori_loop(0, tm // ch, body, 0)


def _mods_kernel(cs_ref, w_ref, b_ref, o_ref):
    s = cs_ref[...]
    s = (s * jax.nn.sigmoid(s)).astype(BF16)
    o_ref[...] = jnp.dot(s, w_ref[...].astype(BF16), preferred_element_type=F32) + b_ref[...]


def _mods_call(cs, mod_w, mod_b):
    depth, d, n = mod_w.shape
    mr = cs.shape[0]
    tn = _pick(n, (1024, 512, 256, 128))
    return pl.pallas_call(
        _mods_kernel,
        grid=(depth, n // tn),
        in_specs=[
            pl.BlockSpec((mr, d), lambda l, j: (0, 0)),
            pl.BlockSpec((None, d, tn), lambda l, j: (l, 0, j)),
            pl.BlockSpec((None, 1, tn), lambda l, j: (l, 0, j)),
        ],
        out_specs=pl.BlockSpec((None, mr, tn), lambda l, j: (l, 0, j)),
        out_shape=jax.ShapeDtypeStruct((depth, mr, n), F32),
        compiler_params=_params("arbitrary", "arbitrary"),
        name="mods",
    )(cs, mod_w, mod_b.reshape(depth, 1, n))


def _modulate_kernel(x_ref, c_ref, sh_ref, sc_ref, h_ref, u_ref, *, n_lat):
    def emit(src_ref):
        v = src_ref[...]
        h_ref[...] = v
        u_ref[...] = (v * (1.0 + sc_ref[...]) + sh_ref[...]).astype(BF16)

    pl.when(pl.program_id(0) < n_lat)(lambda: emit(x_ref))
    pl.when(pl.program_id(0) >= n_lat)(lambda: emit(c_ref))


def _modulate_call(x2, c2, mods5, layer, row_fn, tm):
    d = x2.shape[1]
    n_lat, n_ctx = x2.shape[0] // tm, c2.shape[0] // tm
    r = x2.shape[0] + c2.shape[0]
    tile = pl.BlockSpec((tm, d), lambda i: (i, 0))
    return pl.pallas_call(
        functools.partial(_modulate_kernel, n_lat=n_lat),
        grid=(n_lat + n_ctx,),
        in_specs=[
            pl.BlockSpec((tm, d), lambda i: (jnp.minimum(i, n_lat - 1), 0)),
            pl.BlockSpec((tm, d), lambda i: (jnp.maximum(i - n_lat, 0), 0)),
            _mod_spec(layer, 0, row_fn)(d),
            _mod_spec(layer, 1, row_fn)(d),
        ],
        out_specs=[tile, tile],
        out_shape=[jax.ShapeDtypeStruct((r, d), F32), jax.ShapeDtypeStruct((r, d), BF16)],
        compiler_params=_params("arbitrary"),
        name="modulate0",
    )(x2, c2, mods5, mods5)


def _proj_ln_kernel(a_ref, w_ref, res_ref, gate_ref, sh_ref, sc_ref, lg_ref, lb_ref, h_ref, u_ref, acc, *, alpha):
    acc[...] = jnp.dot(a_ref[...], w_ref[...], preferred_element_type=F32)
    _deepnorm_epilogue(acc, res_ref, gate_ref, lg_ref, lb_ref, h_ref, alpha, u_ref, sh_ref, sc_ref)


def _proj_ln_call(a, w, widx, res, rows, mods5, layer, ln_g, ln_b, row_fn, tm, alpha, name):
    r, k = rows, a.shape[1]
    d = w.shape[2]
    vec = pl.BlockSpec((1, d), lambda i: (0, 0))
    return pl.pallas_call(
        functools.partial(_proj_ln_kernel, alpha=alpha),
        grid=(r // tm,),
        in_specs=[
            pl.BlockSpec((tm, k), lambda i: (i, 0)),
            pl.BlockSpec((None, k, d), lambda i: (widx, 0, 0), pipeline_mode=pl.Buffered(1)),
            pl.BlockSpec((tm, d), lambda i: (i, 0)),
            _mod_spec(layer, 2, row_fn)(d),
            _mod_spec(layer, 3, row_fn)(d),
            _mod_spec(layer, 4, row_fn)(d),
            vec,
            vec,
        ],
        out_specs=[pl.BlockSpec((tm, d), lambda i: (i, 0)), pl.BlockSpec((tm, d), lambda i: (i, 0))],
        out_shape=[jax.ShapeDtypeStruct((r, d), F32), jax.ShapeDtypeStruct((r, d), BF16)],
        scratch_shapes=[pltpu.VMEM((tm, d), F32)],
        compiler_params=_params("arbitrary"),
        name=name,
    )(a, w, res, mods5, mods5, mods5, ln_g.reshape(1, d), ln_b.reshape(1, d))


def _ffn_kernel(*refs, alpha, with_u):
    if with_u:
        u_ref, wg_ref, wu_ref, wd_ref, res_ref, gate_ref, sh_ref, sc_ref, lg_ref, lb_ref, h_ref, un_ref, acc = refs
    else:
        u_ref, wg_ref, wu_ref, wd_ref, res_ref, gate_ref, lg_ref, lb_ref, h_ref, acc = refs
    j = pl.program_id(1)

    @pl.when(j == 0)
    def _():
        acc[...] = jnp.zeros_like(acc)

    u = u_ref[...]
    g = jnp.dot(u, wg_ref[...], preferred_element_type=F32)
    up = jnp.dot(u, wu_ref[...], preferred_element_type=F32)
    hid = (g * jax.nn.sigmoid(g) * up).astype(BF16)
    acc[...] += jnp.dot(hid, wd_ref[...], preferred_element_type=F32)

    @pl.when(j == pl.num_programs(1) - 1)
    def _():
        if with_u:
            _deepnorm_epilogue(acc, res_ref, gate_ref, lg_ref, lb_ref, h_ref, alpha, un_ref, sh_ref, sc_ref)
        else:
            _deepnorm_epilogue(acc, res_ref, gate_ref, lg_ref, lb_ref, h_ref, alpha)


def _ffn_call(u, wg, wu, wd, res, rows, mods5, layer, next_layer, ln_g, ln_b, row_fn, tm, alpha):
    r, d = rows, u.shape[1]
    hid = wg.shape[2]
    th = _pick(hid, (512, 256, 128))
    with_u = next_layer is not None
    row2 = lambda i, j: row_fn(i)
    vec = pl.BlockSpec((1, d), lambda i, j: (0, 0))
    tile = pl.BlockSpec((tm, d), lambda i, j: (i, 0))
    in_specs = [
        tile,
        pl.BlockSpec((None, d, th), lambda i, j: (layer, 0, j)),
        pl.BlockSpec((None, d, th), lambda i, j: (layer, 0, j)),
        pl.BlockSpec((None, th, d), lambda i, j: (layer, j, 0)),
        tile,
        _mod_spec(layer, 5, row2)(d),
    ]
    args = [u, wg, wu, wd, res, mods5]
    if with_u:
        in_specs += [_mod_spec(next_layer, 0, row2)(d), _mod_spec(next_layer, 1, row2)(d)]
        args += [mods5, mods5]
    in_specs += [vec, vec]
    args += [ln_g.reshape(1, d), ln_b.reshape(1, d)]
    out_specs = [tile]
    out_shape = [jax.ShapeDtypeStruct((r, d), F32)]
    if with_u:
        out_specs.append(tile)
        out_shape.append(jax.ShapeDtypeStruct((r, d), BF16))
    out = pl.pallas_call(
        functools.partial(_ffn_kernel, alpha=alpha, with_u=with_u),
        grid=(r // tm, hid // th),
        in_specs=in_specs,
        out_specs=out_specs,
        out_shape=out_shape,
        scratch_shapes=[pltpu.VMEM((tm, d), F32)],
        compiler_params=_params("arbitrary", "arbitrary"),
        name="ffn",
    )(*args)
    return (out[0], out[1]) if with_u else (out[0], None)


def _gelu_tanh(x):
    return x * (0.5 * (1.0 + jnp.tanh(0.7978845608028654 * (x + 0.044715 * (x * x * x)))))


def _win_kernel(u_ref, wy_ref, wz_ref, y_ref, z_ref):
    u = u_ref[...]
    y_ref[...] = _gelu_tanh(jnp.dot(u, wy_ref[...], preferred_element_type=F32)).astype(BF16)
    z_ref[...] = jnp.dot(u, wz_ref[...], preferred_element_type=F32)


def _win_call(u, w_in, widx, tm):
    r, d = u.shape
    dr = w_in.shape[2] // 2
    tn = dr // 2 if (dr // 2) % V7X_LANES == 0 else dr
    nt = dr // tn
    return pl.pallas_call(
        _win_kernel,
        grid=(nt, r // tm),
        in_specs=[
            pl.BlockSpec((tm, d), lambda j, i: (i, 0)),
            pl.BlockSpec((None, d, tn), lambda j, i: (widx, 0, j)),
            pl.BlockSpec((None, d, tn), lambda j, i: (widx, 0, nt + j)),
        ],
        out_specs=[pl.BlockSpec((tm, tn), lambda j, i: (i, j)), pl.BlockSpec((tm, tn), lambda j, i: (i, j))],
        out_shape=[jax.ShapeDtypeStruct((r, dr), BF16), jax.ShapeDtypeStruct((r, dr), F32)],
        compiler_params=_params("arbitrary", "arbitrary"),
        name="rec_in",
    )(u, w_in, w_in)


def _gate_window_plan(dr, bw):
    tn = V7X_MXU_WIDTH
    spans = []
    for n in range(dr // tn):
        c0 = n * tn
        lo = (c0 // bw) * bw
        hi = ((c0 + tn - 1) // bw + 1) * bw
        spans.append(((lo // V7X_LANES) * V7X_LANES, -(-hi // V7X_LANES) * V7X_LANES))
    kw = max(h - l for l, h in spans)
    return kw, tuple(min(l, dr - kw) for l, _ in spans)


def _gate_windows(w, kw, k0s):
    lead, bw = w.shape[:-3], w.shape[-1]
    tn = V7X_MXU_WIDTH
    w = w.astype(BF16)
    tiles = []
    for n, k0 in enumerate(k0s):
        c0 = n * tn
        h0, h1 = c0 // bw, (c0 + tn - 1) // bw
        nbk = h1 - h0 + 1
        diag = jnp.einsum("...hij,hg->...higj", w[..., h0:h1 + 1, :, :], jnp.eye(nbk, dtype=BF16))
        diag = diag.reshape(*lead, nbk * bw, nbk * bw)
        r_off, c_off = h0 * bw - k0, h0 * bw - c0
        cfg = [(0, 0, 0)] * len(lead) + [(r_off, kw - r_off - nbk * bw, 0), (c_off, tn - c_off - nbk * bw, 0)]
        tiles.append(lax.pad(diag, jnp.zeros((), BF16), cfg))
    return jnp.stack(tiles, axis=-3)


def _gates(zb, z_ref, wa_ref, wx_ref, ba_ref, bx_ref, lam_ref, a_s, b_s, kw, k0s):
    tn = V7X_MXU_WIDTH
    for n, k0 in enumerate(k0s):
        cs = slice(n * tn, (n + 1) * tn)
        zw = zb[:, k0:k0 + kw]
        ta = jnp.tanh(0.5 * (jnp.dot(zw, wa_ref[n], preferred_element_type=F32) + ba_ref[:, cs]))
        tx = jnp.tanh(0.5 * (jnp.dot(zw, wx_ref[n], preferred_element_type=F32) + bx_ref[:, cs]))
        nl = -lam_ref[:, cs]
        half = (0.5 * RG_C) * (jnp.maximum(nl, 0.0) + jnp.log1p(jnp.exp(-jnp.abs(nl))))
        q = ta * half + half
        a = jnp.exp(-q)
        a_s[:, cs] = a
        b_s[:, cs] = jnp.sqrt(jnp.tanh(q) * (1.0 + a * a)) * (0.5 * tx + 0.5) * z_ref[:, cs]


def _scan(a_s, b_s, dst, carry, reverse):
    tt, dr = a_s.shape
    tn = V7X_MXU_WIDTH
    sub = V7X_SUBLANES
    ng = tt // sub
    row = lax.broadcasted_iota(jnp.int32, (sub, tn), 0)

    def body(i, c):
        g = (ng - 1 - i) if reverse else i
        rs = pl.ds(pl.multiple_of(g * sub, sub), sub)
        for n in range(dr // tn):
            cs = slice(n * tn, (n + 1) * tn)
            a8 = a_s[rs, cs]
            b8 = b_s[rs, cs]
            for s in (1, 2, 4):
                if reverse:
                    ok = row < sub - s
                    a_sh = pltpu.roll(a8, sub - s, 0)
                    b_sh = pltpu.roll(b8, sub - s, 0)
                else:
                    ok = row >= s
                    a_sh = pltpu.roll(a8, s, 0)
                    b_sh = pltpu.roll(b8, s, 0)
                b8 = a8 * jnp.where(ok, b_sh, 0.0) + b8
                a8 = a8 * jnp.where(ok, a_sh, 1.0)
            h8 = a8 * carry[:, cs] + b8
            dst[rs, cs] = h8
            edge = h8[0:1, :] if reverse else h8[sub - 1:sub, :]
            carry[:, cs] = jnp.broadcast_to(edge, (sub, tn))
        return c

    lax.fori_loop(0, ng, body, 0)


def _scan_fwd_kernel(zc_ref, zp_ref, zn_ref, cw_ref, cb_ref, wa_ref, wx_ref, ba_ref, bx_ref, lam_ref,
                     hf_ref, z_ref, xpad, zb, a_s, b_s, carry, *, kw, k0s, ns):
    tt = zc_ref.shape[0]
    tn = V7X_MXU_WIDTH
    sub = V7X_SUBLANES
    t = pl.program_id(1)

    @pl.when(t == 0)
    def _():
        carry[...] = jnp.zeros_like(carry)

    first = t <= 1
    last = jnp.logical_or(t == 0, t == ns)
    xpad[0:sub, :] = jnp.where(first, 0.0, zp_ref[...])
    xpad[sub:sub + tt, :] = zc_ref[...]
    xpad[sub + tt:2 * sub + tt, :] = jnp.where(last, 0.0, zn_ref[...])
    for n in range(len(k0s)):
        cs = slice(n * tn, (n + 1) * tn)
        xa = xpad[:, cs]
        acc = None
        for j in range(CONV_W):
            shift = (CONV_LEFT - j) % xa.shape[0]
            xs = pltpu.roll(xa, shift, 0) if shift else xa
            term = xs[sub:sub + tt, :] * cw_ref[j:j + 1, cs]
            acc = term if acc is None else acc + term
        z = acc + cb_ref[:, cs]
        z_ref[:, cs] = z
        zb[:, cs] = z.astype(BF16)

    _gates(zb, z_ref, wa_ref, wx_ref, ba_ref, bx_ref, lam_ref, a_s, b_s, kw, k0s)
    _scan(a_s, b_s, hf_ref, carry, reverse=False)


def _scan_bwd_kernel(z_ref, wa_ref, wx_ref, ba_ref, bx_ref, lam_ref, hf_ref, y_ref, o_ref, zb, a_s, b_s, h_s, carry,
                     *, kw, k0s):
    tn = V7X_MXU_WIDTH

    @pl.when(pl.program_id(1) == 0)
    def _():
        carry[...] = jnp.zeros_like(carry)

    zb[...] = z_ref[...].astype(BF16)
    _gates(zb, z_ref, wa_ref, wx_ref, ba_ref, bx_ref, lam_ref, a_s, b_s, kw, k0s)
    _scan(a_s, b_s, h_s, carry, reverse=True)
    for n in range(len(k0s)):
        cs = slice(n * tn, (n + 1) * tn)
        o_ref[:, cs] = (y_ref[:, cs].astype(F32) * (hf_ref[:, cs] + h_s[:, cs])).astype(BF16)


def _scan_calls(zpre, y, conv_w, conv_b, wa, wx, ba, bx, lam, kw, k0s, batch, seq, ctx_len):
    r, dr = zpre.shape
    tt = ctx_len
    ns = seq // tt
    nx = batch * seq // tt
    g8 = tt // V7X_SUBLANES
    nblk8 = r // V7X_SUBLANES
    nt = len(k0s)

    def cur(reverse):
        def f(b, t):
            st = (ns - t) if reverse else (t - 1)
            return jnp.where(t == 0, nx + b, b * ns + st)
        return f

    def tile(reverse):
        return pl.BlockSpec((tt, dr), lambda b, t: (cur(reverse)(b, t), 0))

    fw = cur(False)
    halo_p = pl.BlockSpec((V7X_SUBLANES, dr), lambda b, t: (jnp.maximum(fw(b, t) * g8 - 1, 0), 0))
    halo_n = pl.BlockSpec((V7X_SUBLANES, dr), lambda b, t: (jnp.minimum(fw(b, t) * g8 + g8, nblk8 - 1), 0))
    vec = lambda d: pl.BlockSpec((None, 1, dr), lambda b, t: (d, 0, 0))
    wspec = lambda d: pl.BlockSpec((None, nt, kw, V7X_MXU_WIDTH), lambda b, t: (d, 0, 0, 0),
                                   pipeline_mode=pl.Buffered(1))
    ba, bx, lam = (v.reshape(2, 1, dr) for v in (ba, bx, lam))
    gate_specs = lambda d: [wspec(d), wspec(d), vec(d), vec(d), vec(d)]
    gate_args = [wa, wx, ba, bx, lam]
    work = [
        pltpu.VMEM((tt, dr), BF16),
        pltpu.VMEM((tt, dr), F32),
        pltpu.VMEM((tt, dr), F32),
    ]
    state = pltpu.VMEM((V7X_SUBLANES, dr), F32)
    hf, z = pl.pallas_call(
        functools.partial(_scan_fwd_kernel, kw=kw, k0s=k0s, ns=ns),
        grid=(batch, ns + 1),
        in_specs=[tile(False), halo_p, halo_n, pl.BlockSpec((CONV_W, dr), lambda b, t: (0, 0)),
                  pl.BlockSpec((1, dr), lambda b, t: (0, 0))] + gate_specs(0),
        out_specs=[tile(False), tile(False)],
        out_shape=[jax.ShapeDtypeStruct((r, dr), F32), jax.ShapeDtypeStruct((r, dr), F32)],
        scratch_shapes=[pltpu.VMEM((tt + 2 * V7X_SUBLANES, dr), F32)] + work + [state],
        compiler_params=_params("arbitrary", "arbitrary"),
        name="rec_scan_fwd",
    )(zpre, zpre, zpre, conv_w, conv_b.reshape(1, dr), *gate_args)
    return pl.pallas_call(
        functools.partial(_scan_bwd_kernel, kw=kw, k0s=k0s),
        grid=(batch, ns + 1),
        in_specs=[tile(True)] + gate_specs(1) + [tile(True), tile(True)],
        out_specs=tile(True),
        out_shape=jax.ShapeDtypeStruct((r, dr), BF16),
        scratch_shapes=work + [pltpu.VMEM((tt, dr), F32), state],
        compiler_params=_params("arbitrary", "arbitrary"),
        name="rec_scan_bwd",
    )(z, *gate_args, hf, y)


def _rope_tables(seq, pad_rows):
    rows = seq // GRID_W
    row = jnp.repeat(jnp.arange(rows, dtype=F32), GRID_W)
    col = jnp.tile(jnp.arange(GRID_W, dtype=F32), rows)
    axis_dim = HEAD_DIM // 2
    inv_freq = ROPE_BASE ** (-jnp.arange(0, axis_dim, 2, dtype=F32) / axis_dim)
    ang = jnp.concatenate([row[:, None] * inv_freq, col[:, None] * inv_freq], axis=-1)
    cos, sin = jnp.cos(ang), jnp.sin(ang)
    cos2 = jnp.concatenate([cos, cos], axis=-1)
    sin2 = jnp.concatenate([-sin, sin], axis=-1)
    cos2 = jnp.concatenate([cos2, jnp.ones((pad_rows, HEAD_DIM), F32)], axis=0)
    sin2 = jnp.concatenate([sin2, jnp.zeros((pad_rows, HEAD_DIM), F32)], axis=0)
    return cos2, sin2


def _qkv_kernel(u_ref, w_ref, cos_ref, sin_ref, o_ref, *, n_q_tiles, n_rope_tiles, scale):
    j = pl.program_id(1)
    p = jnp.dot(u_ref[...], w_ref[...], preferred_element_type=F32)

    @pl.when(j < n_rope_tiles)
    def _():
        c = cos_ref[...]
        s = sin_ref[...]
        mul = jnp.where(j < n_q_tiles, scale, 1.0)
        for h in range(p.shape[1] // HEAD_DIM):
            hs = slice(h * HEAD_DIM, (h + 1) * HEAD_DIM)
            t = p[:, hs]
            o_ref[:, hs] = ((t * c + pltpu.roll(t, HEAD_DIM // 2, 1) * s) * mul).astype(BF16)

    @pl.when(j >= n_rope_tiles)
    def _():
        o_ref[...] = p.astype(BF16)


def _qkv_call(u, w, widx, cos2, sin2, seq, rows_x, tm):
    r, d = u.shape
    n = w.shape[2]
    nkv = N_KV_HEADS * HEAD_DIM
    tn = nkv
    nq = n - 2 * nkv
    n_seq_tiles = seq // tm
    tab = pl.BlockSpec((tm, HEAD_DIM), lambda i, j: (jnp.where(i < rows_x // tm, i % n_seq_tiles, n_seq_tiles), 0))
    return pl.pallas_call(
        functools.partial(_qkv_kernel, n_q_tiles=nq // tn, n_rope_tiles=(nq + nkv) // tn, scale=HEAD_DIM ** -0.5),
        grid=(r // tm, n // tn),
        in_specs=[pl.BlockSpec((tm, d), lambda i, j: (i, 0)), pl.BlockSpec((None, d, tn), lambda i, j: (widx, 0, j)),
                  tab, tab],
        out_specs=pl.BlockSpec((tm, tn), lambda i, j: (i, j)),
        out_shape=jax.ShapeDtypeStruct((r, n), BF16),
        compiler_params=_params("arbitrary", "arbitrary"),
        name="att_qkv",
    )(u, w, cos2, sin2)


def _attn_kernel(sink_ref, q_ref, kp_ref, kc_ref, kn_ref, vp_ref, vc_ref, vn_ref, kx_ref, vx_ref, o_ref,
                 k_all, v_all, bias, *, nb, g):
    j = pl.program_id(1)
    blk = WINDOW
    band = 3 * blk
    rows = g * blk
    is_lat = j < nb
    for dst, srcs in ((k_all, (kp_ref, kc_ref, kn_ref)), (v_all, (vp_ref, vc_ref, vn_ref))):
        for o, src in enumerate(srcs):
            dst[o * blk:(o + 1) * blk, :] = src[...]
    k_all[band:, :] = kx_ref[...]
    v_all[band:, :] = vx_ref[...]
    pen_p = jnp.where(jnp.logical_and(is_lat, j > 0), 0.0, NEG_INF)
    pen_c = jnp.where(is_lat, 0.0, NEG_INF)
    pen_n = jnp.where(jnp.logical_and(is_lat, j < nb - 1), 0.0, NEG_INF)
    qi = lax.broadcasted_iota(jnp.int32, (rows, blk), 0) & (blk - 1)
    ki = lax.broadcasted_iota(jnp.int32, (rows, blk), 1)
    bias[:, 0:blk] = jnp.where(ki >= qi, pen_p, NEG_INF)
    bias[:, blk:2 * blk] = jnp.full((rows, blk), pen_c, F32)
    bias[:, 2 * blk:band] = jnp.where(ki <= qi, pen_n, NEG_INF)
    bias[:, band:] = jnp.zeros((rows, bias.shape[1] - band), F32)
    nt_dims = (((1,), (1,)), ((), ()))
    for kh in range(N_KV_HEADS):
        ks = slice(kh * HEAD_DIM, (kh + 1) * HEAD_DIM)
        heads = [kh * g + gi for gi in range(g)]
        qs = jnp.concatenate([q_ref[:, h * HEAD_DIM:(h + 1) * HEAD_DIM] for h in heads], axis=0)
        sk = jnp.concatenate([jnp.full((blk, 1), sink_ref[h], F32) for h in heads], axis=0)
        s = lax.dot_general(qs, k_all[:, ks], nt_dims, preferred_element_type=F32) + bias[...]
        m = jnp.maximum(jnp.max(s, -1, keepdims=True), sk)
        p = jnp.exp(s - m)
        denom = jnp.exp(sk - m) + jnp.sum(p, -1, keepdims=True)
        o = jnp.dot(p.astype(BF16), v_all[:, ks], preferred_element_type=F32) / denom
        for gi, h in enumerate(heads):
            o_ref[:, h * HEAD_DIM:(h + 1) * HEAD_DIM] = o[gi * blk:(gi + 1) * blk].astype(BF16)


def _attn_call(qkv, sink, batch, seq, ctx_len, ctx_queries):
    r, n = qkv.shape
    nkv = N_KV_HEADS * HEAD_DIM
    d = n - 2 * nkv
    g = d // HEAD_DIM // N_KV_HEADS
    blk = WINDOW
    nb = seq // blk
    ncb = ctx_len // blk
    kcol = d // nkv
    x0 = batch * seq

    def qrow(b, j):
        return jnp.where(j < nb, b * nb + j, x0 // blk + b * ncb + (j - nb))

    def krow(b, j, off):
        return b * nb + jnp.clip(jnp.minimum(j, nb - 1) + off, 0, nb - 1)

    qspec = pl.BlockSpec((blk, d), lambda b, j: (qrow(b, j), 0))
    band = lambda off, col: pl.BlockSpec((blk, nkv), lambda b, j: (krow(b, j, off), col))
    ctxs = lambda col: pl.BlockSpec((ctx_len, nkv), lambda b, j: (x0 // ctx_len + b, col))
    return pl.pallas_call(
        functools.partial(_attn_kernel, nb=nb, g=g),
        grid=(batch, nb + ncb if ctx_queries else nb),
        in_specs=[
            pl.BlockSpec(memory_space=pltpu.SMEM),
            qspec,
            band(-1, kcol), band(0, kcol), band(1, kcol),
            band(-1, kcol + 1), band(0, kcol + 1), band(1, kcol + 1),
            ctxs(kcol), ctxs(kcol + 1),
        ],
        out_specs=qspec,
        out_shape=jax.ShapeDtypeStruct((r if ctx_queries else x0, d), BF16),
        scratch_shapes=[
            pltpu.VMEM((3 * blk + ctx_len, nkv), BF16),
            pltpu.VMEM((3 * blk + ctx_len, nkv), BF16),
            pltpu.VMEM((g * blk, 3 * blk + ctx_len), F32),
        ],
        compiler_params=_params("arbitrary", "arbitrary"),
        name="att_core",
    )(sink, qkv, qkv, qkv, qkv, qkv, qkv, qkv, qkv, qkv)


def kernel(x, c, ctx, c_ctx, mod_w, mod_b, ln_mix_g, ln_mix_b, ln_ffn_g, ln_ffn_b, ffn_w_gate, ffn_w_up, ffn_w_down,
           rec_w_in, rec_conv_w, rec_conv_b, rec_gate_a_w, rec_gate_a_b, rec_gate_x_w, rec_gate_x_b, rec_lambda,
           rec_w_out, att_w_qkv, att_sink, att_w_o):
    batch, seq, d = x.shape
    ctx_len = ctx.shape[1]
    depth = mod_w.shape[0]
    rows_x, rows_c = batch * seq, batch * ctx_len
    assert seq % ctx_len == 0 and ctx_len % WINDOW == 0 and seq % GRID_W == 0
    assert d % (N_KV_HEADS * HEAD_DIM) == 0
    alpha = (2.0 * depth) ** 0.25

    tm = _pick(seq, (512, 256, 128))
    while rows_c % tm:
        tm //= 2
    tm_big = 2 * tm if (seq % (2 * tm) == 0 and rows_c % (2 * tm) == 0) else tm

    def row_fn_for(t):
        return lambda i: jnp.minimum((i * t) // seq, batch)

    row_fn, row_fn_big = row_fn_for(tm), row_fn_for(tm_big)

    mr = -(-(batch + 1) // V7X_SUBLANES) * V7X_SUBLANES
    cs = jnp.zeros((mr, d), F32).at[:batch].set(c).at[batch].set(c_ctx)
    mods5 = _mods_call(cs, mod_w, mod_b).reshape(depth, mr, N_MOD, 1, d)

    cos2, sin2 = _rope_tables(seq, tm_big)
    dr = rec_w_out.shape[1]
    kw, k0s = _gate_window_plan(dr, dr // RNN_BLOCKS)
    wa_all = _gate_windows(rec_gate_a_w, kw, k0s)
    wx_all = _gate_windows(rec_gate_x_w, kw, k0s)

    w_gate, w_up, w_down = ffn_w_gate.astype(BF16), ffn_w_up.astype(BF16), ffn_w_down.astype(BF16)
    w_in, w_out = rec_w_in.astype(BF16), rec_w_out.astype(BF16)
    w_qkv, w_o = att_w_qkv.astype(BF16), att_w_o.astype(BF16)

    h, u = _modulate_call(x.reshape(rows_x, d), ctx.reshape(rows_c, d), mods5, 0, row_fn, tm)
    for i in range(depth):
        j = i // N_MIXERS
        last = i == depth - 1
        rows = rows_x if last else rows_x + rows_c
        if i % N_MIXERS == 0:
            y, zpre = _win_call(u, w_in, j, tm)
            yh = _scan_calls(zpre, y, rec_conv_w[j], rec_conv_b[j], wa_all[j], wx_all[j], rec_gate_a_b[j],
                             rec_gate_x_b[j], rec_lambda[j], kw, k0s, batch, seq, ctx_len)
            h, u = _proj_ln_call(yh, w_out, j, h, rows, mods5, i, ln_mix_g[i], ln_mix_b[i], row_fn, tm, alpha, "rec_out")
        else:
            qkv = _qkv_call(u, w_qkv, j, cos2, sin2, seq, rows_x, tm_big)
            ao = _attn_call(qkv, att_sink[j], batch, seq, ctx_len, ctx_queries=not last)
            h, u = _proj_ln_call(ao, w_o, j, h, rows, mods5, i, ln_mix_g[i], ln_mix_b[i], row_fn, tm, alpha, "att_out")
        h, u = _ffn_call(u, w_gate, w_up, w_down, h, rows, mods5, i, None if last else i + 1, ln_ffn_g[i], ln_ffn_b[i],
                         row_fn, tm, alpha)
    return h.reshape(batch, seq, d)
```

```python
import functools

import jax
import jax.numpy as jnp
from jax import lax
from jax.experimental import pallas as pl
from jax.experimental.pallas import tpu as pltpu

HEAD_DIM = 128
N_KV_HEADS = 4
WINDOW = 128
GRID_W = 64
ROPE_BASE = 10000.0
RNN_BLOCKS = 16
CONV_W = 4
CONV_LEFT = 2
RG_C = 8.0
LN_EPS = 1e-5
NEG_INF = -1e30
N_MIXERS = 2
N_MOD = 6

V7X_LANES = 128
V7X_SUBLANES = 8
V7X_MXU_WIDTH = 256
V7X_VMEM_BYTES = 64 * 1024 * 1024
VMEM_LIMIT_BYTES = V7X_VMEM_BYTES - 8 * 1024 * 1024

F32 = jnp.float32
BF16 = jnp.bfloat16
EPILOGUE_ROWS = 128


def _pick(n, cands):
    for c in cands:
        if n % c == 0:
            return c
    raise ValueError(f"no tile in {cands} divides {n}")


def _params(*sem):
    return pltpu.CompilerParams(dimension_semantics=sem, vmem_limit_bytes=VMEM_LIMIT_BYTES)


def _mod_spec(layer, chunk, row_fn):
    return lambda d: pl.BlockSpec((None, None, None, 1, d), lambda *g: (layer, row_fn(*g), chunk, 0, 0))


def _layer_norm(v, g, b):
    mu = jnp.mean(v, axis=-1, keepdims=True)
    d = v - mu
    var = jnp.mean(d * d, axis=-1, keepdims=True)
    return d * lax.rsqrt(var + LN_EPS) * g + b


def _deepnorm_epilogue(acc_ref, res_ref, gate_ref, lg_ref, lb_ref, h_ref, alpha, u_ref=None, sh_ref=None, sc_ref=None,
                       rows=None):
    r0, r1 = rows if rows is not None else (0, acc_ref.shape[0])
    ch = min(r1 - r0, EPILOGUE_ROWS)
    gate, lg, lb = gate_ref[...], lg_ref[...], lb_ref[...]
    if u_ref is not None:
        sh, sc1 = sh_ref[...], 1.0 + sc_ref[...]
    for k in range((r1 - r0) // ch):
        rs = slice(r0 + k * ch, r0 + (k + 1) * ch)
        hn = _layer_norm(alpha * res_ref[rs, :] + gate * acc_ref[rs, :], lg, lb)
        h_ref[rs, :] = hn
        if u_ref is not None:
            u_ref[rs, :] = (hn * sc1 + sh).astype(BF16)


def _mods_kernel(cs_ref, w_ref, b_ref, o_ref):
    s = cs_ref[...]
    s = (s * jax.nn.sigmoid(s)).astype(BF16)
    o_ref[...] = jnp.dot(s, w_ref[...].astype(BF16), preferred_element_type=F32) + b_ref[...]


def _mods_call(cs, mod_w, mod_b):
    depth, d, n = mod_w.shape
    mr = cs.shape[0]
    tn = _pick(n, (1024, 512, 256, 128))
    return pl.pallas_call(
        _mods_kernel,
        grid=(depth, n // tn),
        in_specs=[
            pl.BlockSpec((mr, d), lambda l, j: (0, 0)),
            pl.BlockSpec((None, d, tn), lambda l, j: (l, 0, j)),
            pl.BlockSpec((None, 1, tn), lambda l, j: (l, 0, j)),
        ],
        out_specs=pl.BlockSpec((None, mr, tn), lambda l, j: (l, 0, j)),
        out_shape=jax.ShapeDtypeStruct((depth, mr, n), F32),
        compiler_params=_params("arbitrary", "arbitrary"),
        name="mods",
    )(cs, mod_w, mod_b.reshape(depth, 1, n))


def _modulate_kernel(x_ref, c_ref, sh_ref, sc_ref, h_ref, u_ref, *, n_lat):
    def emit(src_ref):
        v = src_ref[...]
        h_ref[...] = v
        u_ref[...] = (v * (1.0 + sc_ref[...]) + sh_ref[...]).astype(BF16)

    pl.when(pl.program_id(0) < n_lat)(lambda: emit(x_ref))
    pl.when(pl.program_id(0) >= n_lat)(lambda: emit(c_ref))


def _modulate_call(x2, c2, mods5, layer, row_fn, tm):
    d = x2.shape[1]
    n_lat, n_ctx = x2.shape[0] // tm, c2.shape[0] // tm
    r = x2.shape[0] + c2.shape[0]
    tile = pl.BlockSpec((tm, d), lambda i: (i, 0))
    return pl.pallas_call(
        functools.partial(_modulate_kernel, n_lat=n_lat),
        grid=(n_lat + n_ctx,),
        in_specs=[
            pl.BlockSpec((tm, d), lambda i: (jnp.minimum(i, n_lat - 1), 0)),
            pl.BlockSpec((tm, d), lambda i: (jnp.maximum(i - n_lat, 0), 0)),
            _mod_spec(layer, 0, row_fn)(d),
            _mod_spec(layer, 1, row_fn)(d),
        ],
        out_specs=[tile, tile],
        out_shape=[jax.ShapeDtypeStruct((r, d), F32), jax.ShapeDtypeStruct((r, d), BF16)],
        compiler_params=_params("arbitrary"),
        name="modulate0",
    )(x2, c2, mods5, mods5)


def _proj_ln_kernel(a_ref, w_ref, res_ref, gate_ref, sh_ref, sc_ref, lg_ref, lb_ref, h_ref, u_ref, acc, *, alpha):
    tm = a_ref.shape[0]
    halves = 2 if tm % (2 * EPILOGUE_ROWS) == 0 else 1
    for hh in range(halves):
        rows = (hh * tm // halves, (hh + 1) * tm // halves)
        acc[rows[0]:rows[1], :] = jnp.dot(a_ref[rows[0]:rows[1], :], w_ref[...], preferred_element_type=F32)
        _deepnorm_epilogue(acc, res_ref, gate_ref, lg_ref, lb_ref, h_ref, alpha, u_ref, sh_ref, sc_ref, rows=rows)


def _proj_ln_call(a, w, widx, res, rows, mods5, layer, ln_g, ln_b, row_fn, tm, alpha, name):
    r, k = rows, a.shape[1]
    d = w.shape[2]
    vec = pl.BlockSpec((1, d), lambda i: (0, 0))
    return pl.pallas_call(
        functools.partial(_proj_ln_kernel, alpha=alpha),
        grid=(r // tm,),
        in_specs=[
            pl.BlockSpec((tm, k), lambda i: (i, 0)),
            pl.BlockSpec((None, k, d), lambda i: (widx, 0, 0), pipeline_mode=pl.Buffered(1)),
            pl.BlockSpec((tm, d), lambda i: (i, 0)),
            _mod_spec(layer, 2, row_fn)(d),
            _mod_spec(layer, 3, row_fn)(d),
            _mod_spec(layer, 4, row_fn)(d),
            vec,
            vec,
        ],
        out_specs=[pl.BlockSpec((tm, d), lambda i: (i, 0)), pl.BlockSpec((tm, d), lambda i: (i, 0))],
        out_shape=[jax.ShapeDtypeStruct((r, d), F32), jax.ShapeDtypeStruct((r, d), BF16)],
        scratch_shapes=[pltpu.VMEM((tm, d), F32)],
        compiler_params=_params("arbitrary"),
        name=name,
    )(a, w, res, mods5, mods5, mods5, ln_g.reshape(1, d), ln_b.reshape(1, d))


def _ffn_kernel(*refs, alpha, with_u):
    if with_u:
        u_ref, wg_ref, wu_ref, wd_ref, res_ref, gate_ref, sh_ref, sc_ref, lg_ref, lb_ref, h_ref, un_ref, acc = refs
    else:
        u_ref, wg_ref, wu_ref, wd_ref, res_ref, gate_ref, lg_ref, lb_ref, h_ref, acc = refs
    j = pl.program_id(1)

    @pl.when(j == 0)
    def _():
        acc[...] = jnp.zeros_like(acc)

    u = u_ref[...]
    g = jnp.dot(u, wg_ref[...], preferred_element_type=F32)
    up = jnp.dot(u, wu_ref[...], preferred_element_type=F32)
    hid = (g * jax.nn.sigmoid(g) * up).astype(BF16)
    acc[...] += jnp.dot(hid, wd_ref[...], preferred_element_type=F32)

    @pl.when(j == pl.num_programs(1) - 1)
    def _():
        if with_u:
            _deepnorm_epilogue(acc, res_ref, gate_ref, lg_ref, lb_ref, h_ref, alpha, un_ref, sh_ref, sc_ref)
        else:
            _deepnorm_epilogue(acc, res_ref, gate_ref, lg_ref, lb_ref, h_ref, alpha)


def _ffn_call(u, wg, wu, wd, res, rows, mods5, layer, next_layer, ln_g, ln_b, row_fn, tm, alpha):
    r, d = rows, u.shape[1]
    hid = wg.shape[2]
    th = _pick(hid, (512, 256, 128))
    with_u = next_layer is not None
    row2 = lambda i, j: row_fn(i)
    vec = pl.BlockSpec((1, d), lambda i, j: (0, 0))
    tile = pl.BlockSpec((tm, d), lambda i, j: (i, 0))
    in_specs = [
        tile,
        pl.BlockSpec((None, d, th), lambda i, j: (layer, 0, j)),
        pl.BlockSpec((None, d, th), lambda i, j: (layer, 0, j)),
        pl.BlockSpec((None, th, d), lambda i, j: (layer, j, 0)),
        tile,
        _mod_spec(layer, 5, row2)(d),
    ]
    args = [u, wg, wu, wd, res, mods5]
    if with_u:
        in_specs += [_mod_spec(next_layer, 0, row2)(d), _mod_spec(next_layer, 1, row2)(d)]
        args += [mods5, mods5]
    in_specs += [vec, vec]
    args += [ln_g.reshape(1, d), ln_b.reshape(1, d)]
    out_specs = [tile]
    out_shape = [jax.ShapeDtypeStruct((r, d), F32)]
    if with_u:
        out_specs.append(tile)
        out_shape.append(jax.ShapeDtypeStruct((r, d), BF16))
    out = pl.pallas_call(
        functools.partial(_ffn_kernel, alpha=alpha, with_u=with_u),
        grid=(r // tm, hid // th),
        in_specs=in_specs,
        out_specs=out_specs,
        out_shape=out_shape,
        scratch_shapes=[pltpu.VMEM((tm, d), F32)],
        compiler_params=_params("arbitrary", "arbitrary"),
        name="ffn",
    )(*args)
    return (out[0], out[1]) if with_u else (out[0], None)


def _gelu_tanh(x):
    return x * (0.5 * (1.0 + jnp.tanh(0.7978845608028654 * (x + 0.044715 * (x * x * x)))))


def _col_groups(n, width=2 * V7X_MXU_WIDTH):
    return [(c, min(c + width, n)) for c in range(0, n, width)]


def _win_kernel(u_ref, w_ref, y_ref, z_ref):
    u = u_ref[...]
    dr = y_ref.shape[1]
    for c0, c1 in _col_groups(dr):
        y_ref[:, c0:c1] = _gelu_tanh(jnp.dot(u, w_ref[:, c0:c1], preferred_element_type=F32)).astype(BF16)
    for c0, c1 in _col_groups(dr):
        z_ref[:, c0:c1] = jnp.dot(u, w_ref[:, dr + c0:dr + c1], preferred_element_type=F32)


def _win_call(u, w_in, widx, tm):
    r, d = u.shape
    n = w_in.shape[2]
    dr = n // 2
    return pl.pallas_call(
        _win_kernel,
        grid=(r // tm,),
        in_specs=[
            pl.BlockSpec((tm, d), lambda i: (i, 0)),
            pl.BlockSpec((None, d, n), lambda i: (widx, 0, 0), pipeline_mode=pl.Buffered(1)),
        ],
        out_specs=[pl.BlockSpec((tm, dr), lambda i: (i, 0)), pl.BlockSpec((tm, dr), lambda i: (i, 0))],
        out_shape=[jax.ShapeDtypeStruct((r, dr), BF16), jax.ShapeDtypeStruct((r, dr), F32)],
        compiler_params=_params("arbitrary"),
        name="rec_in",
    )(u, w_in)


def _gate_window_plan(dr, bw):
    tn = V7X_MXU_WIDTH
    spans = []
    for n in range(dr // tn):
        c0 = n * tn
        lo = (c0 // bw) * bw
        hi = ((c0 + tn - 1) // bw + 1) * bw
        spans.append(((lo // V7X_LANES) * V7X_LANES, -(-hi // V7X_LANES) * V7X_LANES))
    kw = max(h - l for l, h in spans)
    return kw, tuple(min(l, dr - kw) for l, _ in spans)


def _gate_windows(w, kw, k0s):
    lead, bw = w.shape[:-3], w.shape[-1]
    tn = V7X_MXU_WIDTH
    w = w.astype(BF16)
    tiles = []
    for n, k0 in enumerate(k0s):
        c0 = n * tn
        h0, h1 = c0 // bw, (c0 + tn - 1) // bw
        nbk = h1 - h0 + 1
        diag = jnp.einsum("...hij,hg->...higj", w[..., h0:h1 + 1, :, :], jnp.eye(nbk, dtype=BF16))
        diag = diag.reshape(*lead, nbk * bw, nbk * bw)
        r_off, c_off = h0 * bw - k0, h0 * bw - c0
        cfg = [(0, 0, 0)] * len(lead) + [(r_off, kw - r_off - nbk * bw, 0), (c_off, tn - c_off - nbk * bw, 0)]
        tiles.append(lax.pad(diag, jnp.zeros((), BF16), cfg))
    return jnp.stack(tiles, axis=-3)


def _gates(zb, z_ref, wa_ref, wx_ref, ba_ref, bx_ref, lam_ref, a_s, b_s, kw, k0s):
    tn = V7X_MXU_WIDTH
    for n, k0 in enumerate(k0s):
        cs = slice(n * tn, (n + 1) * tn)
        zw = zb[:, k0:k0 + kw]
        ta = jnp.tanh(0.5 * (jnp.dot(zw, wa_ref[n], preferred_element_type=F32) + ba_ref[:, cs]))
        tx = jnp.tanh(0.5 * (jnp.dot(zw, wx_ref[n], preferred_element_type=F32) + bx_ref[:, cs]))
        nl = -lam_ref[:, cs]
        half = (0.5 * RG_C) * (jnp.maximum(nl, 0.0) + jnp.log1p(jnp.exp(-jnp.abs(nl))))
        q = ta * half + half
        a = jnp.exp(-q)
        a_s[:, cs] = a
        b_s[:, cs] = jnp.sqrt(jnp.tanh(q) * (1.0 + a * a)) * (0.5 * tx + 0.5) * z_ref[:, cs]


def _scan(a_s, b_s, dst, carry, reverse):
    tt, dr = a_s.shape
    tn = V7X_MXU_WIDTH
    sub = V7X_SUBLANES
    ng = tt // sub
    row = lax.broadcasted_iota(jnp.int32, (sub, tn), 0)

    def body(i, c):
        g = (ng - 1 - i) if reverse else i
        rs = pl.ds(pl.multiple_of(g * sub, sub), sub)
        for n in range(dr // tn):
            cs = slice(n * tn, (n + 1) * tn)
            a8 = a_s[rs, cs]
            b8 = b_s[rs, cs]
            for s in (1, 2, 4):
                if reverse:
                    ok = row < sub - s
                    a_sh = pltpu.roll(a8, sub - s, 0)
                    b_sh = pltpu.roll(b8, sub - s, 0)
                else:
                    ok = row >= s
                    a_sh = pltpu.roll(a8, s, 0)
                    b_sh = pltpu.roll(b8, s, 0)
                b8 = a8 * jnp.where(ok, b_sh, 0.0) + b8
                a8 = a8 * jnp.where(ok, a_sh, 1.0)
            h8 = a8 * carry[:, cs] + b8
            dst[rs, cs] = h8
            edge = h8[0:1, :] if reverse else h8[sub - 1:sub, :]
            carry[:, cs] = jnp.broadcast_to(edge, (sub, tn))
        return c

    lax.fori_loop(0, ng, body, 0)


def _scan_fwd_kernel(zc_ref, zp_ref, zn_ref, cw_ref, cb_ref, wa_ref, wx_ref, ba_ref, bx_ref, lam_ref,
                     hf_ref, z_ref, xpad, zb, a_s, b_s, carry, *, kw, k0s, ns):
    tt = zc_ref.shape[0]
    tn = V7X_MXU_WIDTH
    sub = V7X_SUBLANES
    t = pl.program_id(1)

    @pl.when(t == 0)
    def _():
        carry[...] = jnp.zeros_like(carry)

    first = t <= 1
    last = jnp.logical_or(t == 0, t == ns)
    xpad[0:sub, :] = jnp.where(first, 0.0, zp_ref[...])
    xpad[sub:sub + tt, :] = zc_ref[...]
    xpad[sub + tt:2 * sub + tt, :] = jnp.where(last, 0.0, zn_ref[...])
    for n in range(len(k0s)):
        cs = slice(n * tn, (n + 1) * tn)
        xa = xpad[:, cs]
        acc = None
        for j in range(CONV_W):
            shift = (CONV_LEFT - j) % xa.shape[0]
            xs = pltpu.roll(xa, shift, 0) if shift else xa
            term = xs[sub:sub + tt, :] * cw_ref[j:j + 1, cs]
            acc = term if acc is None else acc + term
        z = acc + cb_ref[:, cs]
        z_ref[:, cs] = z
        zb[:, cs] = z.astype(BF16)

    _gates(zb, z_ref, wa_ref, wx_ref, ba_ref, bx_ref, lam_ref, a_s, b_s, kw, k0s)
    _scan(a_s, b_s, hf_ref, carry, reverse=False)


def _scan_bwd_kernel(z_ref, wa_ref, wx_ref, ba_ref, bx_ref, lam_ref, hf_ref, y_ref, o_ref, zb, a_s, b_s, h_s, carry,
                     *, kw, k0s):
    tn = V7X_MXU_WIDTH

    @pl.when(pl.program_id(1) == 0)
    def _():
        carry[...] = jnp.zeros_like(carry)

    zb[...] = z_ref[...].astype(BF16)
    _gates(zb, z_ref, wa_ref, wx_ref, ba_ref, bx_ref, lam_ref, a_s, b_s, kw, k0s)
    _scan(a_s, b_s, h_s, carry, reverse=True)
    for n in range(len(k0s)):
        cs = slice(n * tn, (n + 1) * tn)
        o_ref[:, cs] = (y_ref[:, cs].astype(F32) * (hf_ref[:, cs] + h_s[:, cs])).astype(BF16)


def _scan_calls(zpre, y, conv_w, conv_b, wa, wx, ba, bx, lam, kw, k0s, batch, seq, ctx_len):
    r, dr = zpre.shape
    tt = ctx_len
    ns = seq // tt
    nx = batch * seq // tt
    g8 = tt // V7X_SUBLANES
    nblk8 = r // V7X_SUBLANES
    nt = len(k0s)

    def cur(reverse):
        def f(b, t):
            st = (ns - t) if reverse else (t - 1)
            return jnp.where(t == 0, nx + b, b * ns + st)
        return f

    def tile(reverse):
        return pl.BlockSpec((tt, dr), lambda b, t: (cur(reverse)(b, t), 0))

    fw = cur(False)
    halo_p = pl.BlockSpec((V7X_SUBLANES, dr), lambda b, t: (jnp.maximum(fw(b, t) * g8 - 1, 0), 0))
    halo_n = pl.BlockSpec((V7X_SUBLANES, dr), lambda b, t: (jnp.minimum(fw(b, t) * g8 + g8, nblk8 - 1), 0))
    vec = lambda d: pl.BlockSpec((None, 1, dr), lambda b, t: (d, 0, 0))
    wspec = lambda d: pl.BlockSpec((None, nt, kw, V7X_MXU_WIDTH), lambda b, t: (d, 0, 0, 0),
                                   pipeline_mode=pl.Buffered(1))
    ba, bx, lam = (v.reshape(2, 1, dr) for v in (ba, bx, lam))
    gate_specs = lambda d: [wspec(d), wspec(d), vec(d), vec(d), vec(d)]
    gate_args = [wa, wx, ba, bx, lam]
    work = [
        pltpu.VMEM((tt, dr), BF16),
        pltpu.VMEM((tt, dr), F32),
        pltpu.VMEM((tt, dr), F32),
    ]
    state = pltpu.VMEM((V7X_SUBLANES, dr), F32)
    hf, z = pl.pallas_call(
        functools.partial(_scan_fwd_kernel, kw=kw, k0s=k0s, ns=ns),
        grid=(batch, ns + 1),
        in_specs=[tile(False), halo_p, halo_n, pl.BlockSpec((CONV_W, dr), lambda b, t: (0, 0)),
                  pl.BlockSpec((1, dr), lambda b, t: (0, 0))] + gate_specs(0),
        out_specs=[tile(False), tile(False)],
        out_shape=[jax.ShapeDtypeStruct((r, dr), F32), jax.ShapeDtypeStruct((r, dr), F32)],
        scratch_shapes=[pltpu.VMEM((tt + 2 * V7X_SUBLANES, dr), F32)] + work + [state],
        compiler_params=_params("arbitrary", "arbitrary"),
        name="rec_scan_fwd",
    )(zpre, zpre, zpre, conv_w, conv_b.reshape(1, dr), *gate_args)
    return pl.pallas_call(
        functools.partial(_scan_bwd_kernel, kw=kw, k0s=k0s),
        grid=(batch, ns + 1),
        in_specs=[tile(True)] + gate_specs(1) + [tile(True), tile(True)],
        out_specs=tile(True),
        out_shape=jax.ShapeDtypeStruct((r, dr), BF16),
        scratch_shapes=work + [pltpu.VMEM((tt, dr), F32), state],
        compiler_params=_params("arbitrary", "arbitrary"),
        name="rec_scan_bwd",
    )(z, *gate_args, hf, y)


def _rope_tables(seq, pad_rows):
    rows = seq // GRID_W
    row = jnp.repeat(jnp.arange(rows, dtype=F32), GRID_W)
    col = jnp.tile(jnp.arange(GRID_W, dtype=F32), rows)
    axis_dim = HEAD_DIM // 2
    inv_freq = ROPE_BASE ** (-jnp.arange(0, axis_dim, 2, dtype=F32) / axis_dim)
    ang = jnp.concatenate([row[:, None] * inv_freq, col[:, None] * inv_freq], axis=-1)
    cos, sin = jnp.cos(ang), jnp.sin(ang)
    cos2 = jnp.concatenate([cos, cos], axis=-1)
    sin2 = jnp.concatenate([-sin, sin], axis=-1)
    cos2 = jnp.concatenate([cos2, jnp.ones((pad_rows, HEAD_DIM), F32)], axis=0)
    sin2 = jnp.concatenate([sin2, jnp.zeros((pad_rows, HEAD_DIM), F32)], axis=0)
    return cos2, sin2


def _qkv_kernel(u_ref, w_ref, cos_ref, sin_ref, o_ref, *, nq, nkv, scale):
    u = u_ref[...]
    c = cos_ref[...]
    s = sin_ref[...]
    for c0, c1 in _col_groups(o_ref.shape[1]):
        p = jnp.dot(u, w_ref[:, c0:c1], preferred_element_type=F32)
        for h0 in range(c0, c1, HEAD_DIM):
            t = p[:, h0 - c0:h0 - c0 + HEAD_DIM]
            if h0 < nq + nkv:
                t = t * c + pltpu.roll(t, HEAD_DIM // 2, 1) * s
            if h0 < nq:
                t = t * scale
            o_ref[:, h0:h0 + HEAD_DIM] = t.astype(BF16)


def _qkv_call(u, w, widx, cos2, sin2, seq, rows_x, tm):
    r, d = u.shape
    n = w.shape[2]
    nkv = N_KV_HEADS * HEAD_DIM
    n_seq_tiles = seq // tm
    tab = pl.BlockSpec((tm, HEAD_DIM), lambda i: (jnp.where(i < rows_x // tm, i % n_seq_tiles, n_seq_tiles), 0))
    return pl.pallas_call(
        functools.partial(_qkv_kernel, nq=n - 2 * nkv, nkv=nkv, scale=HEAD_DIM ** -0.5),
        grid=(r // tm,),
        in_specs=[pl.BlockSpec((tm, d), lambda i: (i, 0)),
                  pl.BlockSpec((None, d, n), lambda i: (widx, 0, 0), pipeline_mode=pl.Buffered(1)), tab, tab],
        out_specs=pl.BlockSpec((tm, n), lambda i: (i, 0)),
        out_shape=jax.ShapeDtypeStruct((r, n), BF16),
        compiler_params=_params("arbitrary"),
        name="att_qkv",
    )(u, w, cos2, sin2)


def _attn_kernel(sink_ref, q_ref, kp_ref, kc_ref, kn_ref, vp_ref, vc_ref, vn_ref, kx_ref, vx_ref, o_ref,
                 k_all, v_all, bias, *, nb, g):
    j = pl.program_id(1)
    blk = WINDOW
    band = 3 * blk
    rows = g * blk
    is_lat = j < nb
    for dst, srcs in ((k_all, (kp_ref, kc_ref, kn_ref)), (v_all, (vp_ref, vc_ref, vn_ref))):
        for o, src in enumerate(srcs):
            dst[o * blk:(o + 1) * blk, :] = src[...]
    k_all[band:, :] = kx_ref[...]
    v_all[band:, :] = vx_ref[...]
    pen_p = jnp.where(jnp.logical_and(is_lat, j > 0), 0.0, NEG_INF)
    pen_c = jnp.where(is_lat, 0.0, NEG_INF)
    pen_n = jnp.where(jnp.logical_and(is_lat, j < nb - 1), 0.0, NEG_INF)
    qi = lax.broadcasted_iota(jnp.int32, (rows, blk), 0) & (blk - 1)
    ki = lax.broadcasted_iota(jnp.int32, (rows, blk), 1)
    bias[:, 0:blk] = jnp.where(ki >= qi, pen_p, NEG_INF)
    bias[:, blk:2 * blk] = jnp.full((rows, blk), pen_c, F32)
    bias[:, 2 * blk:band] = jnp.where(ki <= qi, pen_n, NEG_INF)
    bias[:, band:] = jnp.zeros((rows, bias.shape[1] - band), F32)
    nt_dims = (((1,), (1,)), ((), ()))
    for kh in range(N_KV_HEADS):
        ks = slice(kh * HEAD_DIM, (kh + 1) * HEAD_DIM)
        heads = [kh * g + gi for gi in range(g)]
        qs = jnp.concatenate([q_ref[:, h * HEAD_DIM:(h + 1) * HEAD_DIM] for h in heads], axis=0)
        sk = jnp.concatenate([jnp.full((blk, 1), sink_ref[h], F32) for h in heads], axis=0)
        s = lax.dot_general(qs, k_all[:, ks], nt_dims, preferred_element_type=F32) + bias[...]
        m = jnp.maximum(jnp.max(s, -1, keepdims=True), sk)
        p = jnp.exp(s - m)
        denom = jnp.exp(sk - m) + jnp.sum(p, -1, keepdims=True)
        o = jnp.dot(p.astype(BF16), v_all[:, ks], preferred_element_type=F32) / denom
        for gi, h in enumerate(heads):
            o_ref[:, h * HEAD_DIM:(h + 1) * HEAD_DIM] = o[gi * blk:(gi + 1) * blk].astype(BF16)


def _attn_call(qkv, sink, batch, seq, ctx_len, ctx_queries):
    r, n = qkv.shape
    nkv = N_KV_HEADS * HEAD_DIM
    d = n - 2 * nkv
    g = d // HEAD_DIM // N_KV_HEADS
    blk = WINDOW
    nb = seq // blk
    ncb = ctx_len // blk
    kcol = d // nkv
    x0 = batch * seq

    def qrow(b, j):
        return jnp.where(j < nb, b * nb + j, x0 // blk + b * ncb + (j - nb))

    def krow(b, j, off):
        return b * nb + jnp.clip(jnp.minimum(j, nb - 1) + off, 0, nb - 1)

    qspec = pl.BlockSpec((blk, d), lambda b, j: (qrow(b, j), 0))
    band = lambda off, col: pl.BlockSpec((blk, nkv), lambda b, j: (krow(b, j, off), col))
    ctxs = lambda col: pl.BlockSpec((ctx_len, nkv), lambda b, j: (x0 // ctx_len + b, col))
    return pl.pallas_call(
        functools.partial(_attn_kernel, nb=nb, g=g),
        grid=(batch, nb + ncb if ctx_queries else nb),
        in_specs=[
            pl.BlockSpec(memory_space=pltpu.SMEM),
            qspec,
            band(-1, kcol), band(0, kcol), band(1, kcol),
            band(-1, kcol + 1), band(0, kcol + 1), band(1, kcol + 1),
            ctxs(kcol), ctxs(kcol + 1),
        ],
        out_specs=qspec,
        out_shape=jax.ShapeDtypeStruct((r if ctx_queries else x0, d), BF16),
        scratch_shapes=[
            pltpu.VMEM((3 * blk + ctx_len, nkv), BF16),
            pltpu.VMEM((3 * blk + ctx_len, nkv), BF16),
            pltpu.VMEM((g * blk, 3 * blk + ctx_len), F32),
        ],
        compiler_params=_params("arbitrary", "arbitrary"),
        name="att_core",
    )(sink, qkv, qkv, qkv, qkv, qkv, qkv, qkv, qkv, qkv)


def kernel(x, c, ctx, c_ctx, mod_w, mod_b, ln_mix_g, ln_mix_b, ln_ffn_g, ln_ffn_b, ffn_w_gate, ffn_w_up, ffn_w_down,
           rec_w_in, rec_conv_w, rec_conv_b, rec_gate_a_w, rec_gate_a_b, rec_gate_x_w, rec_gate_x_b, rec_lambda,
           rec_w_out, att_w_qkv, att_sink, att_w_o):
    batch, seq, d = x.shape
    ctx_len = ctx.shape[1]
    depth = mod_w.shape[0]
    rows_x, rows_c = batch * seq, batch * ctx_len
    assert seq % ctx_len == 0 and ctx_len % WINDOW == 0 and seq % GRID_W == 0
    assert d % (N_KV_HEADS * HEAD_DIM) == 0
    alpha = (2.0 * depth) ** 0.25

    tm = _pick(seq, (512, 256, 128))
    while rows_c % tm:
        tm //= 2
    tm_big = 2 * tm if (seq % (2 * tm) == 0 and rows_c % (2 * tm) == 0) else tm

    def row_fn_for(t):
        return lambda i: jnp.minimum((i * t) // seq, batch)

    row_fn, row_fn_big = row_fn_for(tm), row_fn_for(tm_big)

    mr = -(-(batch + 1) // V7X_SUBLANES) * V7X_SUBLANES
    cs = jnp.zeros((mr, d), F32).at[:batch].set(c).at[batch].set(c_ctx)
    mods5 = _mods_call(cs, mod_w, mod_b).reshape(depth, mr, N_MOD, 1, d)

    cos2, sin2 = _rope_tables(seq, tm_big)
    dr = rec_w_out.shape[1]
    kw, k0s = _gate_window_plan(dr, dr // RNN_BLOCKS)
    wa_all = _gate_windows(rec_gate_a_w, kw, k0s)
    wx_all = _gate_windows(rec_gate_x_w, kw, k0s)

    w_gate, w_up, w_down = ffn_w_gate.astype(BF16), ffn_w_up.astype(BF16), ffn_w_down.astype(BF16)
    w_in, w_out = rec_w_in.astype(BF16), rec_w_out.astype(BF16)
    w_qkv, w_o = att_w_qkv.astype(BF16), att_w_o.astype(BF16)

    h, u = _modulate_call(x.reshape(rows_x, d), ctx.reshape(rows_c, d), mods5, 0, row_fn, tm)
    for i in range(depth):
        j = i // N_MIXERS
        last = i == depth - 1
        rows = rows_x if last else rows_x + rows_c
        if i % N_MIXERS == 0:
            y, zpre = _win_call(u, w_in, j, tm)
            yh = _scan_calls(zpre, y, rec_conv_w[j], rec_conv_b[j], wa_all[j], wx_all[j], rec_gate_a_b[j],
                             rec_gate_x_b[j], rec_lambda[j], kw, k0s, batch, seq, ctx_len)
            h, u = _proj_ln_call(yh, w_out, j, h, rows, mods5, i, ln_mix_g[i], ln_mix_b[i], row_fn, tm, alpha, "rec_out")
        else:
            qkv = _qkv_call(u, w_qkv, j, cos2, sin2, seq, rows_x, tm_big)
            ao = _attn_call(qkv, att_sink[j], batch, seq, ctx_len, ctx_queries=not last)
            h, u = _proj_ln_call(ao, w_o, j, h, rows, mods5, i, ln_mix_g[i], ln_mix_b[i], row_fn, tm, alpha, "att_out")
        h, u = _ffn_call(u, w_gate, w_up, w_down, h, rows, mods5, i, None if last else i + 1, ln_ffn_g[i], ln_ffn_b[i],
                         row_fn, tm, alpha)
    return h.reshape(batch, seq, d)
```

```python
import functools

import jax
import jax.numpy as jnp
from jax import lax
from jax.experimental import pallas as pl
from jax.experimental.pallas import tpu as pltpu

HEAD_DIM = 128
N_KV_HEADS = 4
WINDOW = 128
GRID_W = 64
ROPE_BASE = 10000.0
RNN_BLOCKS = 16
CONV_W = 4
CONV_LEFT = 2
RG_C = 8.0
LN_EPS = 1e-5
NEG_INF = -1e30
N_MIXERS = 2
N_MOD = 6

V7X_LANES = 128
V7X_SUBLANES = 8
V7X_MXU_WIDTH = 256
V7X_VMEM_BYTES = 64 * 1024 * 1024
VMEM_LIMIT_BYTES = V7X_VMEM_BYTES - 8 * 1024 * 1024

F32 = jnp.float32
BF16 = jnp.bfloat16
EPILOGUE_ROWS = 128


def _pick(n, cands):
    for c in cands:
        if n % c == 0:
            return c
    raise ValueError(f"no tile in {cands} divides {n}")


def _params(*sem):
    return pltpu.CompilerParams(dimension_semantics=sem, vmem_limit_bytes=VMEM_LIMIT_BYTES)


def _mod_spec(layer, chunk, row_fn):
    return lambda d: pl.BlockSpec((None, None, None, 1, d), lambda *g: (layer, row_fn(*g), chunk, 0, 0))


def _layer_norm(v, g, b):
    mu = jnp.mean(v, axis=-1, keepdims=True)
    d = v - mu
    var = jnp.mean(d * d, axis=-1, keepdims=True)
    return d * lax.rsqrt(var + LN_EPS) * g + b


def _deepnorm_epilogue(acc_ref, res_ref, gate_ref, lg_ref, lb_ref, h_ref, alpha, u_ref=None, sh_ref=None, sc_ref=None,
                       rows=None):
    r0, r1 = rows if rows is not None else (0, acc_ref.shape[0])
    ch = min(r1 - r0, EPILOGUE_ROWS)
    gate, lg, lb = gate_ref[...], lg_ref[...], lb_ref[...]
    if u_ref is not None:
        sh, sc1 = sh_ref[...], 1.0 + sc_ref[...]
    for k in range((r1 - r0) // ch):
        rs = slice(r0 + k * ch, r0 + (k + 1) * ch)
        hn = _layer_norm(alpha * res_ref[rs, :] + gate * acc_ref[rs, :], lg, lb)
        h_ref[rs, :] = hn
        if u_ref is not None:
            u_ref[rs, :] = (hn * sc1 + sh).astype(BF16)


def _mods_kernel(cs_ref, w_ref, b_ref, o_ref):
    s = cs_ref[...]
    s = (s * jax.nn.sigmoid(s)).astype(BF16)
    o_ref[...] = jnp.dot(s, w_ref[...].astype(BF16), preferred_element_type=F32) + b_ref[...]


def _mods_call(cs, mod_w, mod_b):
    depth, d, n = mod_w.shape
    mr = cs.shape[0]
    tn = _pick(n, (1024, 512, 256, 128))
    return pl.pallas_call(
        _mods_kernel,
        grid=(depth, n // tn),
        in_specs=[
            pl.BlockSpec((mr, d), lambda l, j: (0, 0)),
            pl.BlockSpec((None, d, tn), lambda l, j: (l, 0, j)),
            pl.BlockSpec((None, 1, tn), lambda l, j: (l, 0, j)),
        ],
        out_specs=pl.BlockSpec((None, mr, tn), lambda l, j: (l, 0, j)),
        out_shape=jax.ShapeDtypeStruct((depth, mr, n), F32),
        compiler_params=_params("arbitrary", "arbitrary"),
        name="mods",
    )(cs, mod_w, mod_b.reshape(depth, 1, n))


def _modulate_kernel(x_ref, c_ref, sh_ref, sc_ref, h_ref, u_ref, *, n_lat):
    def emit(src_ref):
        v = src_ref[...]
        h_ref[...] = v
        u_ref[...] = (v * (1.0 + sc_ref[...]) + sh_ref[...]).astype(BF16)

    pl.when(pl.program_id(0) < n_lat)(lambda: emit(x_ref))
    pl.when(pl.program_id(0) >= n_lat)(lambda: emit(c_ref))


def _modulate_call(x2, c2, mods5, layer, row_fn, tm):
    d = x2.shape[1]
    n_lat, n_ctx = x2.shape[0] // tm, c2.shape[0] // tm
    r = x2.shape[0] + c2.shape[0]
    tile = pl.BlockSpec((tm, d), lambda i: (i, 0))
    return pl.pallas_call(
        functools.partial(_modulate_kernel, n_lat=n_lat),
        grid=(n_lat + n_ctx,),
        in_specs=[
            pl.BlockSpec((tm, d), lambda i: (jnp.minimum(i, n_lat - 1), 0)),
            pl.BlockSpec((tm, d), lambda i: (jnp.maximum(i - n_lat, 0), 0)),
            _mod_spec(layer, 0, row_fn)(d),
            _mod_spec(layer, 1, row_fn)(d),
        ],
        out_specs=[tile, tile],
        out_shape=[jax.ShapeDtypeStruct((r, d), F32), jax.ShapeDtypeStruct((r, d), BF16)],
        compiler_params=_params("arbitrary"),
        name="modulate0",
    )(x2, c2, mods5, mods5)


def _proj_ln_kernel(a_ref, w_ref, res_ref, gate_ref, sh_ref, sc_ref, lg_ref, lb_ref, h_ref, u_ref, acc, *, alpha):
    tm = a_ref.shape[0]
    halves = 2 if tm % (2 * EPILOGUE_ROWS) == 0 else 1
    for hh in range(halves):
        rows = (hh * tm // halves, (hh + 1) * tm // halves)
        acc[rows[0]:rows[1], :] = jnp.dot(a_ref[rows[0]:rows[1], :], w_ref[...], preferred_element_type=F32)
        _deepnorm_epilogue(acc, res_ref, gate_ref, lg_ref, lb_ref, h_ref, alpha, u_ref, sh_ref, sc_ref, rows=rows)


def _proj_ln_call(a, w, widx, res, rows, mods5, layer, ln_g, ln_b, row_fn, tm, alpha, name):
    r, k = rows, a.shape[1]
    d = w.shape[2]
    vec = pl.BlockSpec((1, d), lambda i: (0, 0))
    return pl.pallas_call(
        functools.partial(_proj_ln_kernel, alpha=alpha),
        grid=(r // tm,),
        in_specs=[
            pl.BlockSpec((tm, k), lambda i: (i, 0)),
            pl.BlockSpec((None, k, d), lambda i: (widx, 0, 0), pipeline_mode=pl.Buffered(1)),
            pl.BlockSpec((tm, d), lambda i: (i, 0)),
            _mod_spec(layer, 2, row_fn)(d),
            _mod_spec(layer, 3, row_fn)(d),
            _mod_spec(layer, 4, row_fn)(d),
            vec,
            vec,
        ],
        out_specs=[pl.BlockSpec((tm, d), lambda i: (i, 0)), pl.BlockSpec((tm, d), lambda i: (i, 0))],
        out_shape=[jax.ShapeDtypeStruct((r, d), F32), jax.ShapeDtypeStruct((r, d), BF16)],
        scratch_shapes=[pltpu.VMEM((tm, d), F32)],
        compiler_params=_params("arbitrary"),
        name=name,
    )(a, w, res, mods5, mods5, mods5, ln_g.reshape(1, d), ln_b.reshape(1, d))


def _ffn_kernel(*refs, alpha, with_u, n_m, n_epi):
    if with_u:
        u_ref, wg_ref, wu_ref, wd_ref, res_ref, gate_ref, sh_ref, sc_ref, lg_ref, lb_ref, h_ref, un_ref, acc = refs
    else:
        u_ref, wg_ref, wu_ref, wd_ref, res_ref, gate_ref, lg_ref, lb_ref, h_ref, acc = refs
        un_ref = sh_ref = sc_ref = None
    i, j = pl.program_id(0), pl.program_id(1)
    slot = i % 2
    ch = res_ref.shape[0]

    def matmuls(first):
        u = u_ref[...]
        g = jnp.dot(u, wg_ref[...], preferred_element_type=F32)
        up = jnp.dot(u, wu_ref[...], preferred_element_type=F32)
        hid = (g * jax.nn.sigmoid(g) * up).astype(BF16)
        o = jnp.dot(hid, wd_ref[...], preferred_element_type=F32)
        if first:
            acc[slot] = o
        else:
            acc[slot] += o

    def epilogue():
        rs = pl.ds(pl.multiple_of(j * ch, ch), ch)
        hn = _layer_norm(alpha * res_ref[...] + gate_ref[...] * acc[1 - slot, rs, :], lg_ref[...], lb_ref[...])
        h_ref[...] = hn
        if with_u:
            un_ref[...] = (hn * (1.0 + sc_ref[...]) + sh_ref[...]).astype(BF16)

    land = jnp.logical_and
    live = i < n_m
    epi = land(i > 0, j < n_epi)

    @pl.when(land(land(live, epi), j == 0))
    def _():
        matmuls(True)
        epilogue()

    @pl.when(land(land(live, epi), j > 0))
    def _():
        matmuls(False)
        epilogue()

    @pl.when(land(land(live, jnp.logical_not(epi)), j == 0))
    def _():
        matmuls(True)

    @pl.when(land(land(live, jnp.logical_not(epi)), j > 0))
    def _():
        matmuls(False)

    @pl.when(land(jnp.logical_not(live), epi))
    def _():
        epilogue()


def _ffn_call(u, wg, wu, wd, res, rows, mods5, layer, next_layer, ln_g, ln_b, row_fn, tm, alpha):
    r, d = rows, u.shape[1]
    hid = wg.shape[2]
    th = _pick(hid, (512, 256, 128))
    n_m, n_h = r // tm, hid // th
    ch = min(tm, EPILOGUE_ROWS)
    n_epi = tm // ch
    assert n_epi <= n_h
    with_u = next_layer is not None
    prev = lambda i: jnp.maximum(i - 1, 0)
    row2 = lambda i, j: row_fn(prev(i))
    jw = lambda i, j: jnp.where(i < n_m, j, n_h - 1)
    vec = pl.BlockSpec((1, d), lambda i, j: (0, 0))
    chunk = pl.BlockSpec((ch, d), lambda i, j: (prev(i) * n_epi + jnp.where(i == 0, 0, jnp.minimum(j, n_epi - 1)), 0))
    in_specs = [
        pl.BlockSpec((tm, d), lambda i, j: (jnp.minimum(i, n_m - 1), 0)),
        pl.BlockSpec((None, d, th), lambda i, j: (layer, 0, jw(i, j))),
        pl.BlockSpec((None, d, th), lambda i, j: (layer, 0, jw(i, j))),
        pl.BlockSpec((None, th, d), lambda i, j: (layer, jw(i, j), 0)),
        chunk,
        _mod_spec(layer, 5, row2)(d),
    ]
    args = [u, wg, wu, wd, res, mods5]
    if with_u:
        in_specs += [_mod_spec(next_layer, 0, row2)(d), _mod_spec(next_layer, 1, row2)(d)]
        args += [mods5, mods5]
    in_specs += [vec, vec]
    args += [ln_g.reshape(1, d), ln_b.reshape(1, d)]
    out_specs = [chunk]
    out_shape = [jax.ShapeDtypeStruct((r, d), F32)]
    if with_u:
        out_specs.append(chunk)
        out_shape.append(jax.ShapeDtypeStruct((r, d), BF16))
    out = pl.pallas_call(
        functools.partial(_ffn_kernel, alpha=alpha, with_u=with_u, n_m=n_m, n_epi=n_epi),
        grid=(n_m + 1, n_h),
        in_specs=in_specs,
        out_specs=out_specs,
        out_shape=out_shape,
        scratch_shapes=[pltpu.VMEM((2, tm, d), F32)],
        compiler_params=_params("arbitrary", "arbitrary"),
        name="ffn",
    )(*args)
    return (out[0], out[1]) if with_u else (out[0], None)


def _gelu_tanh(x):
    return x * (0.5 * (1.0 + jnp.tanh(0.7978845608028654 * (x + 0.044715 * (x * x * x)))))


def _col_groups(n, width=2 * V7X_MXU_WIDTH):
    return [(c, min(c + width, n)) for c in range(0, n, width)]


def _win_kernel(u_ref, w_ref, y_ref, z_ref):
    u = u_ref[...]
    dr = y_ref.shape[1]
    for c0, c1 in _col_groups(dr):
        y_ref[:, c0:c1] = _gelu_tanh(jnp.dot(u, w_ref[:, c0:c1], preferred_element_type=F32)).astype(BF16)
    for c0, c1 in _col_groups(dr):
        z_ref[:, c0:c1] = jnp.dot(u, w_ref[:, dr + c0:dr + c1], preferred_element_type=F32)


def _win_call(u, w_in, widx, tm):
    r, d = u.shape
    n = w_in.shape[2]
    dr = n // 2
    return pl.pallas_call(
        _win_kernel,
        grid=(r // tm,),
        in_specs=[
            pl.BlockSpec((tm, d), lambda i: (i, 0)),
            pl.BlockSpec((None, d, n), lambda i: (widx, 0, 0), pipeline_mode=pl.Buffered(1)),
        ],
        out_specs=[pl.BlockSpec((tm, dr), lambda i: (i, 0)), pl.BlockSpec((tm, dr), lambda i: (i, 0))],
        out_shape=[jax.ShapeDtypeStruct((r, dr), BF16), jax.ShapeDtypeStruct((r, dr), F32)],
        compiler_params=_params("arbitrary"),
        name="rec_in",
    )(u, w_in)


def _gate_window_plan(dr, bw):
    tn = V7X_MXU_WIDTH
    spans = []
    for n in range(dr // tn):
        c0 = n * tn
        lo = (c0 // bw) * bw
        hi = ((c0 + tn - 1) // bw + 1) * bw
        spans.append(((lo // V7X_LANES) * V7X_LANES, -(-hi // V7X_LANES) * V7X_LANES))
    kw = max(h - l for l, h in spans)
    return kw, tuple(min(l, dr - kw) for l, _ in spans)


def _gate_windows(w, kw, k0s):
    lead, bw = w.shape[:-3], w.shape[-1]
    tn = V7X_MXU_WIDTH
    w = w.astype(BF16)
    tiles = []
    for n, k0 in enumerate(k0s):
        c0 = n * tn
        h0, h1 = c0 // bw, (c0 + tn - 1) // bw
        nbk = h1 - h0 + 1
        diag = jnp.einsum("...hij,hg->...higj", w[..., h0:h1 + 1, :, :], jnp.eye(nbk, dtype=BF16))
        diag = diag.reshape(*lead, nbk * bw, nbk * bw)
        r_off, c_off = h0 * bw - k0, h0 * bw - c0
        cfg = [(0, 0, 0)] * len(lead) + [(r_off, kw - r_off - nbk * bw, 0), (c_off, tn - c_off - nbk * bw, 0)]
        tiles.append(lax.pad(diag, jnp.zeros((), BF16), cfg))
    return jnp.stack(tiles, axis=-3)


def _gates(zb, z_ref, wa_ref, wx_ref, ba_ref, bx_ref, lam_ref, a_s, b_s, kw, k0s):
    tn = V7X_MXU_WIDTH
    for n, k0 in enumerate(k0s):
        cs = slice(n * tn, (n + 1) * tn)
        zw = zb[:, k0:k0 + kw]
        ta = jnp.tanh(0.5 * (jnp.dot(zw, wa_ref[n], preferred_element_type=F32) + ba_ref[:, cs]))
        tx = jnp.tanh(0.5 * (jnp.dot(zw, wx_ref[n], preferred_element_type=F32) + bx_ref[:, cs]))
        nl = -lam_ref[:, cs]
        half = (0.5 * RG_C) * (jnp.maximum(nl, 0.0) + jnp.log1p(jnp.exp(-jnp.abs(nl))))
        q = ta * half + half
        a = jnp.exp(-q)
        a_s[:, cs] = a
        b_s[:, cs] = jnp.sqrt(jnp.tanh(q) * (1.0 + a * a)) * (0.5 * tx + 0.5) * z_ref[:, cs]


def _scan(a_s, b_s, dst, carry, reverse):
    tt, dr = a_s.shape
    tn = V7X_MXU_WIDTH
    sub = V7X_SUBLANES
    ng = tt // sub
    row = lax.broadcasted_iota(jnp.int32, (sub, tn), 0)

    def body(i, c):
        g = (ng - 1 - i) if reverse else i
        rs = pl.ds(pl.multiple_of(g * sub, sub), sub)
        for n in range(dr // tn):
            cs = slice(n * tn, (n + 1) * tn)
            a8 = a_s[rs, cs]
            b8 = b_s[rs, cs]
            for s in (1, 2, 4):
                if reverse:
                    ok = row < sub - s
                    a_sh = pltpu.roll(a8, sub - s, 0)
                    b_sh = pltpu.roll(b8, sub - s, 0)
                else:
                    ok = row >= s
                    a_sh = pltpu.roll(a8, s, 0)
                    b_sh = pltpu.roll(b8, s, 0)
                b8 = a8 * jnp.where(ok, b_sh, 0.0) + b8
                a8 = a8 * jnp.where(ok, a_sh, 1.0)
            h8 = a8 * carry[:, cs] + b8
            dst[rs, cs] = h8
            edge = h8[0:1, :] if reverse else h8[sub - 1:sub, :]
            carry[:, cs] = jnp.broadcast_to(edge, (sub, tn))
        return c

    lax.fori_loop(0, ng, body, 0)


def _scan_fwd_kernel(zc_ref, zp_ref, zn_ref, cw_ref, cb_ref, wa_ref, wx_ref, ba_ref, bx_ref, lam_ref,
                     hf_ref, z_ref, xpad, zb, a_s, b_s, carry, *, kw, k0s, ns):
    tt = zc_ref.shape[0]
    tn = V7X_MXU_WIDTH
    sub = V7X_SUBLANES
    t = pl.program_id(1)

    @pl.when(t == 0)
    def _():
        carry[...] = jnp.zeros_like(carry)

    first = t <= 1
    last = jnp.logical_or(t == 0, t == ns)
    xpad[0:sub, :] = jnp.where(first, 0.0, zp_ref[...])
    xpad[sub:sub + tt, :] = zc_ref[...]
    xpad[sub + tt:2 * sub + tt, :] = jnp.where(last, 0.0, zn_ref[...])
    for n in range(len(k0s)):
        cs = slice(n * tn, (n + 1) * tn)
        xa = xpad[:, cs]
        acc = None
        for j in range(CONV_W):
            shift = (CONV_LEFT - j) % xa.shape[0]
            xs = pltpu.roll(xa, shift, 0) if shift else xa
            term = xs[sub:sub + tt, :] * cw_ref[j:j + 1, cs]
            acc = term if acc is None else acc + term
        z = acc + cb_ref[:, cs]
        z_ref[:, cs] = z
        zb[:, cs] = z.astype(BF16)

    _gates(zb, z_ref, wa_ref, wx_ref, ba_ref, bx_ref, lam_ref, a_s, b_s, kw, k0s)
    _scan(a_s, b_s, hf_ref, carry, reverse=False)


def _scan_bwd_kernel(z_ref, wa_ref, wx_ref, ba_ref, bx_ref, lam_ref, hf_ref, y_ref, o_ref, zb, a_s, b_s, h_s, carry,
                     *, kw, k0s):
    tn = V7X_MXU_WIDTH

    @pl.when(pl.program_id(1) == 0)
    def _():
        carry[...] = jnp.zeros_like(carry)

    zb[...] = z_ref[...].astype(BF16)
    _gates(zb, z_ref, wa_ref, wx_ref, ba_ref, bx_ref, lam_ref, a_s, b_s, kw, k0s)
    _scan(a_s, b_s, h_s, carry, reverse=True)
    for n in range(len(k0s)):
        cs = slice(n * tn, (n + 1) * tn)
        o_ref[:, cs] = (y_ref[:, cs].astype(F32) * (hf_ref[:, cs] + h_s[:, cs])).astype(BF16)


def _scan_calls(zpre, y, conv_w, conv_b, wa, wx, ba, bx, lam, kw, k0s, batch, seq, ctx_len):
    r, dr = zpre.shape
    tt = ctx_len
    ns = seq // tt
    nx = batch * seq // tt
    g8 = tt // V7X_SUBLANES
    nblk8 = r // V7X_SUBLANES
    nt = len(k0s)

    def cur(reverse):
        def f(b, t):
            st = (ns - t) if reverse else (t - 1)
            return jnp.where(t == 0, nx + b, b * ns + st)
        return f

    def tile(reverse):
        return pl.BlockSpec((tt, dr), lambda b, t: (cur(reverse)(b, t), 0))

    fw = cur(False)
    halo_p = pl.BlockSpec((V7X_SUBLANES, dr), lambda b, t: (jnp.maximum(fw(b, t) * g8 - 1, 0), 0))
    halo_n = pl.BlockSpec((V7X_SUBLANES, dr), lambda b, t: (jnp.minimum(fw(b, t) * g8 + g8, nblk8 - 1), 0))
    vec = lambda d: pl.BlockSpec((None, 1, dr), lambda b, t: (d, 0, 0))
    wspec = lambda d: pl.BlockSpec((None, nt, kw, V7X_MXU_WIDTH), lambda b, t: (d, 0, 0, 0),
                                   pipeline_mode=pl.Buffered(1))
    ba, bx, lam = (v.reshape(2, 1, dr) for v in (ba, bx, lam))
    gate_specs = lambda d: [wspec(d), wspec(d), vec(d), vec(d), vec(d)]
    gate_args = [wa, wx, ba, bx, lam]
    work = [
        pltpu.VMEM((tt, dr), BF16),
        pltpu.VMEM((tt, dr), F32),
        pltpu.VMEM((tt, dr), F32),
    ]
    state = pltpu.VMEM((V7X_SUBLANES, dr), F32)
    hf, z = pl.pallas_call(
        functools.partial(_scan_fwd_kernel, kw=kw, k0s=k0s, ns=ns),
        grid=(batch, ns + 1),
        in_specs=[tile(False), halo_p, halo_n, pl.BlockSpec((CONV_W, dr), lambda b, t: (0, 0)),
                  pl.BlockSpec((1, dr), lambda b, t: (0, 0))] + gate_specs(0),
        out_specs=[tile(False), tile(False)],
        out_shape=[jax.ShapeDtypeStruct((r, dr), F32), jax.ShapeDtypeStruct((r, dr), F32)],
        scratch_shapes=[pltpu.VMEM((tt + 2 * V7X_SUBLANES, dr), F32)] + work + [state],
        compiler_params=_params("arbitrary", "arbitrary"),
        name="rec_scan_fwd",
    )(zpre, zpre, zpre, conv_w, conv_b.reshape(1, dr), *gate_args)
    return pl.pallas_call(
        functools.partial(_scan_bwd_kernel, kw=kw, k0s=k0s),
        grid=(batch, ns + 1),
        in_specs=[tile(True)] + gate_specs(1) + [tile(True), tile(True)],
        out_specs=tile(True),
        out_shape=jax.ShapeDtypeStruct((r, dr), BF16),
        scratch_shapes=work + [pltpu.VMEM((tt, dr), F32), state],
        compiler_params=_params("arbitrary", "arbitrary"),
        name="rec_scan_bwd",
    )(z, *gate_args, hf, y)


def _rope_tables(seq, pad_rows):
    rows = seq // GRID_W
    row = jnp.repeat(jnp.arange(rows, dtype=F32), GRID_W)
    col = jnp.tile(jnp.arange(GRID_W, dtype=F32), rows)
    axis_dim = HEAD_DIM // 2
    inv_freq = ROPE_BASE ** (-jnp.arange(0, axis_dim, 2, dtype=F32) / axis_dim)
    ang = jnp.concatenate([row[:, None] * inv_freq, col[:, None] * inv_freq], axis=-1)
    cos, sin = jnp.cos(ang), jnp.sin(ang)
    cos2 = jnp.concatenate([cos, cos], axis=-1)
    sin2 = jnp.concatenate([-sin, sin], axis=-1)
    cos2 = jnp.concatenate([cos2, jnp.ones((pad_rows, HEAD_DIM), F32)], axis=0)
    sin2 = jnp.concatenate([sin2, jnp.zeros((pad_rows, HEAD_DIM), F32)], axis=0)
    return cos2, sin2


def _qkv_kernel(u_ref, w_ref, cos_ref, sin_ref, o_ref, *, nq, nkv, scale):
    u = u_ref[...]
    c = cos_ref[...]
    s = sin_ref[...]
    for c0, c1 in _col_groups(o_ref.shape[1]):
        p = jnp.dot(u, w_ref[:, c0:c1], preferred_element_type=F32)
        for h0 in range(c0, c1, HEAD_DIM):
            t = p[:, h0 - c0:h0 - c0 + HEAD_DIM]
            if h0 < nq + nkv:
                t = t * c + pltpu.roll(t, HEAD_DIM // 2, 1) * s
            if h0 < nq:
                t = t * scale
            o_ref[:, h0:h0 + HEAD_DIM] = t.astype(BF16)


def _qkv_call(u, w, widx, cos2, sin2, seq, rows_x, tm):
    r, d = u.shape
    n = w.shape[2]
    nkv = N_KV_HEADS * HEAD_DIM
    n_seq_tiles = seq // tm
    tab = pl.BlockSpec((tm, HEAD_DIM), lambda i: (jnp.where(i < rows_x // tm, i % n_seq_tiles, n_seq_tiles), 0))
    return pl.pallas_call(
        functools.partial(_qkv_kernel, nq=n - 2 * nkv, nkv=nkv, scale=HEAD_DIM ** -0.5),
        grid=(r // tm,),
        in_specs=[pl.BlockSpec((tm, d), lambda i: (i, 0)),
                  pl.BlockSpec((None, d, n), lambda i: (widx, 0, 0), pipeline_mode=pl.Buffered(1)), tab, tab],
        out_specs=pl.BlockSpec((tm, n), lambda i: (i, 0)),
        out_shape=jax.ShapeDtypeStruct((r, n), BF16),
        compiler_params=_params("arbitrary"),
        name="att_qkv",
    )(u, w, cos2, sin2)


def _attn_kernel(sink_ref, q_ref, kp_ref, kc_ref, kn_ref, vp_ref, vc_ref, vn_ref, kx_ref, vx_ref, o_ref,
                 k_all, v_all, bias, *, nb, g):
    j = pl.program_id(1)
    blk = WINDOW
    band = 3 * blk
    rows = g * blk
    is_lat = j < nb
    for dst, srcs in ((k_all, (kp_ref, kc_ref, kn_ref)), (v_all, (vp_ref, vc_ref, vn_ref))):
        for o, src in enumerate(srcs):
            dst[o * blk:(o + 1) * blk, :] = src[...]
    k_all[band:, :] = kx_ref[...]
    v_all[band:, :] = vx_ref[...]
    pen_p = jnp.where(jnp.logical_and(is_lat, j > 0), 0.0, NEG_INF)
    pen_c = jnp.where(is_lat, 0.0, NEG_INF)
    pen_n = jnp.where(jnp.logical_and(is_lat, j < nb - 1), 0.0, NEG_INF)
    qi = lax.broadcasted_iota(jnp.int32, (rows, blk), 0) & (blk - 1)
    ki = lax.broadcasted_iota(jnp.int32, (rows, blk), 1)
    bias[:, 0:blk] = jnp.where(ki >= qi, pen_p, NEG_INF)
    bias[:, blk:2 * blk] = jnp.full((rows, blk), pen_c, F32)
    bias[:, 2 * blk:band] = jnp.where(ki <= qi, pen_n, NEG_INF)
    bias[:, band:] = jnp.zeros((rows, bias.shape[1] - band), F32)
    nt_dims = (((1,), (1,)), ((), ()))
    for kh in range(N_KV_HEADS):
        ks = slice(kh * HEAD_DIM, (kh + 1) * HEAD_DIM)
        heads = [kh * g + gi for gi in range(g)]
        qs = jnp.concatenate([q_ref[:, h * HEAD_DIM:(h + 1) * HEAD_DIM] for h in heads], axis=0)
        sk = jnp.concatenate([jnp.full((blk, 1), sink_ref[h], F32) for h in heads], axis=0)
        s = lax.dot_general(qs, k_all[:, ks], nt_dims, preferred_element_type=F32) + bias[...]
        m = jnp.maximum(jnp.max(s, -1, keepdims=True), sk)
        p = jnp.exp(s - m)
        denom = jnp.exp(sk - m) + jnp.sum(p, -1, keepdims=True)
        o = jnp.dot(p.astype(BF16), v_all[:, ks], preferred_element_type=F32) / denom
        for gi, h in enumerate(heads):
            o_ref[:, h * HEAD_DIM:(h + 1) * HEAD_DIM] = o[gi * blk:(gi + 1) * blk].astype(BF16)


def _attn_call(qkv, sink, batch, seq, ctx_len, ctx_queries):
    r, n = qkv.shape
    nkv = N_KV_HEADS * HEAD_DIM
    d = n - 2 * nkv
    g = d // HEAD_DIM // N_KV_HEADS
    blk = WINDOW
    nb = seq // blk
    ncb = ctx_len // blk
    kcol = d // nkv
    x0 = batch * seq

    def qrow(b, j):
        return jnp.where(j < nb, b * nb + j, x0 // blk + b * ncb + (j - nb))

    def krow(b, j, off):
        return b * nb + jnp.clip(jnp.minimum(j, nb - 1) + off, 0, nb - 1)

    qspec = pl.BlockSpec((blk, d), lambda b, j: (qrow(b, j), 0))
    band = lambda off, col: pl.BlockSpec((blk, nkv), lambda b, j: (krow(b, j, off), col))
    ctxs = lambda col: pl.BlockSpec((ctx_len, nkv), lambda b, j: (x0 // ctx_len + b, col))
    return pl.pallas_call(
        functools.partial(_attn_kernel, nb=nb, g=g),
        grid=(batch, nb + ncb if ctx_queries else nb),
        in_specs=[
            pl.BlockSpec(memory_space=pltpu.SMEM),
            qspec,
            band(-1, kcol), band(0, kcol), band(1, kcol),
            band(-1, kcol + 1), band(0, kcol + 1), band(1, kcol + 1),
            ctxs(kcol), ctxs(kcol + 1),
        ],
        out_specs=qspec,
        out_shape=jax.ShapeDtypeStruct((r if ctx_queries else x0, d), BF16),
        scratch_shapes=[
            pltpu.VMEM((3 * blk + ctx_len, nkv), BF16),
            pltpu.VMEM((3 * blk + ctx_len, nkv), BF16),
            pltpu.VMEM((g * blk, 3 * blk + ctx_len), F32),
        ],
        compiler_params=_params("arbitrary", "arbitrary"),
        name="att_core",
    )(sink, qkv, qkv, qkv, qkv, qkv, qkv, qkv, qkv, qkv)


def kernel(x, c, ctx, c_ctx, mod_w, mod_b, ln_mix_g, ln_mix_b, ln_ffn_g, ln_ffn_b, ffn_w_gate, ffn_w_up, ffn_w_down,
           rec_w_in, rec_conv_w, rec_conv_b, rec_gate_a_w, rec_gate_a_b, rec_gate_x_w, rec_gate_x_b, rec_lambda,
           rec_w_out, att_w_qkv, att_sink, att_w_o):
    batch, seq, d = x.shape
    ctx_len = ctx.shape[1]
    depth = mod_w.shape[0]
    rows_x, rows_c = batch * seq, batch * ctx_len
    assert seq % ctx_len == 0 and ctx_len % WINDOW == 0 and seq % GRID_W == 0
    assert d % (N_KV_HEADS * HEAD_DIM) == 0
    alpha = (2.0 * depth) ** 0.25

    tm = _pick(seq, (512, 256, 128))
    while rows_c % tm:
        tm //= 2
    tm_big = 2 * tm if (seq % (2 * tm) == 0 and rows_c % (2 * tm) == 0) else tm

    def row_fn_for(t):
        return lambda i: jnp.minimum((i * t) // seq, batch)

    row_fn, row_fn_big = row_fn_for(tm), row_fn_for(tm_big)

    mr = -(-(batch + 1) // V7X_SUBLANES) * V7X_SUBLANES
    cs = jnp.zeros((mr, d), F32).at[:batch].set(c).at[batch].set(c_ctx)
    mods5 = _mods_call(cs, mod_w, mod_b).reshape(depth, mr, N_MOD, 1, d)

    cos2, sin2 = _rope_tables(seq, tm_big)
    dr = rec_w_out.shape[1]
    kw, k0s = _gate_window_plan(dr, dr // RNN_BLOCKS)
    wa_all = _gate_windows(rec_gate_a_w, kw, k0s)
    wx_all = _gate_windows(rec_gate_x_w, kw, k0s)

    w_gate, w_up, w_down = ffn_w_gate.astype(BF16), ffn_w_up.astype(BF16), ffn_w_down.astype(BF16)
    w_in, w_out = rec_w_in.astype(BF16), rec_w_out.astype(BF16)
    w_qkv, w_o = att_w_qkv.astype(BF16), att_w_o.astype(BF16)

    h, u = _modulate_call(x.reshape(rows_x, d), ctx.reshape(rows_c, d), mods5, 0, row_fn, tm)
    for i in range(depth):
        j = i // N_MIXERS
        last = i == depth - 1
        rows = rows_x if last else rows_x + rows_c
        if i % N_MIXERS == 0:
            y, zpre = _win_call(u, w_in, j, tm)
            yh = _scan_calls(zpre, y, rec_conv_w[j], rec_conv_b[j], wa_all[j], wx_all[j], rec_gate_a_b[j],
                             rec_gate_x_b[j], rec_lambda[j], kw, k0s, batch, seq, ctx_len)
            h, u = _proj_ln_call(yh, w_out, j, h, rows, mods5, i, ln_mix_g[i], ln_mix_b[i], row_fn, tm, alpha, "rec_out")
        else:
            qkv = _qkv_call(u, w_qkv, j, cos2, sin2, seq, rows_x, tm_big)
            ao = _attn_call(qkv, att_sink[j], batch, seq, ctx_len, ctx_queries=not last)
            h, u = _proj_ln_call(ao, w_o, j, h, rows, mods5, i, ln_mix_g[i], ln_mix_b[i], row_fn, tm, alpha, "att_out")
        h, u = _ffn_call(u, w_gate, w_up, w_down, h, rows, mods5, i, None if last else i + 1, ln_ffn_g[i], ln_ffn_b[i],
                         row_fn_big, tm_big, alpha)
    return h.reshape(batch, seq, d)
```

```python
import functools

import jax
import jax.numpy as jnp
from jax import lax
from jax.experimental import pallas as pl
from jax.experimental.pallas import tpu as pltpu

HEAD_DIM = 128
N_KV_HEADS = 4
WINDOW = 128
GRID_W = 64
ROPE_BASE = 10000.0
RNN_BLOCKS = 16
CONV_W = 4
CONV_LEFT = 2
RG_C = 8.0
LN_EPS = 1e-5
NEG_INF = -1e30
N_MIXERS = 2
N_MOD = 6

V7X_LANES = 128
V7X_SUBLANES = 8
V7X_MXU_WIDTH = 256
V7X_VMEM_BYTES = 64 * 1024 * 1024
VMEM_LIMIT_BYTES = V7X_VMEM_BYTES - 8 * 1024 * 1024

F32 = jnp.float32
BF16 = jnp.bfloat16
EPILOGUE_ROWS = 128
LN_ROWS = 128


def _pick(n, cands):
    for c in cands:
        if n % c == 0:
            return c
    raise ValueError(f"no tile in {cands} divides {n}")


def _params(*sem):
    return pltpu.CompilerParams(dimension_semantics=sem, vmem_limit_bytes=VMEM_LIMIT_BYTES)


def _mod_spec(layer, chunk, row_fn):
    return lambda d: pl.BlockSpec((None, None, None, 1, d), lambda *g: (layer, row_fn(*g), chunk, 0, 0))


def _layer_norm(v, g, b):
    mu = jnp.mean(v, axis=-1, keepdims=True)
    d = v - mu
    var = jnp.mean(d * d, axis=-1, keepdims=True)
    return d * lax.rsqrt(var + LN_EPS) * g + b


def _deepnorm_epilogue(acc_ref, res_ref, gate_ref, lg_ref, lb_ref, h_ref, alpha, u_ref=None, sh_ref=None, sc_ref=None,
                       rows=None):
    r0, r1 = rows if rows is not None else (0, acc_ref.shape[0])
    ch = min(r1 - r0, LN_ROWS)
    gate, lg, lb = gate_ref[...], lg_ref[...], lb_ref[...]
    if u_ref is not None:
        sh, sc1 = sh_ref[...], 1.0 + sc_ref[...]
    for k in range((r1 - r0) // ch):
        rs = slice(r0 + k * ch, r0 + (k + 1) * ch)
        hn = _layer_norm(alpha * res_ref[rs, :] + gate * acc_ref[rs, :], lg, lb)
        h_ref[rs, :] = hn
        if u_ref is not None:
            u_ref[rs, :] = (hn * sc1 + sh).astype(BF16)


def _mods_kernel(cs_ref, w_ref, b_ref, o_ref):
    s = cs_ref[...]
    s = (s * jax.nn.sigmoid(s)).astype(BF16)
    o_ref[...] = jnp.dot(s, w_ref[...].astype(BF16), preferred_element_type=F32) + b_ref[...]


def _mods_call(cs, mod_w, mod_b):
    depth, d, n = mod_w.shape
    mr = cs.shape[0]
    tn = _pick(n, (1024, 512, 256, 128))
    return pl.pallas_call(
        _mods_kernel,
        grid=(depth, n // tn),
        in_specs=[
            pl.BlockSpec((mr, d), lambda l, j: (0, 0)),
            pl.BlockSpec((None, d, tn), lambda l, j: (l, 0, j)),
            pl.BlockSpec((None, 1, tn), lambda l, j: (l, 0, j)),
        ],
        out_specs=pl.BlockSpec((None, mr, tn), lambda l, j: (l, 0, j)),
        out_shape=jax.ShapeDtypeStruct((depth, mr, n), F32),
        compiler_params=_params("arbitrary", "arbitrary"),
        name="mods",
    )(cs, mod_w, mod_b.reshape(depth, 1, n))


def _modulate_kernel(x_ref, c_ref, sh_ref, sc_ref, h_ref, u_ref, *, n_lat):
    def emit(src_ref):
        v = src_ref[...]
        h_ref[...] = v
        u_ref[...] = (v * (1.0 + sc_ref[...]) + sh_ref[...]).astype(BF16)

    pl.when(pl.program_id(0) < n_lat)(lambda: emit(x_ref))
    pl.when(pl.program_id(0) >= n_lat)(lambda: emit(c_ref))


def _modulate_call(x2, c2, mods5, layer, row_fn, tm):
    d = x2.shape[1]
    n_lat, n_ctx = x2.shape[0] // tm, c2.shape[0] // tm
    r = x2.shape[0] + c2.shape[0]
    tile = pl.BlockSpec((tm, d), lambda i: (i, 0))
    return pl.pallas_call(
        functools.partial(_modulate_kernel, n_lat=n_lat),
        grid=(n_lat + n_ctx,),
        in_specs=[
            pl.BlockSpec((tm, d), lambda i: (jnp.minimum(i, n_lat - 1), 0)),
            pl.BlockSpec((tm, d), lambda i: (jnp.maximum(i - n_lat, 0), 0)),
            _mod_spec(layer, 0, row_fn)(d),
            _mod_spec(layer, 1, row_fn)(d),
        ],
        out_specs=[tile, tile],
        out_shape=[jax.ShapeDtypeStruct((r, d), F32), jax.ShapeDtypeStruct((r, d), BF16)],
        compiler_params=_params("arbitrary"),
        name="modulate0",
    )(x2, c2, mods5, mods5)


def _proj_ln_kernel(a_ref, w_ref, res_ref, gate_ref, sh_ref, sc_ref, lg_ref, lb_ref, h_ref, u_ref, acc, *, alpha):
    tm = a_ref.shape[0]
    halves = 2 if tm % (2 * EPILOGUE_ROWS) == 0 else 1
    for hh in range(halves):
        rows = (hh * tm // halves, (hh + 1) * tm // halves)
        acc[rows[0]:rows[1], :] = jnp.dot(a_ref[rows[0]:rows[1], :], w_ref[...], preferred_element_type=F32)
        _deepnorm_epilogue(acc, res_ref, gate_ref, lg_ref, lb_ref, h_ref, alpha, u_ref, sh_ref, sc_ref, rows=rows)


def _proj_ln_call(a, w, widx, res, rows, mods5, layer, ln_g, ln_b, row_fn, tm, alpha, name):
    r, k = rows, a.shape[1]
    d = w.shape[2]
    vec = pl.BlockSpec((1, d), lambda i: (0, 0))
    return pl.pallas_call(
        functools.partial(_proj_ln_kernel, alpha=alpha),
        grid=(r // tm,),
        in_specs=[
            pl.BlockSpec((tm, k), lambda i: (i, 0)),
            pl.BlockSpec((None, k, d), lambda i: (widx, 0, 0), pipeline_mode=pl.Buffered(1)),
            pl.BlockSpec((tm, d), lambda i: (i, 0)),
            _mod_spec(layer, 2, row_fn)(d),
            _mod_spec(layer, 3, row_fn)(d),
            _mod_spec(layer, 4, row_fn)(d),
            vec,
            vec,
        ],
        out_specs=[pl.BlockSpec((tm, d), lambda i: (i, 0)), pl.BlockSpec((tm, d), lambda i: (i, 0))],
        out_shape=[jax.ShapeDtypeStruct((r, d), F32), jax.ShapeDtypeStruct((r, d), BF16)],
        scratch_shapes=[pltpu.VMEM((tm, d), F32)],
        compiler_params=_params("arbitrary"),
        name=name,
    )(a, w, res, mods5, mods5, mods5, ln_g.reshape(1, d), ln_b.reshape(1, d))


def _ffn_kernel(*refs, alpha, with_u, n_m, n_epi):
    if with_u:
        u_ref, wg_ref, wu_ref, wd_ref, res_ref, gate_ref, sh_ref, sc_ref, lg_ref, lb_ref, h_ref, un_ref, acc = refs
    else:
        u_ref, wg_ref, wu_ref, wd_ref, res_ref, gate_ref, lg_ref, lb_ref, h_ref, acc = refs
        un_ref = sh_ref = sc_ref = None
    i, j = pl.program_id(0), pl.program_id(1)
    slot = i % 2
    ch = res_ref.shape[0]

    def matmuls(first):
        u = u_ref[...]
        g = jnp.dot(u, wg_ref[...], preferred_element_type=F32)
        up = jnp.dot(u, wu_ref[...], preferred_element_type=F32)
        hid = (g * jax.nn.sigmoid(g) * up).astype(BF16)
        o = jnp.dot(hid, wd_ref[...], preferred_element_type=F32)
        if first:
            acc[slot] = o
        else:
            acc[slot] += o

    def epilogue():
        sub = min(ch, LN_ROWS)
        for k in range(ch // sub):
            rs = pl.ds(pl.multiple_of(j * ch + k * sub, sub), sub)
            ks = slice(k * sub, (k + 1) * sub)
            hn = _layer_norm(alpha * res_ref[ks, :] + gate_ref[...] * acc[1 - slot, rs, :], lg_ref[...], lb_ref[...])
            h_ref[ks, :] = hn
            if with_u:
                un_ref[ks, :] = (hn * (1.0 + sc_ref[...]) + sh_ref[...]).astype(BF16)

    land = jnp.logical_and
    live = i < n_m
    epi = land(i > 0, j < n_epi)

    @pl.when(land(land(live, epi), j == 0))
    def _():
        matmuls(True)
        epilogue()

    @pl.when(land(land(live, epi), j > 0))
    def _():
        matmuls(False)
        epilogue()

    @pl.when(land(land(live, jnp.logical_not(epi)), j == 0))
    def _():
        matmuls(True)

    @pl.when(land(land(live, jnp.logical_not(epi)), j > 0))
    def _():
        matmuls(False)

    @pl.when(land(jnp.logical_not(live), epi))
    def _():
        epilogue()


def _ffn_call(u, wg, wu, wd, res, rows, mods5, layer, next_layer, ln_g, ln_b, row_fn, tm, alpha):
    r, d = rows, u.shape[1]
    hid = wg.shape[2]
    th = _pick(hid, (512, 256, 128))
    n_m, n_h = r // tm, hid // th
    ch = min(tm, EPILOGUE_ROWS)
    n_epi = tm // ch
    assert n_epi <= n_h
    with_u = next_layer is not None
    prev = lambda i: jnp.maximum(i - 1, 0)
    row2 = lambda i, j: row_fn(prev(i))
    jw = lambda i, j: jnp.where(i < n_m, j, n_h - 1)
    vec = pl.BlockSpec((1, d), lambda i, j: (0, 0))
    chunk = pl.BlockSpec((ch, d), lambda i, j: (prev(i) * n_epi + jnp.where(i == 0, 0, jnp.minimum(j, n_epi - 1)), 0))
    in_specs = [
        pl.BlockSpec((tm, d), lambda i, j: (jnp.minimum(i, n_m - 1), 0)),
        pl.BlockSpec((None, d, th), lambda i, j: (layer, 0, jw(i, j))),
        pl.BlockSpec((None, d, th), lambda i, j: (layer, 0, jw(i, j))),
        pl.BlockSpec((None, th, d), lambda i, j: (layer, jw(i, j), 0)),
        chunk,
        _mod_spec(layer, 5, row2)(d),
    ]
    args = [u, wg, wu, wd, res, mods5]
    if with_u:
        in_specs += [_mod_spec(next_layer, 0, row2)(d), _mod_spec(next_layer, 1, row2)(d)]
        args += [mods5, mods5]
    in_specs += [vec, vec]
    args += [ln_g.reshape(1, d), ln_b.reshape(1, d)]
    out_specs = [chunk]
    out_shape = [jax.ShapeDtypeStruct((r, d), F32)]
    if with_u:
        out_specs.append(chunk)
        out_shape.append(jax.ShapeDtypeStruct((r, d), BF16))
    out = pl.pallas_call(
        functools.partial(_ffn_kernel, alpha=alpha, with_u=with_u, n_m=n_m, n_epi=n_epi),
        grid=(n_m + 1, n_h),
        in_specs=in_specs,
        out_specs=out_specs,
        out_shape=out_shape,
        scratch_shapes=[pltpu.VMEM((2, tm, d), F32)],
        compiler_params=_params("arbitrary", "arbitrary"),
        name="ffn",
    )(*args)
    return (out[0], out[1]) if with_u else (out[0], None)


def _gelu_tanh(x):
    return x * (0.5 * (1.0 + jnp.tanh(0.7978845608028654 * (x + 0.044715 * (x * x * x)))))


def _col_groups(n, width=2 * V7X_MXU_WIDTH):
    return [(c, min(c + width, n)) for c in range(0, n, width)]


def _win_kernel(u_ref, w_ref, y_ref, z_ref):
    u = u_ref[...]
    dr = y_ref.shape[1]
    for c0, c1 in _col_groups(dr):
        y_ref[:, c0:c1] = _gelu_tanh(jnp.dot(u, w_ref[:, c0:c1], preferred_element_type=F32)).astype(BF16)
    for c0, c1 in _col_groups(dr):
        z_ref[:, c0:c1] = jnp.dot(u, w_ref[:, dr + c0:dr + c1], preferred_element_type=F32)


def _win_call(u, w_in, widx, tm):
    r, d = u.shape
    n = w_in.shape[2]
    dr = n // 2
    return pl.pallas_call(
        _win_kernel,
        grid=(r // tm,),
        in_specs=[
            pl.BlockSpec((tm, d), lambda i: (i, 0)),
            pl.BlockSpec((None, d, n), lambda i: (widx, 0, 0), pipeline_mode=pl.Buffered(1)),
        ],
        out_specs=[pl.BlockSpec((tm, dr), lambda i: (i, 0)), pl.BlockSpec((tm, dr), lambda i: (i, 0))],
        out_shape=[jax.ShapeDtypeStruct((r, dr), BF16), jax.ShapeDtypeStruct((r, dr), F32)],
        compiler_params=_params("arbitrary"),
        name="rec_in",
    )(u, w_in)


def _gate_window_plan(dr, bw):
    tn = V7X_MXU_WIDTH
    spans = []
    for n in range(dr // tn):
        c0 = n * tn
        lo = (c0 // bw) * bw
        hi = ((c0 + tn - 1) // bw + 1) * bw
        spans.append(((lo // V7X_LANES) * V7X_LANES, -(-hi // V7X_LANES) * V7X_LANES))
    kw = max(h - l for l, h in spans)
    return kw, tuple(min(l, dr - kw) for l, _ in spans)


def _gate_windows(w, kw, k0s):
    lead, bw = w.shape[:-3], w.shape[-1]
    tn = V7X_MXU_WIDTH
    w = w.astype(BF16)
    tiles = []
    for n, k0 in enumerate(k0s):
        c0 = n * tn
        h0, h1 = c0 // bw, (c0 + tn - 1) // bw
        nbk = h1 - h0 + 1
        diag = jnp.einsum("...hij,hg->...higj", w[..., h0:h1 + 1, :, :], jnp.eye(nbk, dtype=BF16))
        diag = diag.reshape(*lead, nbk * bw, nbk * bw)
        r_off, c_off = h0 * bw - k0, h0 * bw - c0
        cfg = [(0, 0, 0)] * len(lead) + [(r_off, kw - r_off - nbk * bw, 0), (c_off, tn - c_off - nbk * bw, 0)]
        tiles.append(lax.pad(diag, jnp.zeros((), BF16), cfg))
    return jnp.stack(tiles, axis=-3)


def _gates(zb, z_ref, wa_ref, wx_ref, ba_ref, bx_ref, lam_ref, a_s, b_s, kw, k0s):
    tn = V7X_MXU_WIDTH
    for n, k0 in enumerate(k0s):
        cs = slice(n * tn, (n + 1) * tn)
        zw = zb[:, k0:k0 + kw]
        ta = jnp.tanh(jnp.dot(zw, wa_ref[n], preferred_element_type=F32) + ba_ref[:, cs])
        tx = jnp.tanh(jnp.dot(zw, wx_ref[n], preferred_element_type=F32) + bx_ref[:, cs])
        nl = -lam_ref[:, cs]
        half = (0.5 * RG_C) * (jnp.maximum(nl, 0.0) + jnp.log1p(jnp.exp(-jnp.abs(nl))))
        q = ta * half + half
        a = jnp.exp(-q)
        a_s[:, cs] = a
        x = jnp.tanh(q) * (1.0 + a * a)
        root = jnp.where(x > 0.0, x * lax.rsqrt(x), 0.0)
        b_s[:, cs] = root * (0.5 * tx + 0.5) * z_ref[:, cs]


def _scan(a_s, b_s, dst, carry, reverse):
    tt, dr = a_s.shape
    tn = V7X_MXU_WIDTH
    sub = V7X_SUBLANES
    ng = tt // sub
    row = lax.broadcasted_iota(jnp.int32, (sub, tn), 0)
    entry = sub - 1 if reverse else 0

    def body(i, c):
        g = (ng - 1 - i) if reverse else i
        rs = pl.ds(pl.multiple_of(g * sub, sub), sub)
        for n in range(dr // tn):
            cs = slice(n * tn, (n + 1) * tn)
            a8 = a_s[rs, cs]
            b8 = b_s[rs, cs]
            b8 = b8 + jnp.where(row == entry, a8 * carry[:, cs], 0.0)
            a8 = jnp.where(row == entry, 0.0, a8)
            for s in (1, 2, 4):
                shift = sub - s if reverse else s
                b8 = a8 * pltpu.roll(b8, shift, 0) + b8
                if s < 4:
                    a8 = a8 * pltpu.roll(a8, shift, 0)
            dst[rs, cs] = b8
            edge = b8[0:1, :] if reverse else b8[sub - 1:sub, :]
            carry[:, cs] = jnp.broadcast_to(edge, (sub, tn))
        return c

    lax.fori_loop(0, ng, body, 0)


def _scan_fwd_kernel(zc_ref, zp_ref, zn_ref, cw_ref, cb_ref, wa_ref, wx_ref, ba_ref, bx_ref, lam_ref,
                     hf_ref, z_ref, xpad, zb, a_s, b_s, carry, *, kw, k0s, ns):
    tt = zc_ref.shape[0]
    tn = V7X_MXU_WIDTH
    sub = V7X_SUBLANES
    t = pl.program_id(1)

    @pl.when(t == 0)
    def _():
        carry[...] = jnp.zeros_like(carry)

    first = t <= 1
    last = jnp.logical_or(t == 0, t == ns)
    xpad[0:sub, :] = jnp.where(first, 0.0, zp_ref[...])
    xpad[sub:sub + tt, :] = zc_ref[...]
    xpad[sub + tt:2 * sub + tt, :] = jnp.where(last, 0.0, zn_ref[...])
    for n in range(len(k0s)):
        cs = slice(n * tn, (n + 1) * tn)
        xa = xpad[:, cs]
        acc = None
        for j in range(CONV_W):
            shift = (CONV_LEFT - j) % xa.shape[0]
            xs = pltpu.roll(xa, shift, 0) if shift else xa
            term = xs[sub:sub + tt, :] * cw_ref[j:j + 1, cs]
            acc = term if acc is None else acc + term
        z = acc + cb_ref[:, cs]
        z_ref[:, cs] = z
        zb[:, cs] = z.astype(BF16)

    _gates(zb, z_ref, wa_ref, wx_ref, ba_ref, bx_ref, lam_ref, a_s, b_s, kw, k0s)
    _scan(a_s, b_s, hf_ref, carry, reverse=False)


def _scan_bwd_kernel(z_ref, wa_ref, wx_ref, ba_ref, bx_ref, lam_ref, hf_ref, y_ref, o_ref, zb, a_s, b_s, h_s, carry,
                     *, kw, k0s):
    tn = V7X_MXU_WIDTH

    @pl.when(pl.program_id(1) == 0)
    def _():
        carry[...] = jnp.zeros_like(carry)

    zb[...] = z_ref[...].astype(BF16)
    _gates(zb, z_ref, wa_ref, wx_ref, ba_ref, bx_ref, lam_ref, a_s, b_s, kw, k0s)
    _scan(a_s, b_s, h_s, carry, reverse=True)
    for n in range(len(k0s)):
        cs = slice(n * tn, (n + 1) * tn)
        o_ref[:, cs] = (y_ref[:, cs].astype(F32) * (hf_ref[:, cs] + h_s[:, cs])).astype(BF16)


def _scan_calls(zpre, y, conv_w, conv_b, wa, wx, ba, bx, lam, kw, k0s, batch, seq, ctx_len):
    r, dr = zpre.shape
    tt = ctx_len
    ns = seq // tt
    nx = batch * seq // tt
    g8 = tt // V7X_SUBLANES
    nblk8 = r // V7X_SUBLANES
    nt = len(k0s)

    def cur(reverse):
        def f(b, t):
            st = (ns - t) if reverse else (t - 1)
            return jnp.where(t == 0, nx + b, b * ns + st)
        return f

    def tile(reverse):
        return pl.BlockSpec((tt, dr), lambda b, t: (cur(reverse)(b, t), 0))

    fw = cur(False)
    halo_p = pl.BlockSpec((V7X_SUBLANES, dr), lambda b, t: (jnp.maximum(fw(b, t) * g8 - 1, 0), 0))
    halo_n = pl.BlockSpec((V7X_SUBLANES, dr), lambda b, t: (jnp.minimum(fw(b, t) * g8 + g8, nblk8 - 1), 0))
    vec = lambda d: pl.BlockSpec((None, 1, dr), lambda b, t: (d, 0, 0))
    wspec = lambda d: pl.BlockSpec((None, nt, kw, V7X_MXU_WIDTH), lambda b, t: (d, 0, 0, 0),
                                   pipeline_mode=pl.Buffered(1))
    ba, bx, lam = (v.reshape(2, 1, dr) for v in (ba, bx, lam))
    gate_specs = lambda d: [wspec(d), wspec(d), vec(d), vec(d), vec(d)]
    gate_args = [wa, wx, ba, bx, lam]
    work = [
        pltpu.VMEM((tt, dr), BF16),
        pltpu.VMEM((tt, dr), F32),
        pltpu.VMEM((tt, dr), F32),
    ]
    state = pltpu.VMEM((V7X_SUBLANES, dr), F32)
    hf, z = pl.pallas_call(
        functools.partial(_scan_fwd_kernel, kw=kw, k0s=k0s, ns=ns),
        grid=(batch, ns + 1),
        in_specs=[tile(False), halo_p, halo_n, pl.BlockSpec((CONV_W, dr), lambda b, t: (0, 0)),
                  pl.BlockSpec((1, dr), lambda b, t: (0, 0))] + gate_specs(0),
        out_specs=[tile(False), tile(False)],
        out_shape=[jax.ShapeDtypeStruct((r, dr), F32), jax.ShapeDtypeStruct((r, dr), F32)],
        scratch_shapes=[pltpu.VMEM((tt + 2 * V7X_SUBLANES, dr), F32)] + work + [state],
        compiler_params=_params("arbitrary", "arbitrary"),
        name="rec_scan_fwd",
    )(zpre, zpre, zpre, conv_w, conv_b.reshape(1, dr), *gate_args)
    return pl.pallas_call(
        functools.partial(_scan_bwd_kernel, kw=kw, k0s=k0s),
        grid=(batch, ns + 1),
        in_specs=[tile(True)] + gate_specs(1) + [tile(True), tile(True)],
        out_specs=tile(True),
        out_shape=jax.ShapeDtypeStruct((r, dr), BF16),
        scratch_shapes=work + [pltpu.VMEM((tt, dr), F32), state],
        compiler_params=_params("arbitrary", "arbitrary"),
        name="rec_scan_bwd",
    )(z, *gate_args, hf, y)


def _rope_tables(seq, pad_rows):
    rows = seq // GRID_W
    row = jnp.repeat(jnp.arange(rows, dtype=F32), GRID_W)
    col = jnp.tile(jnp.arange(GRID_W, dtype=F32), rows)
    axis_dim = HEAD_DIM // 2
    inv_freq = ROPE_BASE ** (-jnp.arange(0, axis_dim, 2, dtype=F32) / axis_dim)
    ang = jnp.concatenate([row[:, None] * inv_freq, col[:, None] * inv_freq], axis=-1)
    cos, sin = jnp.cos(ang), jnp.sin(ang)
    cos2 = jnp.concatenate([cos, cos], axis=-1)
    sin2 = jnp.concatenate([-sin, sin], axis=-1)
    cos2 = jnp.concatenate([cos2, jnp.ones((pad_rows, HEAD_DIM), F32)], axis=0)
    sin2 = jnp.concatenate([sin2, jnp.zeros((pad_rows, HEAD_DIM), F32)], axis=0)
    return cos2, sin2


def _qkv_kernel(u_ref, w_ref, cos_ref, sin_ref, o_ref, *, nq, nkv, scale):
    u = u_ref[...]
    c = cos_ref[...]
    s = sin_ref[...]
    for c0, c1 in _col_groups(o_ref.shape[1]):
        p = jnp.dot(u, w_ref[:, c0:c1], preferred_element_type=F32)
        for h0 in range(c0, c1, HEAD_DIM):
            t = p[:, h0 - c0:h0 - c0 + HEAD_DIM]
            if h0 < nq + nkv:
                t = t * c + pltpu.roll(t, HEAD_DIM // 2, 1) * s
            if h0 < nq:
                t = t * scale
            o_ref[:, h0:h0 + HEAD_DIM] = t.astype(BF16)


def _qkv_call(u, w, widx, cos2, sin2, seq, rows_x, tm):
    r, d = u.shape
    n = w.shape[2]
    nkv = N_KV_HEADS * HEAD_DIM
    n_seq_tiles = seq // tm
    tab = pl.BlockSpec((tm, HEAD_DIM), lambda i: (jnp.where(i < rows_x // tm, i % n_seq_tiles, n_seq_tiles), 0))
    return pl.pallas_call(
        functools.partial(_qkv_kernel, nq=n - 2 * nkv, nkv=nkv, scale=HEAD_DIM ** -0.5),
        grid=(r // tm,),
        in_specs=[pl.BlockSpec((tm, d), lambda i: (i, 0)),
                  pl.BlockSpec((None, d, n), lambda i: (widx, 0, 0), pipeline_mode=pl.Buffered(1)), tab, tab],
        out_specs=pl.BlockSpec((tm, n), lambda i: (i, 0)),
        out_shape=jax.ShapeDtypeStruct((r, n), BF16),
        compiler_params=_params("arbitrary"),
        name="att_qkv",
    )(u, w, cos2, sin2)


def _attn_kernel(sink_ref, q_ref, kp_ref, kc_ref, kn_ref, vp_ref, vc_ref, vn_ref, kx_ref, vx_ref, o_ref,
                 k_all, v_all, bias, *, nb, g):
    j = pl.program_id(1)
    blk = WINDOW
    band = 3 * blk
    rows = g * blk
    is_lat = j < nb
    for dst, srcs in ((k_all, (kp_ref, kc_ref, kn_ref)), (v_all, (vp_ref, vc_ref, vn_ref))):
        for o, src in enumerate(srcs):
            dst[o * blk:(o + 1) * blk, :] = src[...]
    k_all[band:, :] = kx_ref[...]
    v_all[band:, :] = vx_ref[...]
    pen_p = jnp.where(jnp.logical_and(is_lat, j > 0), 0.0, NEG_INF)
    pen_c = jnp.where(is_lat, 0.0, NEG_INF)
    pen_n = jnp.where(jnp.logical_and(is_lat, j < nb - 1), 0.0, NEG_INF)
    qi = lax.broadcasted_iota(jnp.int32, (rows, blk), 0) & (blk - 1)
    ki = lax.broadcasted_iota(jnp.int32, (rows, blk), 1)
    bias[:, 0:blk] = jnp.where(ki >= qi, pen_p, NEG_INF)
    bias[:, blk:2 * blk] = jnp.full((rows, blk), pen_c, F32)
    bias[:, 2 * blk:band] = jnp.where(ki <= qi, pen_n, NEG_INF)
    bias[:, band:] = jnp.zeros((rows, bias.shape[1] - band), F32)
    nt_dims = (((1,), (1,)), ((), ()))
    for kh in range(N_KV_HEADS):
        ks = slice(kh * HEAD_DIM, (kh + 1) * HEAD_DIM)
        heads = [kh * g + gi for gi in range(g)]
        qs = jnp.concatenate([q_ref[:, h * HEAD_DIM:(h + 1) * HEAD_DIM] for h in heads], axis=0)
        sk = jnp.concatenate([jnp.full((blk, 1), sink_ref[h], F32) for h in heads], axis=0)
        s = lax.dot_general(qs, k_all[:, ks], nt_dims, preferred_element_type=F32) + bias[...]
        m = jnp.maximum(jnp.max(s, -1, keepdims=True), sk)
        p = jnp.exp(s - m)
        denom = jnp.exp(sk - m) + jnp.sum(p, -1, keepdims=True)
        o = jnp.dot(p.astype(BF16), v_all[:, ks], preferred_element_type=F32) / denom
        for gi, h in enumerate(heads):
            o_ref[:, h * HEAD_DIM:(h + 1) * HEAD_DIM] = o[gi * blk:(gi + 1) * blk].astype(BF16)


def _attn_call(qkv, sink, batch, seq, ctx_len, ctx_queries):
    r, n = qkv.shape
    nkv = N_KV_HEADS * HEAD_DIM
    d = n - 2 * nkv
    g = d // HEAD_DIM // N_KV_HEADS
    blk = WINDOW
    nb = seq // blk
    ncb = ctx_len // blk
    kcol = d // nkv
    x0 = batch * seq

    def qrow(b, j):
        return jnp.where(j < nb, b * nb + j, x0 // blk + b * ncb + (j - nb))

    def krow(b, j, off):
        return b * nb + jnp.clip(jnp.minimum(j, nb - 1) + off, 0, nb - 1)

    qspec = pl.BlockSpec((blk, d), lambda b, j: (qrow(b, j), 0))
    band = lambda off, col: pl.BlockSpec((blk, nkv), lambda b, j: (krow(b, j, off), col))
    ctxs = lambda col: pl.BlockSpec((ctx_len, nkv), lambda b, j: (x0 // ctx_len + b, col))
    return pl.pallas_call(
        functools.partial(_attn_kernel, nb=nb, g=g),
        grid=(batch, nb + ncb if ctx_queries else nb),
        in_specs=[
            pl.BlockSpec(memory_space=pltpu.SMEM),
            qspec,
            band(-1, kcol), band(0, kcol), band(1, kcol),
            band(-1, kcol + 1), band(0, kcol + 1), band(1, kcol + 1),
            ctxs(kcol), ctxs(kcol + 1),
        ],
        out_specs=qspec,
        out_shape=jax.ShapeDtypeStruct((r if ctx_queries else x0, d), BF16),
        scratch_shapes=[
            pltpu.VMEM((3 * blk + ctx_len, nkv), BF16),
            pltpu.VMEM((3 * blk + ctx_len, nkv), BF16),
            pltpu.VMEM((g * blk, 3 * blk + ctx_len), F32),
        ],
        compiler_params=_params("arbitrary", "arbitrary"),
        name="att_core",
    )(sink, qkv, qkv, qkv, qkv, qkv, qkv, qkv, qkv, qkv)


def kernel(x, c, ctx, c_ctx, mod_w, mod_b, ln_mix_g, ln_mix_b, ln_ffn_g, ln_ffn_b, ffn_w_gate, ffn_w_up, ffn_w_down,
           rec_w_in, rec_conv_w, rec_conv_b, rec_gate_a_w, rec_gate_a_b, rec_gate_x_w, rec_gate_x_b, rec_lambda,
           rec_w_out, att_w_qkv, att_sink, att_w_o):
    batch, seq, d = x.shape
    ctx_len = ctx.shape[1]
    depth = mod_w.shape[0]
    rows_x, rows_c = batch * seq, batch * ctx_len
    assert seq % ctx_len == 0 and ctx_len % WINDOW == 0 and seq % GRID_W == 0
    assert d % (N_KV_HEADS * HEAD_DIM) == 0
    alpha = (2.0 * depth) ** 0.25

    tm = _pick(seq, (512, 256, 128))
    while rows_c % tm:
        tm //= 2
    tm_big = 2 * tm if (seq % (2 * tm) == 0 and rows_c % (2 * tm) == 0) else tm

    def row_fn_for(t):
        return lambda i: jnp.minimum((i * t) // seq, batch)

    row_fn, row_fn_big = row_fn_for(tm), row_fn_for(tm_big)

    mr = -(-(batch + 1) // V7X_SUBLANES) * V7X_SUBLANES
    cs = jnp.zeros((mr, d), F32).at[:batch].set(c).at[batch].set(c_ctx)
    mods5 = _mods_call(cs, mod_w, mod_b).reshape(depth, mr, N_MOD, 1, d)

    cos2, sin2 = _rope_tables(seq, tm_big)
    dr = rec_w_out.shape[1]
    kw, k0s = _gate_window_plan(dr, dr // RNN_BLOCKS)
    wa_all = _gate_windows(0.5 * rec_gate_a_w, kw, k0s)
    wx_all = _gate_windows(0.5 * rec_gate_x_w, kw, k0s)
    ba_all, bx_all = 0.5 * rec_gate_a_b, 0.5 * rec_gate_x_b

    w_gate, w_up, w_down = ffn_w_gate.astype(BF16), ffn_w_up.astype(BF16), ffn_w_down.astype(BF16)
    w_in, w_out = rec_w_in.astype(BF16), rec_w_out.astype(BF16)
    w_qkv, w_o = att_w_qkv.astype(BF16), att_w_o.astype(BF16)

    h, u = _modulate_call(x.reshape(rows_x, d), ctx.reshape(rows_c, d), mods5, 0, row_fn, tm)
    for i in range(depth):
        j = i // N_MIXERS
        last = i == depth - 1
        rows = rows_x if last else rows_x + rows_c
        if i % N_MIXERS == 0:
            y, zpre = _win_call(u, w_in, j, tm)
            yh = _scan_calls(zpre, y, rec_conv_w[j], rec_conv_b[j], wa_all[j], wx_all[j], ba_all[j], bx_all[j],
                             rec_lambda[j], kw, k0s, batch, seq, ctx_len)
            h, u = _proj_ln_call(yh, w_out, j, h, rows, mods5, i, ln_mix_g[i], ln_mix_b[i], row_fn, tm, alpha, "rec_out")
        else:
            qkv = _qkv_call(u, w_qkv, j, cos2, sin2, seq, rows_x, tm_big)
            ao = _attn_call(qkv, att_sink[j], batch, seq, ctx_len, ctx_queries=not last)
            h, u = _proj_ln_call(ao, w_o, j, h, rows, mods5, i, ln_mix_g[i], ln_mix_b[i], row_fn, tm, alpha, "att_out")
        h, u = _ffn_call(u, w_gate, w_up, w_down, h, rows, mods5, i, None if last else i + 1, ln_ffn_g[i], ln_ffn_b[i],
                         row_fn_big, tm_big, alpha)
    return h.reshape(batch, seq, d)
```

```python
import functools

import jax
import jax.numpy as jnp
from jax import lax
from jax.experimental import pallas as pl
from jax.experimental.pallas import tpu as pltpu

HEAD_DIM = 128
N_KV_HEADS = 4
WINDOW = 128
GRID_W = 64
ROPE_BASE = 10000.0
RNN_BLOCKS = 16
CONV_W = 4
CONV_LEFT = 2
RG_C = 8.0
LN_EPS = 1e-5
NEG_INF = -1e30
N_MIXERS = 2
N_MOD = 6

V7X_LANES = 128
V7X_SUBLANES = 8
V7X_MXU_WIDTH = 256
V7X_VMEM_BYTES = 64 * 1024 * 1024
VMEM_LIMIT_BYTES = V7X_VMEM_BYTES - 8 * 1024 * 1024

F32 = jnp.float32
BF16 = jnp.bfloat16
EPILOGUE_ROWS = 128
LN_ROWS = 128


def _pick(n, cands):
    for c in cands:
        if n % c == 0:
            return c
    raise ValueError(f"no tile in {cands} divides {n}")


def _params(*sem):
    return pltpu.CompilerParams(dimension_semantics=sem, vmem_limit_bytes=VMEM_LIMIT_BYTES)


def _mod_spec(layer, chunk, row_fn):
    return lambda d: pl.BlockSpec((None, None, None, 1, d), lambda *g: (layer, row_fn(*g), chunk, 0, 0))


def _layer_norm(v, g, b):
    mu = jnp.mean(v, axis=-1, keepdims=True)
    d = v - mu
    var = jnp.mean(d * d, axis=-1, keepdims=True)
    return d * lax.rsqrt(var + LN_EPS) * g + b


def _deepnorm_epilogue(acc_ref, res_ref, gate_ref, lg_ref, lb_ref, h_ref, alpha, u_ref=None, sh_ref=None, sc_ref=None,
                       rows=None):
    r0, r1 = rows if rows is not None else (0, acc_ref.shape[0])
    ch = min(r1 - r0, LN_ROWS)
    gate, lg, lb = gate_ref[...], lg_ref[...], lb_ref[...]
    if u_ref is not None:
        sh, sc1 = sh_ref[...], 1.0 + sc_ref[...]
    for k in range((r1 - r0) // ch):
        rs = slice(r0 + k * ch, r0 + (k + 1) * ch)
        hn = _layer_norm(alpha * res_ref[rs, :] + gate * acc_ref[rs, :], lg, lb)
        h_ref[rs, :] = hn
        if u_ref is not None:
            u_ref[rs, :] = (hn * sc1 + sh).astype(BF16)


def _mods_kernel(cs_ref, w_ref, b_ref, o_ref):
    s = cs_ref[...]
    s = (s * jax.nn.sigmoid(s)).astype(BF16)
    o_ref[...] = jnp.dot(s, w_ref[...].astype(BF16), preferred_element_type=F32) + b_ref[...]


def _mods_call(cs, mod_w, mod_b):
    depth, d, n = mod_w.shape
    mr = cs.shape[0]
    tn = _pick(n, (1024, 512, 256, 128))
    return pl.pallas_call(
        _mods_kernel,
        grid=(depth, n // tn),
        in_specs=[
            pl.BlockSpec((mr, d), lambda l, j: (0, 0)),
            pl.BlockSpec((None, d, tn), lambda l, j: (l, 0, j)),
            pl.BlockSpec((None, 1, tn), lambda l, j: (l, 0, j)),
        ],
        out_specs=pl.BlockSpec((None, mr, tn), lambda l, j: (l, 0, j)),
        out_shape=jax.ShapeDtypeStruct((depth, mr, n), F32),
        compiler_params=_params("arbitrary", "arbitrary"),
        name="mods",
    )(cs, mod_w, mod_b.reshape(depth, 1, n))


def _proj_ln_kernel(a_ref, w_ref, res_ref, gate_ref, sh_ref, sc_ref, lg_ref, lb_ref, h_ref, u_ref, acc, *, alpha):
    tm = a_ref.shape[0]
    halves = 2 if tm % (2 * EPILOGUE_ROWS) == 0 else 1
    for hh in range(halves):
        rows = (hh * tm // halves, (hh + 1) * tm // halves)
        acc[rows[0]:rows[1], :] = jnp.dot(a_ref[rows[0]:rows[1], :], w_ref[...], preferred_element_type=F32)
        _deepnorm_epilogue(acc, res_ref, gate_ref, lg_ref, lb_ref, h_ref, alpha, u_ref, sh_ref, sc_ref, rows=rows)


def _proj_ln_call(a, w, widx, res, rows, mods5, layer, ln_g, ln_b, row_fn, tm, alpha, name):
    r, k = rows, a.shape[1]
    d = w.shape[2]
    vec = pl.BlockSpec((1, d), lambda i: (0, 0))
    return pl.pallas_call(
        functools.partial(_proj_ln_kernel, alpha=alpha),
        grid=(r // tm,),
        in_specs=[
            pl.BlockSpec((tm, k), lambda i: (i, 0)),
            pl.BlockSpec((None, k, d), lambda i: (widx, 0, 0), pipeline_mode=pl.Buffered(1)),
            pl.BlockSpec((tm, d), lambda i: (i, 0)),
            _mod_spec(layer, 2, row_fn)(d),
            _mod_spec(layer, 3, row_fn)(d),
            _mod_spec(layer, 4, row_fn)(d),
            vec,
            vec,
        ],
        out_specs=[pl.BlockSpec((tm, d), lambda i: (i, 0)), pl.BlockSpec((tm, d), lambda i: (i, 0))],
        out_shape=[jax.ShapeDtypeStruct((r, d), F32), jax.ShapeDtypeStruct((r, d), BF16)],
        scratch_shapes=[pltpu.VMEM((tm, d), F32)],
        compiler_params=_params("arbitrary"),
        name=name,
    )(a, w, res, mods5, mods5, mods5, ln_g.reshape(1, d), ln_b.reshape(1, d))


def _ffn_kernel(*refs, alpha, with_u, n_m, n_epi):
    if with_u:
        u_ref, wg_ref, wu_ref, wd_ref, res_ref, gate_ref, sh_ref, sc_ref, lg_ref, lb_ref, h_ref, un_ref, acc = refs
    else:
        u_ref, wg_ref, wu_ref, wd_ref, res_ref, gate_ref, lg_ref, lb_ref, h_ref, acc = refs
        un_ref = sh_ref = sc_ref = None
    i, j = pl.program_id(0), pl.program_id(1)
    slot = i % 2
    ch = res_ref.shape[0]

    def matmuls(first):
        u = u_ref[...]
        g = jnp.dot(u, wg_ref[...], preferred_element_type=F32)
        up = jnp.dot(u, wu_ref[...], preferred_element_type=F32)
        hid = (g * jax.nn.sigmoid(g) * up).astype(BF16)
        o = jnp.dot(hid, wd_ref[...], preferred_element_type=F32)
        if first:
            acc[slot] = o
        else:
            acc[slot] += o

    def epilogue():
        sub = min(ch, LN_ROWS)
        for k in range(ch // sub):
            rs = pl.ds(pl.multiple_of(j * ch + k * sub, sub), sub)
            ks = slice(k * sub, (k + 1) * sub)
            hn = _layer_norm(alpha * res_ref[ks, :] + gate_ref[...] * acc[1 - slot, rs, :], lg_ref[...], lb_ref[...])
            h_ref[ks, :] = hn
            if with_u:
                un_ref[ks, :] = (hn * (1.0 + sc_ref[...]) + sh_ref[...]).astype(BF16)

    land = jnp.logical_and
    live = i < n_m
    epi = land(i > 0, j < n_epi)

    @pl.when(land(land(live, epi), j == 0))
    def _():
        matmuls(True)
        epilogue()

    @pl.when(land(land(live, epi), j > 0))
    def _():
        matmuls(False)
        epilogue()

    @pl.when(land(land(live, jnp.logical_not(epi)), j == 0))
    def _():
        matmuls(True)

    @pl.when(land(land(live, jnp.logical_not(epi)), j > 0))
    def _():
        matmuls(False)

    @pl.when(land(jnp.logical_not(live), epi))
    def _():
        epilogue()


def _ffn_call(u, wg, wu, wd, res, rows, mods5, layer, next_layer, ln_g, ln_b, row_fn, tm, alpha):
    r, d = rows, u.shape[1]
    hid = wg.shape[2]
    th = _pick(hid, (512, 256, 128))
    n_m, n_h = r // tm, hid // th
    ch = min(tm, EPILOGUE_ROWS)
    n_epi = tm // ch
    assert n_epi <= n_h
    with_u = next_layer is not None
    prev = lambda i: jnp.maximum(i - 1, 0)
    row2 = lambda i, j: row_fn(prev(i))
    jw = lambda i, j: jnp.where(i < n_m, j, n_h - 1)
    vec = pl.BlockSpec((1, d), lambda i, j: (0, 0))
    chunk = pl.BlockSpec((ch, d), lambda i, j: (prev(i) * n_epi + jnp.where(i == 0, 0, jnp.minimum(j, n_epi - 1)), 0))
    in_specs = [
        pl.BlockSpec((tm, d), lambda i, j: (jnp.minimum(i, n_m - 1), 0)),
        pl.BlockSpec((None, d, th), lambda i, j: (layer, 0, jw(i, j))),
        pl.BlockSpec((None, d, th), lambda i, j: (layer, 0, jw(i, j))),
        pl.BlockSpec((None, th, d), lambda i, j: (layer, jw(i, j), 0)),
        chunk,
        _mod_spec(layer, 5, row2)(d),
    ]
    args = [u, wg, wu, wd, res, mods5]
    if with_u:
        in_specs += [_mod_spec(next_layer, 0, row2)(d), _mod_spec(next_layer, 1, row2)(d)]
        args += [mods5, mods5]
    in_specs += [vec, vec]
    args += [ln_g.reshape(1, d), ln_b.reshape(1, d)]
    out_specs = [chunk]
    out_shape = [jax.ShapeDtypeStruct((r, d), F32)]
    if with_u:
        out_specs.append(chunk)
        out_shape.append(jax.ShapeDtypeStruct((r, d), BF16))
    out = pl.pallas_call(
        functools.partial(_ffn_kernel, alpha=alpha, with_u=with_u, n_m=n_m, n_epi=n_epi),
        grid=(n_m + 1, n_h),
        in_specs=in_specs,
        out_specs=out_specs,
        out_shape=out_shape,
        scratch_shapes=[pltpu.VMEM((2, tm, d), F32)],
        compiler_params=_params("arbitrary", "arbitrary"),
        name="ffn",
    )(*args)
    return (out[0], out[1]) if with_u else (out[0], None)


def _gelu_tanh(x):
    return x * (0.5 * (1.0 + jnp.tanh(0.7978845608028654 * (x + 0.044715 * (x * x * x)))))


def _col_groups(n, width=2 * V7X_MXU_WIDTH):
    return [(c, min(c + width, n)) for c in range(0, n, width)]


def _win_kernel(*refs, n_lat):
    if n_lat is None:
        u_ref, w_ref, y_ref, z_ref = refs
        u = u_ref[...]
    else:
        x_ref, c_ref, sh_ref, sc_ref, w_ref, y_ref, z_ref, h_ref, u_s = refs

        def emit(src_ref):
            v = src_ref[...]
            h_ref[...] = v
            u_s[...] = (v * (1.0 + sc_ref[...]) + sh_ref[...]).astype(BF16)

        pl.when(pl.program_id(0) < n_lat)(lambda: emit(x_ref))
        pl.when(pl.program_id(0) >= n_lat)(lambda: emit(c_ref))
        u = u_s[...]
    dr = y_ref.shape[1]
    for c0, c1 in _col_groups(dr):
        y_ref[:, c0:c1] = _gelu_tanh(jnp.dot(u, w_ref[:, c0:c1], preferred_element_type=F32)).astype(BF16)
    for c0, c1 in _col_groups(dr):
        z_ref[:, c0:c1] = jnp.dot(u, w_ref[:, dr + c0:dr + c1], preferred_element_type=F32)


def _win_call(u, w_in, widx, tm, first=None):
    n = w_in.shape[2]
    d, dr = w_in.shape[1], n // 2
    wspec = pl.BlockSpec((None, d, n), lambda i: (widx, 0, 0), pipeline_mode=pl.Buffered(1))
    tile = lambda width: pl.BlockSpec((tm, width), lambda i: (i, 0))
    if first is None:
        r, n_lat = u.shape[0], None
        in_specs, args = [tile(d), wspec], [u, w_in]
        extra_out, extra_shape, scratch = [], [], []
    else:
        x2, c2, mods5, row_fn = first
        n_lat = x2.shape[0] // tm
        r = x2.shape[0] + c2.shape[0]
        in_specs = [
            pl.BlockSpec((tm, d), lambda i: (jnp.minimum(i, n_lat - 1), 0)),
            pl.BlockSpec((tm, d), lambda i: (jnp.maximum(i - n_lat, 0), 0)),
            _mod_spec(0, 0, row_fn)(d),
            _mod_spec(0, 1, row_fn)(d),
            wspec,
        ]
        args = [x2, c2, mods5, mods5, w_in]
        extra_out, extra_shape = [tile(d)], [jax.ShapeDtypeStruct((r, d), F32)]
        scratch = [pltpu.VMEM((tm, d), BF16)]
    return pl.pallas_call(
        functools.partial(_win_kernel, n_lat=n_lat),
        grid=(r // tm,),
        in_specs=in_specs,
        out_specs=[tile(dr), tile(dr)] + extra_out,
        out_shape=[jax.ShapeDtypeStruct((r, dr), BF16), jax.ShapeDtypeStruct((r, dr), F32)] + extra_shape,
        scratch_shapes=scratch,
        compiler_params=_params("arbitrary"),
        name="rec_in",
    )(*args)


def _gate_window_plan(dr, bw):
    tn = V7X_MXU_WIDTH
    spans = []
    for n in range(dr // tn):
        c0 = n * tn
        lo = (c0 // bw) * bw
        hi = ((c0 + tn - 1) // bw + 1) * bw
        spans.append(((lo // V7X_LANES) * V7X_LANES, -(-hi // V7X_LANES) * V7X_LANES))
    kw = max(h - l for l, h in spans)
    return kw, tuple(min(l, dr - kw) for l, _ in spans)


def _gate_windows(w, kw, k0s):
    lead, bw = w.shape[:-3], w.shape[-1]
    tn = V7X_MXU_WIDTH
    w = w.astype(BF16)
    zero = jnp.zeros((), BF16)
    keep = [(0, 0, 0)] * (len(lead) + 1)
    tiles = []
    for n, k0 in enumerate(k0s):
        c0 = n * tn
        strips, filled = [], 0
        for h in range(c0 // bw, (c0 + tn - 1) // bw + 1):
            wr, wc = h * bw - k0, h * bw - c0
            sr0, sr1 = max(0, -wr), bw - max(0, wr + bw - kw)
            sc0, sc1 = max(0, -wc), bw - max(0, wc + bw - tn)
            if max(wr, 0) > filled:
                strips.append(jnp.zeros((*lead, max(wr, 0) - filled, tn), BF16))
            col = max(wc, 0)
            strips.append(lax.pad(w[..., h, sr0:sr1, sc0:sc1], zero, keep + [(col, tn - col - (sc1 - sc0), 0)]))
            filled = max(wr, 0) + sr1 - sr0
        if filled < kw:
            strips.append(jnp.zeros((*lead, kw - filled, tn), BF16))
        tiles.append(jnp.concatenate(strips, axis=-2))
    return jnp.stack(tiles, axis=-3)


def _gates(zb, z_ref, wa_ref, wx_ref, ba_ref, bx_ref, lam_ref, a_s, b_s, kw, k0s):
    tn = V7X_MXU_WIDTH
    for n, k0 in enumerate(k0s):
        cs = slice(n * tn, (n + 1) * tn)
        zw = zb[:, k0:k0 + kw]
        ta = jnp.tanh(jnp.dot(zw, wa_ref[n], preferred_element_type=F32) + ba_ref[:, cs])
        tx = jnp.tanh(jnp.dot(zw, wx_ref[n], preferred_element_type=F32) + bx_ref[:, cs])
        nl = -lam_ref[:, cs]
        half = (0.5 * RG_C) * (jnp.maximum(nl, 0.0) + jnp.log1p(jnp.exp(-jnp.abs(nl))))
        q = ta * half + half
        a = jnp.exp(-q)
        a_s[:, cs] = a
        x = jnp.tanh(q) * (1.0 + a * a)
        root = jnp.where(x > 0.0, x * lax.rsqrt(x), 0.0)
        b_s[:, cs] = root * (0.5 * tx + 0.5) * z_ref[:, cs]


def _scan(a_s, b_s, dst, carry, reverse):
    tt, dr = a_s.shape
    tn = V7X_MXU_WIDTH
    sub = V7X_SUBLANES
    ng = tt // sub
    row = lax.broadcasted_iota(jnp.int32, (sub, tn), 0)
    entry = sub - 1 if reverse else 0

    def body(i, c):
        g = (ng - 1 - i) if reverse else i
        rs = pl.ds(pl.multiple_of(g * sub, sub), sub)
        for n in range(dr // tn):
            cs = slice(n * tn, (n + 1) * tn)
            a8 = a_s[rs, cs]
            b8 = b_s[rs, cs]
            b8 = b8 + jnp.where(row == entry, a8 * carry[:, cs], 0.0)
            a8 = jnp.where(row == entry, 0.0, a8)
            for s in (1, 2, 4):
                shift = sub - s if reverse else s
                b8 = a8 * pltpu.roll(b8, shift, 0) + b8
                if s < 4:
                    a8 = a8 * pltpu.roll(a8, shift, 0)
            dst[rs, cs] = b8
            edge = b8[0:1, :] if reverse else b8[sub - 1:sub, :]
            carry[:, cs] = jnp.broadcast_to(edge, (sub, tn))
        return c

    lax.fori_loop(0, ng, body, 0)


def _scan_fwd_kernel(zc_ref, zp_ref, zn_ref, cw_ref, cb_ref, wa_ref, wx_ref, ba_ref, bx_ref, lam_ref,
                     hf_ref, z_ref, xpad, zb, a_s, b_s, carry, *, kw, k0s, ns):
    tt = zc_ref.shape[0]
    tn = V7X_MXU_WIDTH
    sub = V7X_SUBLANES
    t = pl.program_id(1)

    @pl.when(t == 0)
    def _():
        carry[...] = jnp.zeros_like(carry)

    first = t <= 1
    last = jnp.logical_or(t == 0, t == ns)
    xpad[0:sub, :] = jnp.where(first, 0.0, zp_ref[...])
    xpad[sub:sub + tt, :] = zc_ref[...]
    xpad[sub + tt:2 * sub + tt, :] = jnp.where(last, 0.0, zn_ref[...])
    for n in range(len(k0s)):
        cs = slice(n * tn, (n + 1) * tn)
        xa = xpad[:, cs]
        acc = None
        for j in range(CONV_W):
            shift = (CONV_LEFT - j) % xa.shape[0]
            xs = pltpu.roll(xa, shift, 0) if shift else xa
            term = xs[sub:sub + tt, :] * cw_ref[j:j + 1, cs]
            acc = term if acc is None else acc + term
        z = acc + cb_ref[:, cs]
        z_ref[:, cs] = z
        zb[:, cs] = z.astype(BF16)

    _gates(zb, z_ref, wa_ref, wx_ref, ba_ref, bx_ref, lam_ref, a_s, b_s, kw, k0s)
    _scan(a_s, b_s, hf_ref, carry, reverse=False)


def _scan_bwd_kernel(z_ref, wa_ref, wx_ref, ba_ref, bx_ref, lam_ref, hf_ref, y_ref, o_ref, zb, a_s, b_s, h_s, carry,
                     *, kw, k0s):
    tn = V7X_MXU_WIDTH

    @pl.when(pl.program_id(1) == 0)
    def _():
        carry[...] = jnp.zeros_like(carry)

    zb[...] = z_ref[...].astype(BF16)
    _gates(zb, z_ref, wa_ref, wx_ref, ba_ref, bx_ref, lam_ref, a_s, b_s, kw, k0s)
    _scan(a_s, b_s, h_s, carry, reverse=True)
    for n in range(len(k0s)):
        cs = slice(n * tn, (n + 1) * tn)
        o_ref[:, cs] = (y_ref[:, cs].astype(F32) * (hf_ref[:, cs] + h_s[:, cs])).astype(BF16)


def _scan_calls(zpre, y, conv_w, conv_b, wa, wx, ba, bx, lam, kw, k0s, batch, seq, ctx_len):
    r, dr = zpre.shape
    tt = ctx_len
    ns = seq // tt
    nx = batch * seq // tt
    g8 = tt // V7X_SUBLANES
    nblk8 = r // V7X_SUBLANES
    nt = len(k0s)

    def cur(reverse):
        def f(b, t):
            st = (ns - t) if reverse else (t - 1)
            return jnp.where(t == 0, nx + b, b * ns + st)
        return f

    def tile(reverse):
        return pl.BlockSpec((tt, dr), lambda b, t: (cur(reverse)(b, t), 0))

    fw = cur(False)
    halo_p = pl.BlockSpec((V7X_SUBLANES, dr), lambda b, t: (jnp.maximum(fw(b, t) * g8 - 1, 0), 0))
    halo_n = pl.BlockSpec((V7X_SUBLANES, dr), lambda b, t: (jnp.minimum(fw(b, t) * g8 + g8, nblk8 - 1), 0))
    vec = lambda d: pl.BlockSpec((None, 1, dr), lambda b, t: (d, 0, 0))
    wspec = lambda d: pl.BlockSpec((None, nt, kw, V7X_MXU_WIDTH), lambda b, t: (d, 0, 0, 0),
                                   pipeline_mode=pl.Buffered(1))
    ba, bx, lam = (v.reshape(2, 1, dr) for v in (ba, bx, lam))
    gate_specs = lambda d: [wspec(d), wspec(d), vec(d), vec(d), vec(d)]
    gate_args = [wa, wx, ba, bx, lam]
    work = [
        pltpu.VMEM((tt, dr), BF16),
        pltpu.VMEM((tt, dr), F32),
        pltpu.VMEM((tt, dr), F32),
    ]
    state = pltpu.VMEM((V7X_SUBLANES, dr), F32)
    hf, z = pl.pallas_call(
        functools.partial(_scan_fwd_kernel, kw=kw, k0s=k0s, ns=ns),
        grid=(batch, ns + 1),
        in_specs=[tile(False), halo_p, halo_n, pl.BlockSpec((CONV_W, dr), lambda b, t: (0, 0)),
                  pl.BlockSpec((1, dr), lambda b, t: (0, 0))] + gate_specs(0),
        out_specs=[tile(False), tile(False)],
        out_shape=[jax.ShapeDtypeStruct((r, dr), F32), jax.ShapeDtypeStruct((r, dr), F32)],
        scratch_shapes=[pltpu.VMEM((tt + 2 * V7X_SUBLANES, dr), F32)] + work + [state],
        compiler_params=_params("arbitrary", "arbitrary"),
        name="rec_scan_fwd",
    )(zpre, zpre, zpre, conv_w, conv_b.reshape(1, dr), *gate_args)
    return pl.pallas_call(
        functools.partial(_scan_bwd_kernel, kw=kw, k0s=k0s),
        grid=(batch, ns + 1),
        in_specs=[tile(True)] + gate_specs(1) + [tile(True), tile(True)],
        out_specs=tile(True),
        out_shape=jax.ShapeDtypeStruct((r, dr), BF16),
        scratch_shapes=work + [pltpu.VMEM((tt, dr), F32), state],
        compiler_params=_params("arbitrary", "arbitrary"),
        name="rec_scan_bwd",
    )(z, *gate_args, hf, y)


def _rope_tables(seq, pad_rows):
    rows = seq // GRID_W
    row = jnp.repeat(jnp.arange(rows, dtype=F32), GRID_W)
    col = jnp.tile(jnp.arange(GRID_W, dtype=F32), rows)
    axis_dim = HEAD_DIM // 2
    inv_freq = ROPE_BASE ** (-jnp.arange(0, axis_dim, 2, dtype=F32) / axis_dim)
    ang = jnp.concatenate([row[:, None] * inv_freq, col[:, None] * inv_freq], axis=-1)
    cos, sin = jnp.cos(ang), jnp.sin(ang)
    cos2 = jnp.concatenate([cos, cos], axis=-1)
    sin2 = jnp.concatenate([-sin, sin], axis=-1)
    cos2 = jnp.concatenate([cos2, jnp.ones((pad_rows, HEAD_DIM), F32)], axis=0)
    sin2 = jnp.concatenate([sin2, jnp.zeros((pad_rows, HEAD_DIM), F32)], axis=0)
    return cos2, sin2


def _qkv_kernel(u_ref, w_ref, cos_ref, sin_ref, o_ref, *, nq, nkv, scale):
    u = u_ref[...]
    c = cos_ref[...]
    s = sin_ref[...]
    for c0, c1 in _col_groups(o_ref.shape[1]):
        p = jnp.dot(u, w_ref[:, c0:c1], preferred_element_type=F32)
        for h0 in range(c0, c1, HEAD_DIM):
            t = p[:, h0 - c0:h0 - c0 + HEAD_DIM]
            if h0 < nq + nkv:
                t = t * c + pltpu.roll(t, HEAD_DIM // 2, 1) * s
            if h0 < nq:
                t = t * scale
            o_ref[:, h0:h0 + HEAD_DIM] = t.astype(BF16)


def _qkv_call(u, w, widx, cos2, sin2, seq, rows_x, tm):
    r, d = u.shape
    n = w.shape[2]
    nkv = N_KV_HEADS * HEAD_DIM
    n_seq_tiles = seq // tm
    tab = pl.BlockSpec((tm, HEAD_DIM), lambda i: (jnp.where(i < rows_x // tm, i % n_seq_tiles, n_seq_tiles), 0))
    return pl.pallas_call(
        functools.partial(_qkv_kernel, nq=n - 2 * nkv, nkv=nkv, scale=HEAD_DIM ** -0.5),
        grid=(r // tm,),
        in_specs=[pl.BlockSpec((tm, d), lambda i: (i, 0)),
                  pl.BlockSpec((None, d, n), lambda i: (widx, 0, 0), pipeline_mode=pl.Buffered(1)), tab, tab],
        out_specs=pl.BlockSpec((tm, n), lambda i: (i, 0)),
        out_shape=jax.ShapeDtypeStruct((r, n), BF16),
        compiler_params=_params("arbitrary"),
        name="att_qkv",
    )(u, w, cos2, sin2)


def _attn_masks(gq, ctx_len):
    blk = WINDOW
    qi = lax.broadcasted_iota(jnp.int32, (gq * blk, blk), 0) % blk
    ki = lax.broadcasted_iota(jnp.int32, (gq * blk, blk), 1)
    on = jnp.zeros((gq * blk, blk), F32)
    off = jnp.full((gq * blk, blk), NEG_INF, F32)
    prev = jnp.where(ki >= qi, 0.0, NEG_INF).astype(F32)
    nxt = jnp.where(ki <= qi, 0.0, NEG_INF).astype(F32)
    ctx_on = jnp.zeros((gq * blk, ctx_len), F32)
    variants = ((off, on, nxt), (prev, on, nxt), (prev, on, off), (off, off, off))
    return jnp.stack([jnp.concatenate(v + (ctx_on,), axis=1) for v in variants])


def _attn_kernel(sink_ref, q_ref, kp_ref, kc_ref, kn_ref, vp_ref, vc_ref, vn_ref, kx_ref, vx_ref, bias, o_ref,
                 k_all, v_all, *, g):
    blk = WINDOW
    band = 3 * blk
    for dst, srcs in ((k_all, (kp_ref, kc_ref, kn_ref)), (v_all, (vp_ref, vc_ref, vn_ref))):
        for o, src in enumerate(srcs):
            dst[o * blk:(o + 1) * blk, :] = src[...]
    k_all[band:, :] = kx_ref[...]
    v_all[band:, :] = vx_ref[...]
    nt_dims = (((1,), (1,)), ((), ()))
    gq = bias.shape[0] // blk
    for h0 in range(0, N_KV_HEADS * g, gq):
        kh = h0 // g
        ks = slice(kh * HEAD_DIM, (kh + 1) * HEAD_DIM)
        heads = range(h0, h0 + gq)
        qs = jnp.concatenate([q_ref[:, h * HEAD_DIM:(h + 1) * HEAD_DIM] for h in heads], axis=0)
        sk = jnp.concatenate([jnp.full((blk, 1), sink_ref[h], F32) for h in heads], axis=0)
        s = lax.dot_general(qs, k_all[:, ks], nt_dims, preferred_element_type=F32) + bias[...]
        m = jnp.maximum(jnp.max(s, -1, keepdims=True), sk)
        p = jnp.exp(s - m)
        denom = jnp.exp(sk - m) + jnp.sum(p, -1, keepdims=True)
        o = jnp.dot(p.astype(BF16), v_all[:, ks], preferred_element_type=F32) / denom
        for gi, h in enumerate(heads):
            o_ref[:, h * HEAD_DIM:(h + 1) * HEAD_DIM] = o[gi * blk:(gi + 1) * blk].astype(BF16)


def _attn_call(qkv, sink, batch, seq, ctx_len, ctx_queries):
    r, n = qkv.shape
    nkv = N_KV_HEADS * HEAD_DIM
    d = n - 2 * nkv
    g = d // HEAD_DIM // N_KV_HEADS
    gq = g
    blk = WINDOW
    nb = seq // blk
    ncb = ctx_len // blk
    kcol = d // nkv
    x0 = batch * seq

    def qrow(b, j):
        return jnp.where(j < nb, b * nb + j, x0 // blk + b * ncb + (j - nb))

    def krow(b, j, off):
        return b * nb + jnp.clip(jnp.minimum(j, nb - 1) + off, 0, nb - 1)

    assert nb >= 2
    qspec = pl.BlockSpec((blk, d), lambda b, j: (qrow(b, j), 0))
    band = lambda off, col: pl.BlockSpec((blk, nkv), lambda b, j: (krow(b, j, off), col))
    ctxs = lambda col: pl.BlockSpec((ctx_len, nkv), lambda b, j: (x0 // ctx_len + b, col))
    nkeys = 3 * blk + ctx_len
    mask = pl.BlockSpec((None, gq * blk, nkeys),
                        lambda b, j: (jnp.where(j >= nb, 3, jnp.where(j == 0, 0, jnp.where(j == nb - 1, 2, 1))), 0, 0))
    return pl.pallas_call(
        functools.partial(_attn_kernel, g=g),
        grid=(batch, nb + ncb if ctx_queries else nb),
        in_specs=[
            pl.BlockSpec(memory_space=pltpu.SMEM),
            qspec,
            band(-1, kcol), band(0, kcol), band(1, kcol),
            band(-1, kcol + 1), band(0, kcol + 1), band(1, kcol + 1),
            ctxs(kcol), ctxs(kcol + 1),
            mask,
        ],
        out_specs=qspec,
        out_shape=jax.ShapeDtypeStruct((r if ctx_queries else x0, d), BF16),
        scratch_shapes=[
            pltpu.VMEM((nkeys, nkv), BF16),
            pltpu.VMEM((nkeys, nkv), BF16),
        ],
        compiler_params=_params("arbitrary", "arbitrary"),
        name="att_core",
    )(sink, qkv, qkv, qkv, qkv, qkv, qkv, qkv, qkv, qkv, _attn_masks(gq, ctx_len))


def kernel(x, c, ctx, c_ctx, mod_w, mod_b, ln_mix_g, ln_mix_b, ln_ffn_g, ln_ffn_b, ffn_w_gate, ffn_w_up, ffn_w_down,
           rec_w_in, rec_conv_w, rec_conv_b, rec_gate_a_w, rec_gate_a_b, rec_gate_x_w, rec_gate_x_b, rec_lambda,
           rec_w_out, att_w_qkv, att_sink, att_w_o):
    batch, seq, d = x.shape
    ctx_len = ctx.shape[1]
    depth = mod_w.shape[0]
    rows_x, rows_c = batch * seq, batch * ctx_len
    assert seq % ctx_len == 0 and ctx_len % WINDOW == 0 and seq % GRID_W == 0
    assert d % (N_KV_HEADS * HEAD_DIM) == 0
    alpha = (2.0 * depth) ** 0.25

    tm = _pick(seq, (512, 256, 128))
    while rows_c % tm:
        tm //= 2
    tm_big = 2 * tm if (seq % (2 * tm) == 0 and rows_c % (2 * tm) == 0) else tm

    def row_fn_for(t):
        return lambda i: jnp.minimum((i * t) // seq, batch)

    row_fn, row_fn_big = row_fn_for(tm), row_fn_for(tm_big)

    mr = -(-(batch + 1) // V7X_SUBLANES) * V7X_SUBLANES
    cs = jnp.zeros((mr, d), F32).at[:batch].set(c).at[batch].set(c_ctx)
    mods5 = _mods_call(cs, mod_w, mod_b).reshape(depth, mr, N_MOD, 1, d)

    cos2, sin2 = _rope_tables(seq, tm_big)
    dr = rec_w_out.shape[1]
    kw, k0s = _gate_window_plan(dr, dr // RNN_BLOCKS)
    wa_all = _gate_windows(0.5 * rec_gate_a_w, kw, k0s)
    wx_all = _gate_windows(0.5 * rec_gate_x_w, kw, k0s)
    ba_all, bx_all = 0.5 * rec_gate_a_b, 0.5 * rec_gate_x_b

    w_gate, w_up, w_down = ffn_w_gate.astype(BF16), ffn_w_up.astype(BF16), ffn_w_down.astype(BF16)
    w_in, w_out = rec_w_in.astype(BF16), rec_w_out.astype(BF16)
    w_qkv, w_o = att_w_qkv.astype(BF16), att_w_o.astype(BF16)

    h = u = None
    for i in range(depth):
        j = i // N_MIXERS
        last = i == depth - 1
        rows = rows_x if last else rows_x + rows_c
        if i % N_MIXERS == 0:
            if i == 0:
                tm0 = max(tm // 2, WINDOW)
                first = (x.reshape(rows_x, d), ctx.reshape(rows_c, d), mods5, row_fn_for(tm0))
                y, zpre, h = _win_call(None, w_in, j, tm0, first)
            else:
                y, zpre = _win_call(u, w_in, j, tm)
            yh = _scan_calls(zpre, y, rec_conv_w[j], rec_conv_b[j], wa_all[j], wx_all[j], ba_all[j], bx_all[j],
                             rec_lambda[j], kw, k0s, batch, seq, ctx_len)
            h, u = _proj_ln_call(yh, w_out, j, h, rows, mods5, i, ln_mix_g[i], ln_mix_b[i], row_fn, tm, alpha, "rec_out")
        else:
            qkv = _qkv_call(u, w_qkv, j, cos2, sin2, seq, rows_x, tm_big)
            ao = _attn_call(qkv, att_sink[j], batch, seq, ctx_len, ctx_queries=not last)
            h, u = _proj_ln_call(ao, w_o, j, h, rows, mods5, i, ln_mix_g[i], ln_mix_b[i], row_fn, tm, alpha, "att_out")
        h, u = _ffn_call(u, w_gate, w_up, w_down, h, rows, mods5, i, None if last else i + 1, ln_ffn_g[i], ln_ffn_b[i],
                         row_fn_big, tm_big, alpha)
    return h.reshape(batch, seq, d)
```

```python
import functools

import jax
import jax.numpy as jnp
from jax import lax
from jax.experimental import pallas as pl
from jax.experimental.pallas import tpu as pltpu

HEAD_DIM = 128
N_KV_HEADS = 4
WINDOW = 128
GRID_W = 64
ROPE_BASE = 10000.0
RNN_BLOCKS = 16
CONV_W = 4
CONV_LEFT = 2
RG_C = 8.0
LN_EPS = 1e-5
NEG_INF = -1e30
N_MIXERS = 2
N_MOD = 6
LOG2_E = 1.4426950408889634

V7X_LANES = 128
V7X_SUBLANES = 8
V7X_MXU_WIDTH = 256
V7X_VMEM_BYTES = 64 * 1024 * 1024
VMEM_LIMIT_BYTES = V7X_VMEM_BYTES - 8 * 1024 * 1024

F32 = jnp.float32
BF16 = jnp.bfloat16
EPILOGUE_ROWS = 128
LN_ROWS = 128


def _pick(n, cands):
    for c in cands:
        if n % c == 0:
            return c
    raise ValueError(f"no tile in {cands} divides {n}")


def _params(*sem):
    return pltpu.CompilerParams(dimension_semantics=sem, vmem_limit_bytes=VMEM_LIMIT_BYTES)


def _mod_spec(layer, chunk, row_fn):
    return lambda d: pl.BlockSpec((None, None, None, 1, d), lambda *g: (layer, row_fn(*g), chunk, 0, 0))


def _layer_norm(v, g, b):
    mu = jnp.mean(v, axis=-1, keepdims=True)
    d = v - mu
    var = jnp.mean(d * d, axis=-1, keepdims=True)
    return d * lax.rsqrt(var + LN_EPS) * g + b


def _deepnorm_epilogue(acc_ref, res_ref, gate_ref, lg_ref, lb_ref, h_ref, alpha, u_ref=None, sh_ref=None, sc_ref=None,
                       rows=None):
    r0, r1 = rows if rows is not None else (0, acc_ref.shape[0])
    ch = min(r1 - r0, LN_ROWS)
    gate, lg, lb = gate_ref[...], lg_ref[...], lb_ref[...]
    if u_ref is not None:
        sh, sc1 = sh_ref[...], 1.0 + sc_ref[...]
    for k in range((r1 - r0) // ch):
        rs = slice(r0 + k * ch, r0 + (k + 1) * ch)
        hn = _layer_norm(alpha * res_ref[rs, :] + gate * acc_ref[rs, :], lg, lb)
        h_ref[rs, :] = hn
        if u_ref is not None:
            u_ref[rs, :] = (hn * sc1 + sh).astype(BF16)


def _mods_kernel(cs_ref, w_ref, b_ref, o_ref):
    s = cs_ref[...]
    s = (s * jax.nn.sigmoid(s)).astype(BF16)
    o_ref[...] = jnp.dot(s, w_ref[...].astype(BF16), preferred_element_type=F32) + b_ref[...]


def _mods_call(cs, mod_w, mod_b):
    depth, d, n = mod_w.shape
    mr = cs.shape[0]
    tn = _pick(n, (1024, 512, 256, 128))
    return pl.pallas_call(
        _mods_kernel,
        grid=(depth, n // tn),
        in_specs=[
            pl.BlockSpec((mr, d), lambda l, j: (0, 0)),
            pl.BlockSpec((None, d, tn), lambda l, j: (l, 0, j)),
            pl.BlockSpec((None, 1, tn), lambda l, j: (l, 0, j)),
        ],
        out_specs=pl.BlockSpec((None, mr, tn), lambda l, j: (l, 0, j)),
        out_shape=jax.ShapeDtypeStruct((depth, mr, n), F32),
        compiler_params=_params("arbitrary", "arbitrary"),
        name="mods",
    )(cs, mod_w, mod_b.reshape(depth, 1, n))


def _proj_ln_kernel(a_ref, w_ref, res_ref, gate_ref, sh_ref, sc_ref, lg_ref, lb_ref, h_ref, u_ref, acc, *, alpha):
    tm = a_ref.shape[0]
    halves = 2 if tm % (2 * EPILOGUE_ROWS) == 0 else 1
    for hh in range(halves):
        rows = (hh * tm // halves, (hh + 1) * tm // halves)
        acc[rows[0]:rows[1], :] = jnp.dot(a_ref[rows[0]:rows[1], :], w_ref[...], preferred_element_type=F32)
        _deepnorm_epilogue(acc, res_ref, gate_ref, lg_ref, lb_ref, h_ref, alpha, u_ref, sh_ref, sc_ref, rows=rows)


def _proj_ln_call(a, w, widx, res, rows, mods5, layer, ln_g, ln_b, row_fn, tm, alpha, name):
    r, k = rows, a.shape[1]
    d = w.shape[2]
    vec = pl.BlockSpec((1, d), lambda i: (0, 0))
    return pl.pallas_call(
        functools.partial(_proj_ln_kernel, alpha=alpha),
        grid=(r // tm,),
        in_specs=[
            pl.BlockSpec((tm, k), lambda i: (i, 0)),
            pl.BlockSpec((None, k, d), lambda i: (widx, 0, 0), pipeline_mode=pl.Buffered(1)),
            pl.BlockSpec((tm, d), lambda i: (i, 0)),
            _mod_spec(layer, 2, row_fn)(d),
            _mod_spec(layer, 3, row_fn)(d),
            _mod_spec(layer, 4, row_fn)(d),
            vec,
            vec,
        ],
        out_specs=[pl.BlockSpec((tm, d), lambda i: (i, 0)), pl.BlockSpec((tm, d), lambda i: (i, 0))],
        out_shape=[jax.ShapeDtypeStruct((r, d), F32), jax.ShapeDtypeStruct((r, d), BF16)],
        scratch_shapes=[pltpu.VMEM((tm, d), F32)],
        compiler_params=_params("arbitrary"),
        name=name,
    )(a, w, res, mods5, mods5, mods5, ln_g.reshape(1, d), ln_b.reshape(1, d))


def _ffn_kernel(*refs, alpha, with_u, n_m, n_epi):
    if with_u:
        u_ref, wg_ref, wu_ref, wd_ref, res_ref, gate_ref, sh_ref, sc_ref, lg_ref, lb_ref, h_ref, un_ref, acc, done = refs
    else:
        u_ref, wg_ref, wu_ref, wd_ref, res_ref, gate_ref, lg_ref, lb_ref, h_ref, acc, done = refs
        un_ref = sh_ref = sc_ref = None
    i, j = pl.program_id(0), pl.program_id(1)
    ch = res_ref.shape[0]

    def matmuls(first):
        u = u_ref[...]
        g = jnp.dot(u, wg_ref[...], preferred_element_type=F32)
        up = jnp.dot(u, wu_ref[...], preferred_element_type=F32)
        hid = (g * jax.nn.sigmoid(g) * up).astype(BF16)
        o = jnp.dot(hid, wd_ref[...], preferred_element_type=F32)
        if first:
            acc[...] = o
        else:
            acc[...] += o

    def epilogue():
        sub = min(ch, LN_ROWS)
        for k in range(ch // sub):
            rs = pl.ds(pl.multiple_of(j * ch + k * sub, sub), sub)
            ks = slice(k * sub, (k + 1) * sub)
            hn = _layer_norm(alpha * res_ref[ks, :] + gate_ref[...] * done[rs, :], lg_ref[...], lb_ref[...])
            h_ref[ks, :] = hn
            if with_u:
                un_ref[ks, :] = (hn * (1.0 + sc_ref[...]) + sh_ref[...]).astype(BF16)

    land = jnp.logical_and
    live = i < n_m
    epi = land(i > 0, j < n_epi)

    @pl.when(land(i > 0, j == 0))
    def _():
        done[...] = acc[...]

    @pl.when(land(land(live, epi), j == 0))
    def _():
        matmuls(True)
        epilogue()

    @pl.when(land(land(live, epi), j > 0))
    def _():
        matmuls(False)
        epilogue()

    @pl.when(land(land(live, jnp.logical_not(epi)), j == 0))
    def _():
        matmuls(True)

    @pl.when(land(land(live, jnp.logical_not(epi)), j > 0))
    def _():
        matmuls(False)

    @pl.when(land(jnp.logical_not(live), epi))
    def _():
        epilogue()


def _ffn_call(u, wg, wu, wd, res, rows, mods5, layer, next_layer, ln_g, ln_b, row_fn, tm, alpha):
    r, d = rows, u.shape[1]
    hid = wg.shape[2]
    th = _pick(hid, (512, 256, 128))
    n_m, n_h = r // tm, hid // th
    ch = min(tm, EPILOGUE_ROWS)
    n_epi = tm // ch
    assert n_epi <= n_h
    with_u = next_layer is not None
    prev = lambda i: jnp.maximum(i - 1, 0)
    row2 = lambda i, j: row_fn(prev(i))
    jw = lambda i, j: jnp.where(i < n_m, j, n_h - 1)
    vec = pl.BlockSpec((1, d), lambda i, j: (0, 0))
    chunk = pl.BlockSpec((ch, d), lambda i, j: (prev(i) * n_epi + jnp.where(i == 0, 0, jnp.minimum(j, n_epi - 1)), 0))
    in_specs = [
        pl.BlockSpec((tm, d), lambda i, j: (jnp.minimum(i, n_m - 1), 0)),
        pl.BlockSpec((None, d, th), lambda i, j: (layer, 0, jw(i, j))),
        pl.BlockSpec((None, d, th), lambda i, j: (layer, 0, jw(i, j))),
        pl.BlockSpec((None, th, d), lambda i, j: (layer, jw(i, j), 0)),
        chunk,
        _mod_spec(layer, 5, row2)(d),
    ]
    args = [u, wg, wu, wd, res, mods5]
    if with_u:
        in_specs += [_mod_spec(next_layer, 0, row2)(d), _mod_spec(next_layer, 1, row2)(d)]
        args += [mods5, mods5]
    in_specs += [vec, vec]
    args += [ln_g.reshape(1, d), ln_b.reshape(1, d)]
    out_specs = [chunk]
    out_shape = [jax.ShapeDtypeStruct((r, d), F32)]
    if with_u:
        out_specs.append(chunk)
        out_shape.append(jax.ShapeDtypeStruct((r, d), BF16))
    out = pl.pallas_call(
        functools.partial(_ffn_kernel, alpha=alpha, with_u=with_u, n_m=n_m, n_epi=n_epi),
        grid=(n_m + 1, n_h),
        in_specs=in_specs,
        out_specs=out_specs,
        out_shape=out_shape,
        scratch_shapes=[pltpu.VMEM((tm, d), F32), pltpu.VMEM((tm, d), F32)],
        compiler_params=_params("arbitrary", "arbitrary"),
        name="ffn",
    )(*args)
    return (out[0], out[1]) if with_u else (out[0], None)


def _gelu_tanh(x):
    return x * (0.5 * (1.0 + jnp.tanh(0.7978845608028654 * (x + 0.044715 * (x * x * x)))))


def _col_groups(n, width=2 * V7X_MXU_WIDTH):
    return [(c, min(c + width, n)) for c in range(0, n, width)]


def _win_kernel(*refs, n_lat):
    if n_lat is None:
        u_ref, w_ref, y_ref, z_ref = refs
        u = u_ref[...]
    else:
        x_ref, c_ref, sh_ref, sc_ref, w_ref, y_ref, z_ref, h_ref, u_s = refs

        def emit(src_ref):
            v = src_ref[...]
            h_ref[...] = v
            u_s[...] = (v * (1.0 + sc_ref[...]) + sh_ref[...]).astype(BF16)

        pl.when(pl.program_id(0) < n_lat)(lambda: emit(x_ref))
        pl.when(pl.program_id(0) >= n_lat)(lambda: emit(c_ref))
        u = u_s[...]
    dr = y_ref.shape[1]
    for c0, c1 in _col_groups(dr):
        y_ref[:, c0:c1] = _gelu_tanh(jnp.dot(u, w_ref[:, c0:c1], preferred_element_type=F32)).astype(BF16)
    for c0, c1 in _col_groups(dr):
        z_ref[:, c0:c1] = jnp.dot(u, w_ref[:, dr + c0:dr + c1], preferred_element_type=F32)


def _win_call(u, w_in, widx, tm, first=None):
    n = w_in.shape[2]
    d, dr = w_in.shape[1], n // 2
    wspec = pl.BlockSpec((None, d, n), lambda i: (widx, 0, 0), pipeline_mode=pl.Buffered(1))
    tile = lambda width: pl.BlockSpec((tm, width), lambda i: (i, 0))
    if first is None:
        r, n_lat = u.shape[0], None
        in_specs, args = [tile(d), wspec], [u, w_in]
        extra_out, extra_shape, scratch = [], [], []
    else:
        x2, c2, mods5, row_fn = first
        n_lat = x2.shape[0] // tm
        r = x2.shape[0] + c2.shape[0]
        in_specs = [
            pl.BlockSpec((tm, d), lambda i: (jnp.minimum(i, n_lat - 1), 0)),
            pl.BlockSpec((tm, d), lambda i: (jnp.maximum(i - n_lat, 0), 0)),
            _mod_spec(0, 0, row_fn)(d),
            _mod_spec(0, 1, row_fn)(d),
            wspec,
        ]
        args = [x2, c2, mods5, mods5, w_in]
        extra_out, extra_shape = [tile(d)], [jax.ShapeDtypeStruct((r, d), F32)]
        scratch = [pltpu.VMEM((tm, d), BF16)]
    return pl.pallas_call(
        functools.partial(_win_kernel, n_lat=n_lat),
        grid=(r // tm,),
        in_specs=in_specs,
        out_specs=[tile(dr), tile(dr)] + extra_out,
        out_shape=[jax.ShapeDtypeStruct((r, dr), BF16), jax.ShapeDtypeStruct((r, dr), F32)] + extra_shape,
        scratch_shapes=scratch,
        compiler_params=_params("arbitrary"),
        name="rec_in",
    )(*args)


def _gate_window_plan(dr, bw):
    tn = V7X_MXU_WIDTH
    spans = []
    for n in range(dr // tn):
        c0 = n * tn
        lo = (c0 // bw) * bw
        hi = ((c0 + tn - 1) // bw + 1) * bw
        spans.append(((lo // V7X_LANES) * V7X_LANES, -(-hi // V7X_LANES) * V7X_LANES))
    kw = max(h - l for l, h in spans)
    return kw, tuple(min(l, dr - kw) for l, _ in spans)


def _gate_windows(w, kw, k0s):
    lead, bw = w.shape[:-3], w.shape[-1]
    tn = V7X_MXU_WIDTH
    w = w.astype(BF16)
    zero = jnp.zeros((), BF16)
    keep = [(0, 0, 0)] * (len(lead) + 1)
    tiles = []
    for n, k0 in enumerate(k0s):
        c0 = n * tn
        strips, filled = [], 0
        for h in range(c0 // bw, (c0 + tn - 1) // bw + 1):
            wr, wc = h * bw - k0, h * bw - c0
            sr0, sr1 = max(0, -wr), bw - max(0, wr + bw - kw)
            sc0, sc1 = max(0, -wc), bw - max(0, wc + bw - tn)
            if max(wr, 0) > filled:
                strips.append(jnp.zeros((*lead, max(wr, 0) - filled, tn), BF16))
            col = max(wc, 0)
            strips.append(lax.pad(w[..., h, sr0:sr1, sc0:sc1], zero, keep + [(col, tn - col - (sc1 - sc0), 0)]))
            filled = max(wr, 0) + sr1 - sr0
        if filled < kw:
            strips.append(jnp.zeros((*lead, kw - filled, tn), BF16))
        tiles.append(jnp.concatenate(strips, axis=-2))
    return jnp.stack(tiles, axis=-3)


def _gates_scan(zb, z_ref, wa_ref, wx_ref, ba_ref, bx_ref, lam_ref, a_s, b_s, dst, carry, kw, k0s, reverse):
    tt = zb.shape[0]
    tn = V7X_MXU_WIDTH
    sub = V7X_SUBLANES
    ng = tt // sub
    row = lax.broadcasted_iota(jnp.int32, (sub, tn), 0)
    entry = sub - 1 if reverse else 0
    for n, k0 in enumerate(k0s):
        cs = slice(n * tn, (n + 1) * tn)
        zw = zb[:, k0:k0 + kw]
        ta = jnp.tanh(jnp.dot(zw, wa_ref[n], preferred_element_type=F32) + ba_ref[:, cs])
        tx = jnp.tanh(jnp.dot(zw, wx_ref[n], preferred_element_type=F32) + bx_ref[:, cs])
        nl = -lam_ref[:, cs]
        half = (0.5 * RG_C) * (jnp.maximum(nl, 0.0) + jnp.log1p(jnp.exp(-jnp.abs(nl))))
        q = ta * half + half
        a = jnp.exp(-q)
        a_s[:, cs] = a
        x = jnp.tanh(q) * (1.0 + a * a)
        root = jnp.where(x > 0.0, x * lax.rsqrt(x), 0.0)
        b_s[:, cs] = root * (0.5 * tx + 0.5) * z_ref[:, cs]

        h = carry[:, cs]
        for g in (range(ng - 1, -1, -1) if reverse else range(ng)):
            rs = slice(g * sub, (g + 1) * sub)
            a8 = a_s[rs, cs]
            b8 = b_s[rs, cs]
            b8 = b8 + jnp.where(row == entry, a8 * h, 0.0)
            a8 = jnp.where(row == entry, 0.0, a8)
            for s in (1, 2, 4):
                shift = sub - s if reverse else s
                b8 = a8 * pltpu.roll(b8, shift, 0) + b8
                if s < 4:
                    a8 = a8 * pltpu.roll(a8, shift, 0)
            dst[rs, cs] = b8
            h = jnp.broadcast_to(b8[0:1, :] if reverse else b8[sub - 1:sub, :], (sub, tn))
        carry[:, cs] = h


def _scan_fwd_kernel(zc_ref, zp_ref, zn_ref, cw_ref, cb_ref, wa_ref, wx_ref, ba_ref, bx_ref, lam_ref,
                     hf_ref, z_ref, xpad, zb, a_s, b_s, carry, *, kw, k0s, ns):
    tt = zc_ref.shape[0]
    tn = V7X_MXU_WIDTH
    sub = V7X_SUBLANES
    t = pl.program_id(1)

    @pl.when(t == 0)
    def _():
        carry[...] = jnp.zeros_like(carry)

    first = t <= 1
    last = jnp.logical_or(t == 0, t == ns)
    xpad[0:sub, :] = jnp.where(first, 0.0, zp_ref[...])
    xpad[sub:sub + tt, :] = zc_ref[...]
    xpad[sub + tt:2 * sub + tt, :] = jnp.where(last, 0.0, zn_ref[...])
    for n in range(len(k0s)):
        cs = slice(n * tn, (n + 1) * tn)
        xa = xpad[:, cs]
        acc = None
        for j in range(CONV_W):
            shift = (CONV_LEFT - j) % xa.shape[0]
            xs = pltpu.roll(xa, shift, 0) if shift else xa
            term = xs[sub:sub + tt, :] * cw_ref[j:j + 1, cs]
            acc = term if acc is None else acc + term
        z = acc + cb_ref[:, cs]
        z_ref[:, cs] = z
        zb[:, cs] = z.astype(BF16)

    _gates_scan(zb, z_ref, wa_ref, wx_ref, ba_ref, bx_ref, lam_ref, a_s, b_s, hf_ref, carry, kw, k0s, reverse=False)


def _scan_bwd_kernel(z_ref, wa_ref, wx_ref, ba_ref, bx_ref, lam_ref, hf_ref, y_ref, o_ref, zb, a_s, b_s, h_s, carry,
                     *, kw, k0s):
    tn = V7X_MXU_WIDTH

    @pl.when(pl.program_id(1) == 0)
    def _():
        carry[...] = jnp.zeros_like(carry)

    zb[...] = z_ref[...].astype(BF16)
    _gates_scan(zb, z_ref, wa_ref, wx_ref, ba_ref, bx_ref, lam_ref, a_s, b_s, h_s, carry, kw, k0s, reverse=True)
    for n in range(len(k0s)):
        cs = slice(n * tn, (n + 1) * tn)
        o_ref[:, cs] = (y_ref[:, cs].astype(F32) * (hf_ref[:, cs] + h_s[:, cs])).astype(BF16)


def _scan_calls(zpre, y, conv_w, conv_b, wa, wx, ba, bx, lam, kw, k0s, batch, seq, ctx_len):
    r, dr = zpre.shape
    tt = ctx_len
    ns = seq // tt
    nx = batch * seq // tt
    g8 = tt // V7X_SUBLANES
    nblk8 = r // V7X_SUBLANES
    nt = len(k0s)

    def cur(reverse):
        def f(b, t):
            st = (ns - t) if reverse else (t - 1)
            return jnp.where(t == 0, nx + b, b * ns + st)
        return f

    def tile(reverse):
        return pl.BlockSpec((tt, dr), lambda b, t: (cur(reverse)(b, t), 0))

    fw = cur(False)
    halo_p = pl.BlockSpec((V7X_SUBLANES, dr), lambda b, t: (jnp.maximum(fw(b, t) * g8 - 1, 0), 0))
    halo_n = pl.BlockSpec((V7X_SUBLANES, dr), lambda b, t: (jnp.minimum(fw(b, t) * g8 + g8, nblk8 - 1), 0))
    vec = lambda d: pl.BlockSpec((None, 1, dr), lambda b, t: (d, 0, 0))
    wspec = lambda d: pl.BlockSpec((None, nt, kw, V7X_MXU_WIDTH), lambda b, t: (d, 0, 0, 0),
                                   pipeline_mode=pl.Buffered(1))
    ba, bx, lam = (v.reshape(2, 1, dr) for v in (ba, bx, lam))
    gate_specs = lambda d: [wspec(d), wspec(d), vec(d), vec(d), vec(d)]
    gate_args = [wa, wx, ba, bx, lam]
    work = [
        pltpu.VMEM((tt, dr), BF16),
        pltpu.VMEM((tt, dr), F32),
        pltpu.VMEM((tt, dr), F32),
    ]
    state = pltpu.VMEM((V7X_SUBLANES, dr), F32)
    hf, z = pl.pallas_call(
        functools.partial(_scan_fwd_kernel, kw=kw, k0s=k0s, ns=ns),
        grid=(batch, ns + 1),
        in_specs=[tile(False), halo_p, halo_n, pl.BlockSpec((CONV_W, dr), lambda b, t: (0, 0)),
                  pl.BlockSpec((1, dr), lambda b, t: (0, 0))] + gate_specs(0),
        out_specs=[tile(False), tile(False)],
        out_shape=[jax.ShapeDtypeStruct((r, dr), F32), jax.ShapeDtypeStruct((r, dr), F32)],
        scratch_shapes=[pltpu.VMEM((tt + 2 * V7X_SUBLANES, dr), F32)] + work + [state],
        compiler_params=_params("arbitrary", "arbitrary"),
        name="rec_scan_fwd",
    )(zpre, zpre, zpre, conv_w, conv_b.reshape(1, dr), *gate_args)
    return pl.pallas_call(
        functools.partial(_scan_bwd_kernel, kw=kw, k0s=k0s),
        grid=(batch, ns + 1),
        in_specs=[tile(True)] + gate_specs(1) + [tile(True), tile(True)],
        out_specs=tile(True),
        out_shape=jax.ShapeDtypeStruct((r, dr), BF16),
        scratch_shapes=work + [pltpu.VMEM((tt, dr), F32), state],
        compiler_params=_params("arbitrary", "arbitrary"),
        name="rec_scan_bwd",
    )(z, *gate_args, hf, y)


def _rope_tables(seq, pad_rows):
    rows = seq // GRID_W
    row = jnp.repeat(jnp.arange(rows, dtype=F32), GRID_W)
    col = jnp.tile(jnp.arange(GRID_W, dtype=F32), rows)
    axis_dim = HEAD_DIM // 2
    inv_freq = ROPE_BASE ** (-jnp.arange(0, axis_dim, 2, dtype=F32) / axis_dim)
    ang = jnp.concatenate([row[:, None] * inv_freq, col[:, None] * inv_freq], axis=-1)
    cos, sin = jnp.cos(ang), jnp.sin(ang)
    cos2 = jnp.concatenate([cos, cos], axis=-1)
    sin2 = jnp.concatenate([-sin, sin], axis=-1)
    cos2 = jnp.concatenate([cos2, jnp.ones((pad_rows, HEAD_DIM), F32)], axis=0)
    sin2 = jnp.concatenate([sin2, jnp.zeros((pad_rows, HEAD_DIM), F32)], axis=0)
    return cos2, sin2


def _qkv_kernel(u_ref, w_ref, cos_ref, sin_ref, o_ref, *, nq, nkv, scale):
    u = u_ref[...]
    c = cos_ref[...]
    s = sin_ref[...]
    for c0, c1 in _col_groups(o_ref.shape[1]):
        p = jnp.dot(u, w_ref[:, c0:c1], preferred_element_type=F32)
        for h0 in range(c0, c1, HEAD_DIM):
            t = p[:, h0 - c0:h0 - c0 + HEAD_DIM]
            if h0 < nq + nkv:
                t = t * c + pltpu.roll(t, HEAD_DIM // 2, 1) * s
            if h0 < nq:
                t = t * scale
            o_ref[:, h0:h0 + HEAD_DIM] = t.astype(BF16)


def _qkv_call(u, w, widx, cos2, sin2, seq, rows_x, tm):
    r, d = u.shape
    n = w.shape[2]
    nkv = N_KV_HEADS * HEAD_DIM
    n_seq_tiles = seq // tm
    tab = pl.BlockSpec((tm, HEAD_DIM), lambda i: (jnp.where(i < rows_x // tm, i % n_seq_tiles, n_seq_tiles), 0))
    return pl.pallas_call(
        functools.partial(_qkv_kernel, nq=n - 2 * nkv, nkv=nkv, scale=HEAD_DIM ** -0.5 * LOG2_E),
        grid=(r // tm,),
        in_specs=[pl.BlockSpec((tm, d), lambda i: (i, 0)),
                  pl.BlockSpec((None, d, n), lambda i: (widx, 0, 0), pipeline_mode=pl.Buffered(1)), tab, tab],
        out_specs=pl.BlockSpec((tm, n), lambda i: (i, 0)),
        out_shape=jax.ShapeDtypeStruct((r, n), BF16),
        compiler_params=_params("arbitrary"),
        name="att_qkv",
    )(u, w, cos2, sin2)


def _attn_masks(gq, ctx_len):
    blk = WINDOW
    qi = lax.broadcasted_iota(jnp.int32, (gq * blk, blk), 0) % blk
    ki = lax.broadcasted_iota(jnp.int32, (gq * blk, blk), 1)
    on = jnp.zeros((gq * blk, blk), F32)
    off = jnp.full((gq * blk, blk), NEG_INF, F32)
    prev = jnp.where(ki >= qi, 0.0, NEG_INF).astype(F32)
    nxt = jnp.where(ki <= qi, 0.0, NEG_INF).astype(F32)
    ctx_on = jnp.zeros((gq * blk, ctx_len), F32)
    variants = ((off, on, nxt), (prev, on, nxt), (prev, on, off), (off, off, off))
    return jnp.stack([jnp.concatenate(v + (ctx_on,), axis=1) for v in variants])


def _attn_kernel(sink_ref, q_ref, kp_ref, kc_ref, kn_ref, vp_ref, vc_ref, vn_ref, kx_ref, vx_ref, bias, o_ref,
                 k_all, v_all, *, g):
    blk = WINDOW
    band = 3 * blk
    for dst, srcs in ((k_all, (kp_ref, kc_ref, kn_ref)), (v_all, (vp_ref, vc_ref, vn_ref))):
        for o, src in enumerate(srcs):
            dst[o * blk:(o + 1) * blk, :] = src[...]
    k_all[band:, :] = kx_ref[...]
    v_all[band:, :] = vx_ref[...]
    nt_dims = (((1,), (1,)), ((), ()))
    gq = bias.shape[0] // blk
    for h0 in range(0, N_KV_HEADS * g, gq):
        kh = h0 // g
        ks = slice(kh * HEAD_DIM, (kh + 1) * HEAD_DIM)
        heads = range(h0, h0 + gq)
        qs = jnp.concatenate([q_ref[:, h * HEAD_DIM:(h + 1) * HEAD_DIM] for h in heads], axis=0)
        sk = jnp.concatenate([jnp.full((blk, 1), sink_ref[h] * LOG2_E, F32) for h in heads], axis=0)
        s = lax.dot_general(qs, k_all[:, ks], nt_dims, preferred_element_type=F32) + bias[...]
        m = jnp.maximum(jnp.max(s, -1, keepdims=True), sk)
        p = jnp.exp2(s - m)
        denom = jnp.exp2(sk - m) + jnp.sum(p, -1, keepdims=True)
        o = jnp.dot(p.astype(BF16), v_all[:, ks], preferred_element_type=F32) / denom
        for gi, h in enumerate(heads):
            o_ref[:, h * HEAD_DIM:(h + 1) * HEAD_DIM] = o[gi * blk:(gi + 1) * blk].astype(BF16)


def _attn_call(qkv, sink, batch, seq, ctx_len, ctx_queries):
    r, n = qkv.shape
    nkv = N_KV_HEADS * HEAD_DIM
    d = n - 2 * nkv
    g = d // HEAD_DIM // N_KV_HEADS
    gq = g
    blk = WINDOW
    nb = seq // blk
    ncb = ctx_len // blk
    kcol = d // nkv
    x0 = batch * seq

    def qrow(b, j):
        return jnp.where(j < nb, b * nb + j, x0 // blk + b * ncb + (j - nb))

    def krow(b, j, off):
        return b * nb + jnp.clip(jnp.minimum(j, nb - 1) + off, 0, nb - 1)

    assert nb >= 2
    qspec = pl.BlockSpec((blk, d), lambda b, j: (qrow(b, j), 0))
    band = lambda off, col: pl.BlockSpec((blk, nkv), lambda b, j: (krow(b, j, off), col))
    ctxs = lambda col: pl.BlockSpec((ctx_len, nkv), lambda b, j: (x0 // ctx_len + b, col))
    nkeys = 3 * blk + ctx_len
    mask = pl.BlockSpec((None, gq * blk, nkeys),
                        lambda b, j: (jnp.where(j >= nb, 3, jnp.where(j == 0, 0, jnp.where(j == nb - 1, 2, 1))), 0, 0))
    return pl.pallas_call(
        functools.partial(_attn_kernel, g=g),
        grid=(batch, nb + ncb if ctx_queries else nb),
        in_specs=[
            pl.BlockSpec(memory_space=pltpu.SMEM),
            qspec,
            band(-1, kcol), band(0, kcol), band(1, kcol),
            band(-1, kcol + 1), band(0, kcol + 1), band(1, kcol + 1),
            ctxs(kcol), ctxs(kcol + 1),
            mask,
        ],
        out_specs=qspec,
        out_shape=jax.ShapeDtypeStruct((r if ctx_queries else x0, d), BF16),
        scratch_shapes=[
            pltpu.VMEM((nkeys, nkv), BF16),
            pltpu.VMEM((nkeys, nkv), BF16),
        ],
        compiler_params=_params("arbitrary", "arbitrary"),
        name="att_core",
    )(sink, qkv, qkv, qkv, qkv, qkv, qkv, qkv, qkv, qkv, _attn_masks(gq, ctx_len))


def kernel(x, c, ctx, c_ctx, mod_w, mod_b, ln_mix_g, ln_mix_b, ln_ffn_g, ln_ffn_b, ffn_w_gate, ffn_w_up, ffn_w_down,
           rec_w_in, rec_conv_w, rec_conv_b, rec_gate_a_w, rec_gate_a_b, rec_gate_x_w, rec_gate_x_b, rec_lambda,
           rec_w_out, att_w_qkv, att_sink, att_w_o):
    batch, seq, d = x.shape
    ctx_len = ctx.shape[1]
    depth = mod_w.shape[0]
    rows_x, rows_c = batch * seq, batch * ctx_len
    assert seq % ctx_len == 0 and ctx_len % WINDOW == 0 and seq % GRID_W == 0
    assert d % (N_KV_HEADS * HEAD_DIM) == 0
    alpha = (2.0 * depth) ** 0.25

    tm = _pick(seq, (512, 256, 128))
    while rows_c % tm:
        tm //= 2
    tm_big = 2 * tm if (seq % (2 * tm) == 0 and rows_c % (2 * tm) == 0) else tm

    def row_fn_for(t):
        return lambda i: jnp.minimum((i * t) // seq, batch)

    row_fn, row_fn_big = row_fn_for(tm), row_fn_for(tm_big)

    mr = -(-(batch + 1) // V7X_SUBLANES) * V7X_SUBLANES
    cs = jnp.zeros((mr, d), F32).at[:batch].set(c).at[batch].set(c_ctx)
    mods5 = _mods_call(cs, mod_w, mod_b).reshape(depth, mr, N_MOD, 1, d)

    cos2, sin2 = _rope_tables(seq, tm_big)
    dr = rec_w_out.shape[1]
    kw, k0s = _gate_window_plan(dr, dr // RNN_BLOCKS)
    wa_all = _gate_windows(0.5 * rec_gate_a_w, kw, k0s)
    wx_all = _gate_windows(0.5 * rec_gate_x_w, kw, k0s)
    ba_all, bx_all = 0.5 * rec_gate_a_b, 0.5 * rec_gate_x_b

    w_gate, w_up, w_down = ffn_w_gate.astype(BF16), ffn_w_up.astype(BF16), ffn_w_down.astype(BF16)
    w_in, w_out = rec_w_in.astype(BF16), rec_w_out.astype(BF16)
    w_qkv, w_o = att_w_qkv.astype(BF16), att_w_o.astype(BF16)

    h = u = None
    for i in range(depth):
        j = i // N_MIXERS
        last = i == depth - 1
        rows = rows_x if last else rows_x + rows_c
        if i % N_MIXERS == 0:
            if i == 0:
                tm0 = max(tm // 2, WINDOW)
                first = (x.reshape(rows_x, d), ctx.reshape(rows_c, d), mods5, row_fn_for(tm0))
                y, zpre, h = _win_call(None, w_in, j, tm0, first)
            else:
                y, zpre = _win_call(u, w_in, j, tm)
            yh = _scan_calls(zpre, y, rec_conv_w[j], rec_conv_b[j], wa_all[j], wx_all[j], ba_all[j], bx_all[j],
                             rec_lambda[j], kw, k0s, batch, seq, ctx_len)
            h, u = _proj_ln_call(yh, w_out, j, h, rows, mods5, i, ln_mix_g[i], ln_mix_b[i], row_fn, tm, alpha, "rec_out")
        else:
            qkv = _qkv_call(u, w_qkv, j, cos2, sin2, seq, rows_x, tm_big)
            ao = _attn_call(qkv, att_sink[j], batch, seq, ctx_len, ctx_queries=not last)
            h, u = _proj_ln_call(ao, w_o, j, h, rows, mods5, i, ln_mix_g[i], ln_mix_b[i], row_fn, tm, alpha, "att_out")
        h, u = _ffn_call(u, w_gate, w_up, w_down, h, rows, mods5, i, None if last else i + 1, ln_ffn_g[i], ln_ffn_b[i],
                         row_fn_big, tm_big, alpha)
    return h.reshape(batch, seq, d)
```

```python
import functools

import jax
import jax.numpy as jnp
from jax import lax
from jax.experimental import pallas as pl
from jax.experimental.pallas import tpu as pltpu

HEAD_DIM = 128
N_KV_HEADS = 4
WINDOW = 128
GRID_W = 64
ROPE_BASE = 10000.0
RNN_BLOCKS = 16
CONV_W = 4
CONV_LEFT = 2
RG_C = 8.0
LN_EPS = 1e-5
NEG_INF = -1e30
N_MIXERS = 2
N_MOD = 6
LOG2_E = 1.4426950408889634

V7X_LANES = 128
V7X_SUBLANES = 8
V7X_MXU_WIDTH = 256
V7X_VMEM_BYTES = 64 * 1024 * 1024
VMEM_LIMIT_BYTES = V7X_VMEM_BYTES - 8 * 1024 * 1024

F32 = jnp.float32
BF16 = jnp.bfloat16
EPILOGUE_ROWS = 128
LN_ROWS = 128


def _pick(n, cands):
    for c in cands:
        if n % c == 0:
            return c
    raise ValueError(f"no tile in {cands} divides {n}")


def _params(*sem):
    return pltpu.CompilerParams(dimension_semantics=sem, vmem_limit_bytes=VMEM_LIMIT_BYTES)


def _mod_spec(layer, chunk, row_fn):
    return lambda d: pl.BlockSpec((None, None, None, 1, d), lambda *g: (layer, row_fn(*g), chunk, 0, 0))


def _layer_norm(v, g, b):
    mu = jnp.mean(v, axis=-1, keepdims=True)
    d = v - mu
    var = jnp.mean(d * d, axis=-1, keepdims=True)
    return d * lax.rsqrt(var + LN_EPS) * g + b


def _deepnorm_epilogue(acc_ref, res_ref, gate_ref, lg_ref, lb_ref, h_ref, alpha, u_ref=None, sh_ref=None, sc_ref=None,
                       rows=None):
    r0, r1 = rows if rows is not None else (0, acc_ref.shape[0])
    ch = min(r1 - r0, LN_ROWS)
    gate, lg, lb = gate_ref[...], lg_ref[...], lb_ref[...]
    if u_ref is not None:
        sh, sc1 = sh_ref[...], 1.0 + sc_ref[...]
    for k in range((r1 - r0) // ch):
        rs = slice(r0 + k * ch, r0 + (k + 1) * ch)
        hn = _layer_norm(alpha * res_ref[rs, :] + gate * acc_ref[rs, :], lg, lb)
        h_ref[rs, :] = hn
        if u_ref is not None:
            u_ref[rs, :] = (hn * sc1 + sh).astype(BF16)


def _mods_kernel(cs_ref, w_ref, b_ref, o_ref):
    s = cs_ref[...]
    s = (s * jax.nn.sigmoid(s)).astype(BF16)
    o_ref[...] = jnp.dot(s, w_ref[...].astype(BF16), preferred_element_type=F32) + b_ref[...]


def _mods_call(cs, mod_w, mod_b):
    depth, d, n = mod_w.shape
    mr = cs.shape[0]
    tn = _pick(n, (1024, 512, 256, 128))
    return pl.pallas_call(
        _mods_kernel,
        grid=(depth, n // tn),
        in_specs=[
            pl.BlockSpec((mr, d), lambda l, j: (0, 0)),
            pl.BlockSpec((None, d, tn), lambda l, j: (l, 0, j)),
            pl.BlockSpec((None, 1, tn), lambda l, j: (l, 0, j)),
        ],
        out_specs=pl.BlockSpec((None, mr, tn), lambda l, j: (l, 0, j)),
        out_shape=jax.ShapeDtypeStruct((depth, mr, n), F32),
        compiler_params=_params("arbitrary", "arbitrary"),
        name="mods",
    )(cs, mod_w, mod_b.reshape(depth, 1, n))


def _proj_ln_kernel(a_ref, w_ref, res_ref, gate_ref, sh_ref, sc_ref, lg_ref, lb_ref, h_ref, u_ref, acc, *, alpha):
    tm = a_ref.shape[0]
    halves = 2 if tm % (2 * EPILOGUE_ROWS) == 0 else 1
    for hh in range(halves):
        rows = (hh * tm // halves, (hh + 1) * tm // halves)
        acc[rows[0]:rows[1], :] = jnp.dot(a_ref[rows[0]:rows[1], :], w_ref[...], preferred_element_type=F32)
        _deepnorm_epilogue(acc, res_ref, gate_ref, lg_ref, lb_ref, h_ref, alpha, u_ref, sh_ref, sc_ref, rows=rows)


def _proj_ln_call(a, w, widx, res, rows, mods5, layer, ln_g, ln_b, row_fn, tm, alpha, name):
    r, k = rows, a.shape[1]
    d = w.shape[2]
    vec = pl.BlockSpec((1, d), lambda i: (0, 0))
    return pl.pallas_call(
        functools.partial(_proj_ln_kernel, alpha=alpha),
        grid=(r // tm,),
        in_specs=[
            pl.BlockSpec((tm, k), lambda i: (i, 0)),
            pl.BlockSpec((None, k, d), lambda i: (widx, 0, 0), pipeline_mode=pl.Buffered(1)),
            pl.BlockSpec((tm, d), lambda i: (i, 0)),
            _mod_spec(layer, 2, row_fn)(d),
            _mod_spec(layer, 3, row_fn)(d),
            _mod_spec(layer, 4, row_fn)(d),
            vec,
            vec,
        ],
        out_specs=[pl.BlockSpec((tm, d), lambda i: (i, 0)), pl.BlockSpec((tm, d), lambda i: (i, 0))],
        out_shape=[jax.ShapeDtypeStruct((r, d), F32), jax.ShapeDtypeStruct((r, d), BF16)],
        scratch_shapes=[pltpu.VMEM((tm, d), F32)],
        compiler_params=_params("arbitrary"),
        name=name,
    )(a, w, res, mods5, mods5, mods5, ln_g.reshape(1, d), ln_b.reshape(1, d))


def _ffn_kernel(*refs, alpha, with_u, n_m, n_epi):
    if with_u:
        (u_ref, wg_ref, wu_ref, wd_ref, res_ref, gate_ref, sh_ref, sc_ref, lg_ref, lb_ref, *next_f32,
         h_ref, un_ref, ng_ref, nu_ref, nd_ref, acc, done) = refs
        for src, dst in zip(next_f32, (ng_ref, nu_ref, nd_ref)):
            dst[...] = src[...].astype(BF16)
    else:
        u_ref, wg_ref, wu_ref, wd_ref, res_ref, gate_ref, lg_ref, lb_ref, h_ref, acc, done = refs
        un_ref = sh_ref = sc_ref = None
    i, j = pl.program_id(0), pl.program_id(1)
    ch = res_ref.shape[0]

    def matmuls(first):
        u = u_ref[...]
        g = jnp.dot(u, wg_ref[...], preferred_element_type=F32)
        up = jnp.dot(u, wu_ref[...], preferred_element_type=F32)
        hid = (g * jax.nn.sigmoid(g) * up).astype(BF16)
        o = jnp.dot(hid, wd_ref[...], preferred_element_type=F32)
        if first:
            acc[...] = o
        else:
            acc[...] += o

    def epilogue():
        sub = min(ch, LN_ROWS)
        for k in range(ch // sub):
            rs = pl.ds(pl.multiple_of(j * ch + k * sub, sub), sub)
            ks = slice(k * sub, (k + 1) * sub)
            hn = _layer_norm(alpha * res_ref[ks, :] + gate_ref[...] * done[rs, :], lg_ref[...], lb_ref[...])
            h_ref[ks, :] = hn
            if with_u:
                un_ref[ks, :] = (hn * (1.0 + sc_ref[...]) + sh_ref[...]).astype(BF16)

    land = jnp.logical_and
    live = i < n_m
    epi = land(i > 0, j < n_epi)

    @pl.when(land(i > 0, j == 0))
    def _():
        done[...] = acc[...]

    @pl.when(land(land(live, epi), j == 0))
    def _():
        matmuls(True)
        epilogue()

    @pl.when(land(land(live, epi), j > 0))
    def _():
        matmuls(False)
        epilogue()

    @pl.when(land(land(live, jnp.logical_not(epi)), j == 0))
    def _():
        matmuls(True)

    @pl.when(land(land(live, jnp.logical_not(epi)), j > 0))
    def _():
        matmuls(False)

    @pl.when(land(jnp.logical_not(live), epi))
    def _():
        epilogue()


def _ffn_call(u, wts, res, rows, mods5, layer, next_wts, ln_g, ln_b, row_fn, tm, alpha):
    wg, wu, wd = wts
    r, d = rows, u.shape[1]
    hid = wg.shape[1]
    th = _pick(hid, (512, 256, 128))
    n_m, n_h = r // tm, hid // th
    ch = min(tm, EPILOGUE_ROWS)
    n_epi = tm // ch
    assert n_epi <= n_h
    with_u = next_wts is not None
    prev = lambda i: jnp.maximum(i - 1, 0)
    row2 = lambda i, j: row_fn(prev(i))
    jw = lambda i, j: jnp.where(i < n_m, j, n_h - 1)
    vec = pl.BlockSpec((1, d), lambda i, j: (0, 0))
    chunk = pl.BlockSpec((ch, d), lambda i, j: (prev(i) * n_epi + jnp.where(i == 0, 0, jnp.minimum(j, n_epi - 1)), 0))
    in_specs = [
        pl.BlockSpec((tm, d), lambda i, j: (jnp.minimum(i, n_m - 1), 0)),
        pl.BlockSpec((d, th), lambda i, j: (0, jw(i, j))),
        pl.BlockSpec((d, th), lambda i, j: (0, jw(i, j))),
        pl.BlockSpec((th, d), lambda i, j: (jw(i, j), 0)),
        chunk,
        _mod_spec(layer, 5, row2)(d),
    ]
    args = [u, wg, wu, wd, res, mods5]
    if with_u:
        in_specs += [_mod_spec(layer + 1, 0, row2)(d), _mod_spec(layer + 1, 1, row2)(d)]
        args += [mods5, mods5]
    in_specs += [vec, vec]
    args += [ln_g.reshape(1, d), ln_b.reshape(1, d)]
    out_specs = [chunk]
    out_shape = [jax.ShapeDtypeStruct((r, d), F32)]
    if with_u:
        out_specs.append(chunk)
        out_shape.append(jax.ShapeDtypeStruct((r, d), BF16))
        steps = (n_m + 1) * n_h
        for w32 in next_wts:
            _, rows_w, cols_w = w32.shape
            slab = _pick(rows_w, [s for s in (16, 32, 64, 128, 256, 512) if rows_w % s == 0 and rows_w // s <= steps])
            slab_of = lambda i, j, n=rows_w // slab: jnp.minimum(i * n_h + j, n - 1)
            in_specs.append(pl.BlockSpec((None, slab, cols_w), lambda i, j, f=slab_of: (layer + 1, f(i, j), 0)))
            args.append(w32)
            out_specs.append(pl.BlockSpec((slab, cols_w), lambda i, j, f=slab_of: (f(i, j), 0)))
            out_shape.append(jax.ShapeDtypeStruct((rows_w, cols_w), BF16))
    out = pl.pallas_call(
        functools.partial(_ffn_kernel, alpha=alpha, with_u=with_u, n_m=n_m, n_epi=n_epi),
        grid=(n_m + 1, n_h),
        in_specs=in_specs,
        out_specs=out_specs,
        out_shape=out_shape,
        scratch_shapes=[pltpu.VMEM((tm, d), F32), pltpu.VMEM((tm, d), F32)],
        compiler_params=_params("arbitrary", "arbitrary"),
        name="ffn",
    )(*args)
    return (out[0], out[1], tuple(out[2:])) if with_u else (out[0], None, None)


def _gelu_tanh(x):
    return x * (0.5 * (1.0 + jnp.tanh(0.7978845608028654 * (x + 0.044715 * (x * x * x)))))


def _col_groups(n, width=2 * V7X_MXU_WIDTH):
    return [(c, min(c + width, n)) for c in range(0, n, width)]


def _win_kernel(*refs, n_lat):
    if n_lat is None:
        u_ref, w_ref, y_ref, z_ref = refs
        u = u_ref[...]
    else:
        x_ref, c_ref, sh_ref, sc_ref, w_ref, y_ref, z_ref, h_ref, u_s = refs

        def emit(src_ref):
            v = src_ref[...]
            h_ref[...] = v
            u_s[...] = (v * (1.0 + sc_ref[...]) + sh_ref[...]).astype(BF16)

        pl.when(pl.program_id(0) < n_lat)(lambda: emit(x_ref))
        pl.when(pl.program_id(0) >= n_lat)(lambda: emit(c_ref))
        u = u_s[...]
    dr = y_ref.shape[1]
    for c0, c1 in _col_groups(dr):
        y_ref[:, c0:c1] = _gelu_tanh(jnp.dot(u, w_ref[:, c0:c1], preferred_element_type=F32)).astype(BF16)
    for c0, c1 in _col_groups(dr):
        z_ref[:, c0:c1] = jnp.dot(u, w_ref[:, dr + c0:dr + c1], preferred_element_type=F32)


def _win_call(u, w_in, widx, tm, first=None):
    n = w_in.shape[2]
    d, dr = w_in.shape[1], n // 2
    wspec = pl.BlockSpec((None, d, n), lambda i: (widx, 0, 0), pipeline_mode=pl.Buffered(1))
    tile = lambda width: pl.BlockSpec((tm, width), lambda i: (i, 0))
    if first is None:
        r, n_lat = u.shape[0], None
        in_specs, args = [tile(d), wspec], [u, w_in]
        extra_out, extra_shape, scratch = [], [], []
    else:
        x2, c2, mods5, row_fn = first
        n_lat = x2.shape[0] // tm
        r = x2.shape[0] + c2.shape[0]
        in_specs = [
            pl.BlockSpec((tm, d), lambda i: (jnp.minimum(i, n_lat - 1), 0)),
            pl.BlockSpec((tm, d), lambda i: (jnp.maximum(i - n_lat, 0), 0)),
            _mod_spec(0, 0, row_fn)(d),
            _mod_spec(0, 1, row_fn)(d),
            wspec,
        ]
        args = [x2, c2, mods5, mods5, w_in]
        extra_out, extra_shape = [tile(d)], [jax.ShapeDtypeStruct((r, d), F32)]
        scratch = [pltpu.VMEM((tm, d), BF16)]
    return pl.pallas_call(
        functools.partial(_win_kernel, n_lat=n_lat),
        grid=(r // tm,),
        in_specs=in_specs,
        out_specs=[tile(dr), tile(dr)] + extra_out,
        out_shape=[jax.ShapeDtypeStruct((r, dr), BF16), jax.ShapeDtypeStruct((r, dr), F32)] + extra_shape,
        scratch_shapes=scratch,
        compiler_params=_params("arbitrary"),
        name="rec_in",
    )(*args)


def _gate_window_plan(dr, bw):
    tn = V7X_MXU_WIDTH
    spans = []
    for n in range(dr // tn):
        c0 = n * tn
        lo = (c0 // bw) * bw
        hi = ((c0 + tn - 1) // bw + 1) * bw
        spans.append(((lo // V7X_LANES) * V7X_LANES, -(-hi // V7X_LANES) * V7X_LANES))
    kw = max(h - l for l, h in spans)
    return kw, tuple(min(l, dr - kw) for l, _ in spans)


def _gate_windows(w, kw, k0s):
    lead, bw = w.shape[:-3], w.shape[-1]
    tn = V7X_MXU_WIDTH
    w = w.astype(BF16)
    zero = jnp.zeros((), BF16)
    keep = [(0, 0, 0)] * (len(lead) + 1)
    tiles = []
    for n, k0 in enumerate(k0s):
        c0 = n * tn
        strips, filled = [], 0
        for h in range(c0 // bw, (c0 + tn - 1) // bw + 1):
            wr, wc = h * bw - k0, h * bw - c0
            sr0, sr1 = max(0, -wr), bw - max(0, wr + bw - kw)
            sc0, sc1 = max(0, -wc), bw - max(0, wc + bw - tn)
            if max(wr, 0) > filled:
                strips.append(jnp.zeros((*lead, max(wr, 0) - filled, tn), BF16))
            col = max(wc, 0)
            strips.append(lax.pad(w[..., h, sr0:sr1, sc0:sc1], zero, keep + [(col, tn - col - (sc1 - sc0), 0)]))
            filled = max(wr, 0) + sr1 - sr0
        if filled < kw:
            strips.append(jnp.zeros((*lead, kw - filled, tn), BF16))
        tiles.append(jnp.concatenate(strips, axis=-2))
    return jnp.stack(tiles, axis=-3)


def _gates_scan(zb, z_ref, wa_ref, wx_ref, ba_ref, bx_ref, lam_ref, a_s, b_s, dst, carry, kw, k0s, reverse):
    tt = zb.shape[0]
    tn = V7X_MXU_WIDTH
    sub = V7X_SUBLANES
    ng = tt // sub
    row = lax.broadcasted_iota(jnp.int32, (sub, tn), 0)
    entry = sub - 1 if reverse else 0
    for n, k0 in enumerate(k0s):
        cs = slice(n * tn, (n + 1) * tn)
        zw = zb[:, k0:k0 + kw]
        ta = jnp.tanh(jnp.dot(zw, wa_ref[n], preferred_element_type=F32) + ba_ref[:, cs])
        tx = jnp.tanh(jnp.dot(zw, wx_ref[n], preferred_element_type=F32) + bx_ref[:, cs])
        nl = -lam_ref[:, cs]
        half = (0.5 * RG_C) * (jnp.maximum(nl, 0.0) + jnp.log1p(jnp.exp(-jnp.abs(nl))))
        q = ta * half + half
        a = jnp.exp(-q)
        a_s[:, cs] = a
        x = jnp.tanh(q) * (1.0 + a * a)
        root = jnp.where(x > 0.0, x * lax.rsqrt(x), 0.0)
        b_s[:, cs] = root * (0.5 * tx + 0.5) * z_ref[:, cs]

        h = carry[:, cs]
        for g in (range(ng - 1, -1, -1) if reverse else range(ng)):
            rs = slice(g * sub, (g + 1) * sub)
            a8 = a_s[rs, cs]
            b8 = b_s[rs, cs]
            b8 = b8 + jnp.where(row == entry, a8 * h, 0.0)
            a8 = jnp.where(row == entry, 0.0, a8)
            for s in (1, 2, 4):
                shift = sub - s if reverse else s
                b8 = a8 * pltpu.roll(b8, shift, 0) + b8
                if s < 4:
                    a8 = a8 * pltpu.roll(a8, shift, 0)
            dst[rs, cs] = b8
            h = jnp.broadcast_to(b8[0:1, :] if reverse else b8[sub - 1:sub, :], (sub, tn))
        carry[:, cs] = h


def _scan_fwd_kernel(zc_ref, zp_ref, zn_ref, cw_ref, cb_ref, wa_ref, wx_ref, ba_ref, bx_ref, lam_ref,
                     hf_ref, z_ref, xpad, zb, a_s, b_s, carry, *, kw, k0s, ns):
    tt = zc_ref.shape[0]
    tn = V7X_MXU_WIDTH
    sub = V7X_SUBLANES
    t = pl.program_id(1)

    @pl.when(t == 0)
    def _():
        carry[...] = jnp.zeros_like(carry)

    first = t <= 1
    last = jnp.logical_or(t == 0, t == ns)
    xpad[0:sub, :] = jnp.where(first, 0.0, zp_ref[...])
    xpad[sub:sub + tt, :] = zc_ref[...]
    xpad[sub + tt:2 * sub + tt, :] = jnp.where(last, 0.0, zn_ref[...])
    for n in range(len(k0s)):
        cs = slice(n * tn, (n + 1) * tn)
        xa = xpad[:, cs]
        acc = None
        for j in range(CONV_W):
            shift = (CONV_LEFT - j) % xa.shape[0]
            xs = pltpu.roll(xa, shift, 0) if shift else xa
            term = xs[sub:sub + tt, :] * cw_ref[j:j + 1, cs]
            acc = term if acc is None else acc + term
        z = acc + cb_ref[:, cs]
        z_ref[:, cs] = z
        zb[:, cs] = z.astype(BF16)

    _gates_scan(zb, z_ref, wa_ref, wx_ref, ba_ref, bx_ref, lam_ref, a_s, b_s, hf_ref, carry, kw, k0s, reverse=False)


def _scan_bwd_kernel(z_ref, wa_ref, wx_ref, ba_ref, bx_ref, lam_ref, hf_ref, y_ref, o_ref, zb, a_s, b_s, h_s, carry,
                     *, kw, k0s):
    tn = V7X_MXU_WIDTH

    @pl.when(pl.program_id(1) == 0)
    def _():
        carry[...] = jnp.zeros_like(carry)

    zb[...] = z_ref[...].astype(BF16)
    _gates_scan(zb, z_ref, wa_ref, wx_ref, ba_ref, bx_ref, lam_ref, a_s, b_s, h_s, carry, kw, k0s, reverse=True)
    for n in range(len(k0s)):
        cs = slice(n * tn, (n + 1) * tn)
        o_ref[:, cs] = (y_ref[:, cs].astype(F32) * (hf_ref[:, cs] + h_s[:, cs])).astype(BF16)


def _scan_calls(zpre, y, conv_w, conv_b, wa, wx, ba, bx, lam, kw, k0s, batch, seq, ctx_len):
    r, dr = zpre.shape
    tt = ctx_len
    ns = seq // tt
    nx = batch * seq // tt
    g8 = tt // V7X_SUBLANES
    nblk8 = r // V7X_SUBLANES
    nt = len(k0s)

    def cur(reverse):
        def f(b, t):
            st = (ns - t) if reverse else (t - 1)
            return jnp.where(t == 0, nx + b, b * ns + st)
        return f

    def tile(reverse):
        return pl.BlockSpec((tt, dr), lambda b, t: (cur(reverse)(b, t), 0))

    fw = cur(False)
    halo_p = pl.BlockSpec((V7X_SUBLANES, dr), lambda b, t: (jnp.maximum(fw(b, t) * g8 - 1, 0), 0))
    halo_n = pl.BlockSpec((V7X_SUBLANES, dr), lambda b, t: (jnp.minimum(fw(b, t) * g8 + g8, nblk8 - 1), 0))
    vec = lambda d: pl.BlockSpec((None, 1, dr), lambda b, t: (d, 0, 0))
    wspec = lambda d: pl.BlockSpec((None, nt, kw, V7X_MXU_WIDTH), lambda b, t: (d, 0, 0, 0),
                                   pipeline_mode=pl.Buffered(1))
    ba, bx, lam = (v.reshape(2, 1, dr) for v in (ba, bx, lam))
    gate_specs = lambda d: [wspec(d), wspec(d), vec(d), vec(d), vec(d)]
    gate_args = [wa, wx, ba, bx, lam]
    work = [
        pltpu.VMEM((tt, dr), BF16),
        pltpu.VMEM((tt, dr), F32),
        pltpu.VMEM((tt, dr), F32),
    ]
    state = pltpu.VMEM((V7X_SUBLANES, dr), F32)
    hf, z = pl.pallas_call(
        functools.partial(_scan_fwd_kernel, kw=kw, k0s=k0s, ns=ns),
        grid=(batch, ns + 1),
        in_specs=[tile(False), halo_p, halo_n, pl.BlockSpec((CONV_W, dr), lambda b, t: (0, 0)),
                  pl.BlockSpec((1, dr), lambda b, t: (0, 0))] + gate_specs(0),
        out_specs=[tile(False), tile(False)],
        out_shape=[jax.ShapeDtypeStruct((r, dr), F32), jax.ShapeDtypeStruct((r, dr), F32)],
        scratch_shapes=[pltpu.VMEM((tt + 2 * V7X_SUBLANES, dr), F32)] + work + [state],
        compiler_params=_params("arbitrary", "arbitrary"),
        name="rec_scan_fwd",
    )(zpre, zpre, zpre, conv_w, conv_b.reshape(1, dr), *gate_args)
    return pl.pallas_call(
        functools.partial(_scan_bwd_kernel, kw=kw, k0s=k0s),
        grid=(batch, ns + 1),
        in_specs=[tile(True)] + gate_specs(1) + [tile(True), tile(True)],
        out_specs=tile(True),
        out_shape=jax.ShapeDtypeStruct((r, dr), BF16),
        scratch_shapes=work + [pltpu.VMEM((tt, dr), F32), state],
        compiler_params=_params("arbitrary", "arbitrary"),
        name="rec_scan_bwd",
    )(z, *gate_args, hf, y)


def _rope_tables(seq, pad_rows):
    rows = seq // GRID_W
    row = jnp.repeat(jnp.arange(rows, dtype=F32), GRID_W)
    col = jnp.tile(jnp.arange(GRID_W, dtype=F32), rows)
    axis_dim = HEAD_DIM // 2
    inv_freq = ROPE_BASE ** (-jnp.arange(0, axis_dim, 2, dtype=F32) / axis_dim)
    ang = jnp.concatenate([row[:, None] * inv_freq, col[:, None] * inv_freq], axis=-1)
    cos, sin = jnp.cos(ang), jnp.sin(ang)
    cos2 = jnp.concatenate([cos, cos], axis=-1)
    sin2 = jnp.concatenate([-sin, sin], axis=-1)
    cos2 = jnp.concatenate([cos2, jnp.ones((pad_rows, HEAD_DIM), F32)], axis=0)
    sin2 = jnp.concatenate([sin2, jnp.zeros((pad_rows, HEAD_DIM), F32)], axis=0)
    return cos2, sin2


def _qkv_kernel(u_ref, w_ref, cos_ref, sin_ref, o_ref, *, nq, nkv, scale):
    u = u_ref[...]
    c = cos_ref[...]
    s = sin_ref[...]
    for c0, c1 in _col_groups(o_ref.shape[1]):
        p = jnp.dot(u, w_ref[:, c0:c1], preferred_element_type=F32)
        for h0 in range(c0, c1, HEAD_DIM):
            t = p[:, h0 - c0:h0 - c0 + HEAD_DIM]
            if h0 < nq + nkv:
                t = t * c + pltpu.roll(t, HEAD_DIM // 2, 1) * s
            if h0 < nq:
                t = t * scale
            o_ref[:, h0:h0 + HEAD_DIM] = t.astype(BF16)


def _qkv_call(u, w, widx, cos2, sin2, seq, rows_x, tm):
    r, d = u.shape
    n = w.shape[2]
    nkv = N_KV_HEADS * HEAD_DIM
    n_seq_tiles = seq // tm
    tab = pl.BlockSpec((tm, HEAD_DIM), lambda i: (jnp.where(i < rows_x // tm, i % n_seq_tiles, n_seq_tiles), 0))
    return pl.pallas_call(
        functools.partial(_qkv_kernel, nq=n - 2 * nkv, nkv=nkv, scale=HEAD_DIM ** -0.5 * LOG2_E),
        grid=(r // tm,),
        in_specs=[pl.BlockSpec((tm, d), lambda i: (i, 0)),
                  pl.BlockSpec((None, d, n), lambda i: (widx, 0, 0), pipeline_mode=pl.Buffered(1)), tab, tab],
        out_specs=pl.BlockSpec((tm, n), lambda i: (i, 0)),
        out_shape=jax.ShapeDtypeStruct((r, n), BF16),
        compiler_params=_params("arbitrary"),
        name="att_qkv",
    )(u, w, cos2, sin2)


def _attn_masks(gq, ctx_len):
    blk = WINDOW
    qi = lax.broadcasted_iota(jnp.int32, (gq * blk, blk), 0) % blk
    ki = lax.broadcasted_iota(jnp.int32, (gq * blk, blk), 1)
    on = jnp.zeros((gq * blk, blk), F32)
    off = jnp.full((gq * blk, blk), NEG_INF, F32)
    prev = jnp.where(ki >= qi, 0.0, NEG_INF).astype(F32)
    nxt = jnp.where(ki <= qi, 0.0, NEG_INF).astype(F32)
    ctx_on = jnp.zeros((gq * blk, ctx_len), F32)
    variants = ((off, on, nxt), (prev, on, nxt), (prev, on, off), (off, off, off))
    return jnp.stack([jnp.concatenate(v + (ctx_on,), axis=1) for v in variants])


def _attn_kernel(sink_ref, q_ref, kp_ref, kc_ref, kn_ref, vp_ref, vc_ref, vn_ref, kx_ref, vx_ref, bias, o_ref,
                 k_all, v_all, *, g):
    blk = WINDOW
    band = 3 * blk
    for dst, srcs in ((k_all, (kp_ref, kc_ref, kn_ref)), (v_all, (vp_ref, vc_ref, vn_ref))):
        for o, src in enumerate(srcs):
            dst[o * blk:(o + 1) * blk, :] = src[...]
    k_all[band:, :] = kx_ref[...]
    v_all[band:, :] = vx_ref[...]
    nt_dims = (((1,), (1,)), ((), ()))
    gq = bias.shape[0] // blk
    for h0 in range(0, N_KV_HEADS * g, gq):
        kh = h0 // g
        ks = slice(kh * HEAD_DIM, (kh + 1) * HEAD_DIM)
        heads = range(h0, h0 + gq)
        qs = jnp.concatenate([q_ref[:, h * HEAD_DIM:(h + 1) * HEAD_DIM] for h in heads], axis=0)
        sk = jnp.concatenate([jnp.full((blk, 1), sink_ref[h] * LOG2_E, F32) for h in heads], axis=0)
        s = lax.dot_general(qs, k_all[:, ks], nt_dims, preferred_element_type=F32) + bias[...]
        m = jnp.maximum(jnp.max(s, -1, keepdims=True), sk)
        p = jnp.exp2(s - m)
        denom = jnp.exp2(sk - m) + jnp.sum(p, -1, keepdims=True)
        o = jnp.dot(p.astype(BF16), v_all[:, ks], preferred_element_type=F32) / denom
        for gi, h in enumerate(heads):
            o_ref[:, h * HEAD_DIM:(h + 1) * HEAD_DIM] = o[gi * blk:(gi + 1) * blk].astype(BF16)


def _attn_call(qkv, sink, batch, seq, ctx_len, ctx_queries):
    r, n = qkv.shape
    nkv = N_KV_HEADS * HEAD_DIM
    d = n - 2 * nkv
    g = d // HEAD_DIM // N_KV_HEADS
    gq = g
    blk = WINDOW
    nb = seq // blk
    ncb = ctx_len // blk
    kcol = d // nkv
    x0 = batch * seq

    def qrow(b, j):
        return jnp.where(j < nb, b * nb + j, x0 // blk + b * ncb + (j - nb))

    def krow(b, j, off):
        return b * nb + jnp.clip(jnp.minimum(j, nb - 1) + off, 0, nb - 1)

    assert nb >= 2
    qspec = pl.BlockSpec((blk, d), lambda b, j: (qrow(b, j), 0))
    band = lambda off, col: pl.BlockSpec((blk, nkv), lambda b, j: (krow(b, j, off), col))
    ctxs = lambda col: pl.BlockSpec((ctx_len, nkv), lambda b, j: (x0 // ctx_len + b, col))
    nkeys = 3 * blk + ctx_len
    mask = pl.BlockSpec((None, gq * blk, nkeys),
                        lambda b, j: (jnp.where(j >= nb, 3, jnp.where(j == 0, 0, jnp.where(j == nb - 1, 2, 1))), 0, 0))
    return pl.pallas_call(
        functools.partial(_attn_kernel, g=g),
        grid=(batch, nb + ncb if ctx_queries else nb),
        in_specs=[
            pl.BlockSpec(memory_space=pltpu.SMEM),
            qspec,
            band(-1, kcol), band(0, kcol), band(1, kcol),
            band(-1, kcol + 1), band(0, kcol + 1), band(1, kcol + 1),
            ctxs(kcol), ctxs(kcol + 1),
            mask,
        ],
        out_specs=qspec,
        out_shape=jax.ShapeDtypeStruct((r if ctx_queries else x0, d), BF16),
        scratch_shapes=[
            pltpu.VMEM((nkeys, nkv), BF16),
            pltpu.VMEM((nkeys, nkv), BF16),
        ],
        compiler_params=_params("arbitrary", "arbitrary"),
        name="att_core",
    )(sink, qkv, qkv, qkv, qkv, qkv, qkv, qkv, qkv, qkv, _attn_masks(gq, ctx_len))


def kernel(x, c, ctx, c_ctx, mod_w, mod_b, ln_mix_g, ln_mix_b, ln_ffn_g, ln_ffn_b, ffn_w_gate, ffn_w_up, ffn_w_down,
           rec_w_in, rec_conv_w, rec_conv_b, rec_gate_a_w, rec_gate_a_b, rec_gate_x_w, rec_gate_x_b, rec_lambda,
           rec_w_out, att_w_qkv, att_sink, att_w_o):
    batch, seq, d = x.shape
    ctx_len = ctx.shape[1]
    depth = mod_w.shape[0]
    rows_x, rows_c = batch * seq, batch * ctx_len
    assert seq % ctx_len == 0 and ctx_len % WINDOW == 0 and seq % GRID_W == 0
    assert d % (N_KV_HEADS * HEAD_DIM) == 0
    alpha = (2.0 * depth) ** 0.25

    tm = _pick(seq, (512, 256, 128))
    while rows_c % tm:
        tm //= 2
    tm_big = 2 * tm if (seq % (2 * tm) == 0 and rows_c % (2 * tm) == 0) else tm

    def row_fn_for(t):
        return lambda i: jnp.minimum((i * t) // seq, batch)

    row_fn, row_fn_big = row_fn_for(tm), row_fn_for(tm_big)

    mr = -(-(batch + 1) // V7X_SUBLANES) * V7X_SUBLANES
    cs = jnp.zeros((mr, d), F32).at[:batch].set(c).at[batch].set(c_ctx)
    mods5 = _mods_call(cs, mod_w, mod_b).reshape(depth, mr, N_MOD, 1, d)

    cos2, sin2 = _rope_tables(seq, tm_big)
    dr = rec_w_out.shape[1]
    kw, k0s = _gate_window_plan(dr, dr // RNN_BLOCKS)
    wa_all = _gate_windows(0.5 * rec_gate_a_w, kw, k0s)
    wx_all = _gate_windows(0.5 * rec_gate_x_w, kw, k0s)
    ba_all, bx_all = 0.5 * rec_gate_a_b, 0.5 * rec_gate_x_b

    ffn_f32 = (ffn_w_gate, ffn_w_up, ffn_w_down)
    ffn_wts = tuple(w[0].astype(BF16) for w in ffn_f32)
    w_in, w_out = rec_w_in.astype(BF16), rec_w_out.astype(BF16)
    w_qkv, w_o = att_w_qkv.astype(BF16), att_w_o.astype(BF16)

    h = u = None
    for i in range(depth):
        j = i // N_MIXERS
        last = i == depth - 1
        rows = rows_x if last else rows_x + rows_c
        if i % N_MIXERS == 0:
            if i == 0:
                tm0 = max(tm // 2, WINDOW)
                first = (x.reshape(rows_x, d), ctx.reshape(rows_c, d), mods5, row_fn_for(tm0))
                y, zpre, h = _win_call(None, w_in, j, tm0, first)
            else:
                y, zpre = _win_call(u, w_in, j, tm)
            yh = _scan_calls(zpre, y, rec_conv_w[j], rec_conv_b[j], wa_all[j], wx_all[j], ba_all[j], bx_all[j],
                             rec_lambda[j], kw, k0s, batch, seq, ctx_len)
            h, u = _proj_ln_call(yh, w_out, j, h, rows, mods5, i, ln_mix_g[i], ln_mix_b[i], row_fn, tm, alpha, "rec_out")
        else:
            qkv = _qkv_call(u, w_qkv, j, cos2, sin2, seq, rows_x, tm_big)
            ao = _attn_call(qkv, att_sink[j], batch, seq, ctx_len, ctx_queries=not last)
            h, u = _proj_ln_call(ao, w_o, j, h, rows, mods5, i, ln_mix_g[i], ln_mix_b[i], row_fn, tm, alpha, "att_out")
        next_wts = None if last else ffn_f32
        h, u, ffn_wts = _ffn_call(u, ffn_wts, h, rows, mods5, i, next_wts, ln_ffn_g[i], ln_ffn_b[i], row_fn_big, tm_big,
                                  alpha)
    return h.reshape(batch, seq, d)
```

```python
import functools

import jax
import jax.numpy as jnp
from jax import lax
from jax.experimental import pallas as pl
from jax.experimental.pallas import tpu as pltpu

HEAD_DIM = 128
N_KV_HEADS = 4
WINDOW = 128
GRID_W = 64
ROPE_BASE = 10000.0
RNN_BLOCKS = 16
CONV_W = 4
CONV_LEFT = 2
RG_C = 8.0
LN_EPS = 1e-5
NEG_INF = -1e30
N_MIXERS = 2
N_MOD = 6
LOG2_E = 1.4426950408889634

V7X_LANES = 128
V7X_SUBLANES = 8
V7X_MXU_WIDTH = 256
V7X_VMEM_BYTES = 64 * 1024 * 1024
VMEM_LIMIT_BYTES = V7X_VMEM_BYTES - 8 * 1024 * 1024

F32 = jnp.float32
BF16 = jnp.bfloat16
EPILOGUE_ROWS = 128
LN_ROWS = 128


def _pick(n, cands):
    for c in cands:
        if n % c == 0:
            return c
    raise ValueError(f"no tile in {cands} divides {n}")


def _params(*sem):
    return pltpu.CompilerParams(dimension_semantics=sem, vmem_limit_bytes=VMEM_LIMIT_BYTES)


def _mod_spec(layer, chunk, row_fn):
    return lambda d: pl.BlockSpec((None, None, None, 1, d), lambda *g: (layer, row_fn(*g), chunk, 0, 0))


def _cast_side_job(items, steps, step_of):
    in_specs, args, out_specs, out_shape = [], [], [], []
    for stack, layer in items:
        _, rows_w, cols_w = stack.shape
        slab = _pick(rows_w, [s for s in (16, 32, 64, 128, 256, 512, 1024) if rows_w // s <= steps] + [rows_w])
        slab_of = lambda *g, n=rows_w // slab: jnp.minimum(step_of(*g), n - 1)
        in_specs.append(pl.BlockSpec((None, slab, cols_w), lambda *g, f=slab_of, l=layer: (l, f(*g), 0)))
        args.append(stack)
        out_specs.append(pl.BlockSpec((slab, cols_w), lambda *g, f=slab_of: (f(*g), 0)))
        out_shape.append(jax.ShapeDtypeStruct((rows_w, cols_w), BF16))
    return in_specs, args, out_specs, out_shape


def _layer_norm(v, g, b):
    mu = jnp.mean(v, axis=-1, keepdims=True)
    d = v - mu
    var = jnp.mean(d * d, axis=-1, keepdims=True)
    return d * lax.rsqrt(var + LN_EPS) * g + b


def _deepnorm_epilogue(acc_ref, res_ref, gate_ref, lg_ref, lb_ref, h_ref, alpha, u_ref=None, sh_ref=None, sc_ref=None,
                       rows=None):
    r0, r1 = rows if rows is not None else (0, acc_ref.shape[0])
    ch = min(r1 - r0, LN_ROWS)
    gate, lg, lb = gate_ref[...], lg_ref[...], lb_ref[...]
    if u_ref is not None:
        sh, sc1 = sh_ref[...], 1.0 + sc_ref[...]
    for k in range((r1 - r0) // ch):
        rs = slice(r0 + k * ch, r0 + (k + 1) * ch)
        hn = _layer_norm(alpha * res_ref[rs, :] + gate * acc_ref[rs, :], lg, lb)
        h_ref[rs, :] = hn
        if u_ref is not None:
            u_ref[rs, :] = (hn * sc1 + sh).astype(BF16)


def _mods_kernel(cs_ref, w_ref, b_ref, o_ref):
    s = cs_ref[...]
    s = (s * jax.nn.sigmoid(s)).astype(BF16)
    o_ref[...] = jnp.dot(s, w_ref[...].astype(BF16), preferred_element_type=F32) + b_ref[...]


def _mods_call(cs, mod_w, mod_b):
    depth, d, n = mod_w.shape
    mr = cs.shape[0]
    tn = _pick(n, (1024, 512, 256, 128))
    return pl.pallas_call(
        _mods_kernel,
        grid=(depth, n // tn),
        in_specs=[
            pl.BlockSpec((mr, d), lambda l, j: (0, 0)),
            pl.BlockSpec((None, d, tn), lambda l, j: (l, 0, j)),
            pl.BlockSpec((None, 1, tn), lambda l, j: (l, 0, j)),
        ],
        out_specs=pl.BlockSpec((None, mr, tn), lambda l, j: (l, 0, j)),
        out_shape=jax.ShapeDtypeStruct((depth, mr, n), F32),
        compiler_params=_params("arbitrary", "arbitrary"),
        name="mods",
    )(cs, mod_w, mod_b.reshape(depth, 1, n))


def _proj_ln_kernel(a_ref, w_ref, res_ref, gate_ref, sh_ref, sc_ref, lg_ref, lb_ref, h_ref, u_ref, acc, *, alpha):
    tm = a_ref.shape[0]
    halves = 2 if tm % (2 * EPILOGUE_ROWS) == 0 else 1
    for hh in range(halves):
        rows = (hh * tm // halves, (hh + 1) * tm // halves)
        acc[rows[0]:rows[1], :] = jnp.dot(a_ref[rows[0]:rows[1], :], w_ref[...], preferred_element_type=F32)
        _deepnorm_epilogue(acc, res_ref, gate_ref, lg_ref, lb_ref, h_ref, alpha, u_ref, sh_ref, sc_ref, rows=rows)


def _proj_ln_call(a, w, res, rows, mods5, layer, ln_g, ln_b, row_fn, tm, alpha, name):
    r, k = rows, a.shape[1]
    d = w.shape[1]
    vec = pl.BlockSpec((1, d), lambda i: (0, 0))
    return pl.pallas_call(
        functools.partial(_proj_ln_kernel, alpha=alpha),
        grid=(r // tm,),
        in_specs=[
            pl.BlockSpec((tm, k), lambda i: (i, 0)),
            pl.BlockSpec((k, d), lambda i: (0, 0), pipeline_mode=pl.Buffered(1)),
            pl.BlockSpec((tm, d), lambda i: (i, 0)),
            _mod_spec(layer, 2, row_fn)(d),
            _mod_spec(layer, 3, row_fn)(d),
            _mod_spec(layer, 4, row_fn)(d),
            vec,
            vec,
        ],
        out_specs=[pl.BlockSpec((tm, d), lambda i: (i, 0)), pl.BlockSpec((tm, d), lambda i: (i, 0))],
        out_shape=[jax.ShapeDtypeStruct((r, d), F32), jax.ShapeDtypeStruct((r, d), BF16)],
        scratch_shapes=[pltpu.VMEM((tm, d), F32)],
        compiler_params=_params("arbitrary"),
        name=name,
    )(a, w, res, mods5, mods5, mods5, ln_g.reshape(1, d), ln_b.reshape(1, d))


def _ffn_kernel(*refs, alpha, with_u, n_m, n_epi, n_cast):
    if with_u:
        u_ref, wg_ref, wu_ref, wd_ref, res_ref, gate_ref, sh_ref, sc_ref, lg_ref, lb_ref = refs[:10]
        cast_in, (h_ref, un_ref) = refs[10:10 + n_cast], refs[10 + n_cast:12 + n_cast]
        cast_out, (acc, done) = refs[12 + n_cast:12 + 2 * n_cast], refs[12 + 2 * n_cast:]
        for src, dst in zip(cast_in, cast_out):
            dst[...] = src[...].astype(BF16)
    else:
        u_ref, wg_ref, wu_ref, wd_ref, res_ref, gate_ref, lg_ref, lb_ref, h_ref, acc, done = refs
        un_ref = sh_ref = sc_ref = None
    i, j = pl.program_id(0), pl.program_id(1)
    ch = res_ref.shape[0]

    def matmuls(first):
        u = u_ref[...]
        g = jnp.dot(u, wg_ref[...], preferred_element_type=F32)
        up = jnp.dot(u, wu_ref[...], preferred_element_type=F32)
        hid = (g * jax.nn.sigmoid(g) * up).astype(BF16)
        o = jnp.dot(hid, wd_ref[...], preferred_element_type=F32)
        if first:
            acc[...] = o
        else:
            acc[...] += o

    def epilogue():
        sub = min(ch, LN_ROWS)
        for k in range(ch // sub):
            rs = pl.ds(pl.multiple_of(j * ch + k * sub, sub), sub)
            ks = slice(k * sub, (k + 1) * sub)
            hn = _layer_norm(alpha * res_ref[ks, :] + gate_ref[...] * done[rs, :], lg_ref[...], lb_ref[...])
            h_ref[ks, :] = hn
            if with_u:
                un_ref[ks, :] = (hn * (1.0 + sc_ref[...]) + sh_ref[...]).astype(BF16)

    land = jnp.logical_and
    live = i < n_m
    epi = land(i > 0, j < n_epi)

    @pl.when(land(i > 0, j == 0))
    def _():
        done[...] = acc[...]

    @pl.when(land(land(live, epi), j == 0))
    def _():
        matmuls(True)
        epilogue()

    @pl.when(land(land(live, epi), j > 0))
    def _():
        matmuls(False)
        epilogue()

    @pl.when(land(land(live, jnp.logical_not(epi)), j == 0))
    def _():
        matmuls(True)

    @pl.when(land(land(live, jnp.logical_not(epi)), j > 0))
    def _():
        matmuls(False)

    @pl.when(land(jnp.logical_not(live), epi))
    def _():
        epilogue()


def _ffn_call(u, wts, res, rows, mods5, layer, next_wts, ln_g, ln_b, row_fn, tm, alpha):
    wg, wu, wd = wts
    r, d = rows, u.shape[1]
    hid = wg.shape[1]
    th = _pick(hid, (512, 256, 128))
    n_m, n_h = r // tm, hid // th
    ch = min(tm, EPILOGUE_ROWS)
    n_epi = tm // ch
    assert n_epi <= n_h
    with_u = next_wts is not None
    prev = lambda i: jnp.maximum(i - 1, 0)
    row2 = lambda i, j: row_fn(prev(i))
    jw = lambda i, j: jnp.where(i < n_m, j, n_h - 1)
    vec = pl.BlockSpec((1, d), lambda i, j: (0, 0))
    chunk = pl.BlockSpec((ch, d), lambda i, j: (prev(i) * n_epi + jnp.where(i == 0, 0, jnp.minimum(j, n_epi - 1)), 0))
    in_specs = [
        pl.BlockSpec((tm, d), lambda i, j: (jnp.minimum(i, n_m - 1), 0)),
        pl.BlockSpec((d, th), lambda i, j: (0, jw(i, j))),
        pl.BlockSpec((d, th), lambda i, j: (0, jw(i, j))),
        pl.BlockSpec((th, d), lambda i, j: (jw(i, j), 0)),
        chunk,
        _mod_spec(layer, 5, row2)(d),
    ]
    args = [u, wg, wu, wd, res, mods5]
    if with_u:
        in_specs += [_mod_spec(layer + 1, 0, row2)(d), _mod_spec(layer + 1, 1, row2)(d)]
        args += [mods5, mods5]
    in_specs += [vec, vec]
    args += [ln_g.reshape(1, d), ln_b.reshape(1, d)]
    out_specs = [chunk]
    out_shape = [jax.ShapeDtypeStruct((r, d), F32)]
    if with_u:
        out_specs.append(chunk)
        out_shape.append(jax.ShapeDtypeStruct((r, d), BF16))
        c_in, c_args, c_out, c_shape = _cast_side_job(next_wts, (n_m + 1) * n_h, lambda i, j: i * n_h + j)
        in_specs += c_in
        args += c_args
        out_specs += c_out
        out_shape += c_shape
    n_cast = len(next_wts) if with_u else 0
    out = pl.pallas_call(
        functools.partial(_ffn_kernel, alpha=alpha, with_u=with_u, n_m=n_m, n_epi=n_epi, n_cast=n_cast),
        grid=(n_m + 1, n_h),
        in_specs=in_specs,
        out_specs=out_specs,
        out_shape=out_shape,
        scratch_shapes=[pltpu.VMEM((tm, d), F32), pltpu.VMEM((tm, d), F32)],
        compiler_params=_params("arbitrary", "arbitrary"),
        name="ffn",
    )(*args)
    return (out[0], out[1], tuple(out[2:])) if with_u else (out[0], None, None)


def _gelu_tanh(x):
    return x * (0.5 * (1.0 + jnp.tanh(0.7978845608028654 * (x + 0.044715 * (x * x * x)))))


def _col_groups(n, width=2 * V7X_MXU_WIDTH):
    return [(c, min(c + width, n)) for c in range(0, n, width)]


def _win_kernel(*refs, n_lat, n_cast):
    if n_lat is None:
        u_ref, w_ref, y_ref, z_ref = refs
        u = u_ref[...]
    else:
        x_ref, c_ref, sh_ref, sc_ref, w_ref = refs[:5]
        cast_in, (y_ref, z_ref, h_ref) = refs[5:5 + n_cast], refs[5 + n_cast:8 + n_cast]
        cast_out, u_s = refs[8 + n_cast:8 + 2 * n_cast], refs[8 + 2 * n_cast]
        for src, dst in zip(cast_in, cast_out):
            dst[...] = src[...].astype(BF16)

        def emit(src_ref):
            v = src_ref[...]
            h_ref[...] = v
            u_s[...] = (v * (1.0 + sc_ref[...]) + sh_ref[...]).astype(BF16)

        pl.when(pl.program_id(0) < n_lat)(lambda: emit(x_ref))
        pl.when(pl.program_id(0) >= n_lat)(lambda: emit(c_ref))
        u = u_s[...]
    dr = y_ref.shape[1]
    for c0, c1 in _col_groups(dr):
        y_ref[:, c0:c1] = _gelu_tanh(jnp.dot(u, w_ref[:, c0:c1], preferred_element_type=F32)).astype(BF16)
    for c0, c1 in _col_groups(dr):
        z_ref[:, c0:c1] = jnp.dot(u, w_ref[:, dr + c0:dr + c1], preferred_element_type=F32)


def _win_call(u, w_in, tm, first=None):
    d, n = w_in.shape
    dr = n // 2
    wspec = pl.BlockSpec((d, n), lambda i: (0, 0), pipeline_mode=pl.Buffered(1))
    tile = lambda width: pl.BlockSpec((tm, width), lambda i: (i, 0))
    if first is None:
        r, n_lat, n_cast = u.shape[0], None, 0
        in_specs, args = [tile(d), wspec], [u, w_in]
        extra_out, extra_shape, scratch = [], [], []
    else:
        x2, c2, mods5, row_fn, cast_items = first
        n_lat, n_cast = x2.shape[0] // tm, len(cast_items)
        r = x2.shape[0] + c2.shape[0]
        c_in, c_args, c_out, c_shape = _cast_side_job(cast_items, r // tm, lambda i: i)
        in_specs = [
            pl.BlockSpec((tm, d), lambda i: (jnp.minimum(i, n_lat - 1), 0)),
            pl.BlockSpec((tm, d), lambda i: (jnp.maximum(i - n_lat, 0), 0)),
            _mod_spec(0, 0, row_fn)(d),
            _mod_spec(0, 1, row_fn)(d),
            wspec,
        ] + c_in
        args = [x2, c2, mods5, mods5, w_in] + c_args
        extra_out, extra_shape = [tile(d)] + c_out, [jax.ShapeDtypeStruct((r, d), F32)] + c_shape
        scratch = [pltpu.VMEM((tm, d), BF16)]
    out = pl.pallas_call(
        functools.partial(_win_kernel, n_lat=n_lat, n_cast=n_cast),
        grid=(r // tm,),
        in_specs=in_specs,
        out_specs=[tile(dr), tile(dr)] + extra_out,
        out_shape=[jax.ShapeDtypeStruct((r, dr), BF16), jax.ShapeDtypeStruct((r, dr), F32)] + extra_shape,
        scratch_shapes=scratch,
        compiler_params=_params("arbitrary"),
        name="rec_in",
    )(*args)
    return out if first is None else (out[0], out[1], out[2], tuple(out[3:]))


def _gate_window_plan(dr, bw):
    tn = V7X_MXU_WIDTH
    spans = []
    for n in range(dr // tn):
        c0 = n * tn
        lo = (c0 // bw) * bw
        hi = ((c0 + tn - 1) // bw + 1) * bw
        spans.append(((lo // V7X_LANES) * V7X_LANES, -(-hi // V7X_LANES) * V7X_LANES))
    kw = max(h - l for l, h in spans)
    return kw, tuple(min(l, dr - kw) for l, _ in spans)


def _gate_windows(w, kw, k0s):
    lead, bw = w.shape[:-3], w.shape[-1]
    tn = V7X_MXU_WIDTH
    w = w.astype(BF16)
    zero = jnp.zeros((), BF16)
    keep = [(0, 0, 0)] * (len(lead) + 1)
    tiles = []
    for n, k0 in enumerate(k0s):
        c0 = n * tn
        strips, filled = [], 0
        for h in range(c0 // bw, (c0 + tn - 1) // bw + 1):
            wr, wc = h * bw - k0, h * bw - c0
            sr0, sr1 = max(0, -wr), bw - max(0, wr + bw - kw)
            sc0, sc1 = max(0, -wc), bw - max(0, wc + bw - tn)
            if max(wr, 0) > filled:
                strips.append(jnp.zeros((*lead, max(wr, 0) - filled, tn), BF16))
            col = max(wc, 0)
            strips.append(lax.pad(w[..., h, sr0:sr1, sc0:sc1], zero, keep + [(col, tn - col - (sc1 - sc0), 0)]))
            filled = max(wr, 0) + sr1 - sr0
        if filled < kw:
            strips.append(jnp.zeros((*lead, kw - filled, tn), BF16))
        tiles.append(jnp.concatenate(strips, axis=-2))
    return jnp.stack(tiles, axis=-3)


def _gates_scan(zb, z_ref, wa_ref, wx_ref, ba_ref, bx_ref, lam_ref, a_s, b_s, dst, carry, kw, k0s, reverse):
    tt = zb.shape[0]
    tn = V7X_MXU_WIDTH
    sub = V7X_SUBLANES
    ng = tt // sub
    row = lax.broadcasted_iota(jnp.int32, (sub, tn), 0)
    entry = sub - 1 if reverse else 0
    for n, k0 in enumerate(k0s):
        cs = slice(n * tn, (n + 1) * tn)
        zw = zb[:, k0:k0 + kw]
        ta = jnp.tanh(jnp.dot(zw, wa_ref[n], preferred_element_type=F32) + ba_ref[:, cs])
        tx = jnp.tanh(jnp.dot(zw, wx_ref[n], preferred_element_type=F32) + bx_ref[:, cs])
        nl = -lam_ref[:, cs]
        half = (0.5 * RG_C) * (jnp.maximum(nl, 0.0) + jnp.log1p(jnp.exp(-jnp.abs(nl))))
        q = ta * half + half
        a = jnp.exp(-q)
        a_s[:, cs] = a
        x = jnp.tanh(q) * (1.0 + a * a)
        root = jnp.where(x > 0.0, x * lax.rsqrt(x), 0.0)
        b_s[:, cs] = root * (0.5 * tx + 0.5) * z_ref[:, cs]

        h = carry[:, cs]
        for g in (range(ng - 1, -1, -1) if reverse else range(ng)):
            rs = slice(g * sub, (g + 1) * sub)
            a8 = a_s[rs, cs]
            b8 = b_s[rs, cs]
            b8 = b8 + jnp.where(row == entry, a8 * h, 0.0)
            a8 = jnp.where(row == entry, 0.0, a8)
            for s in (1, 2, 4):
                shift = sub - s if reverse else s
                b8 = a8 * pltpu.roll(b8, shift, 0) + b8
                if s < 4:
                    a8 = a8 * pltpu.roll(a8, shift, 0)
            dst[rs, cs] = b8
            h = jnp.broadcast_to(b8[0:1, :] if reverse else b8[sub - 1:sub, :], (sub, tn))
        carry[:, cs] = h


def _scan_fwd_kernel(zc_ref, zp_ref, zn_ref, cw_ref, cb_ref, wa_ref, wx_ref, ba_ref, bx_ref, lam_ref,
                     hf_ref, z_ref, xpad, zb, a_s, b_s, carry, *, kw, k0s, ns):
    tt = zc_ref.shape[0]
    tn = V7X_MXU_WIDTH
    sub = V7X_SUBLANES
    t = pl.program_id(1)

    @pl.when(t == 0)
    def _():
        carry[...] = jnp.zeros_like(carry)

    first = t <= 1
    last = jnp.logical_or(t == 0, t == ns)
    xpad[0:sub, :] = jnp.where(first, 0.0, zp_ref[...])
    xpad[sub:sub + tt, :] = zc_ref[...]
    xpad[sub + tt:2 * sub + tt, :] = jnp.where(last, 0.0, zn_ref[...])
    for n in range(len(k0s)):
        cs = slice(n * tn, (n + 1) * tn)
        xa = xpad[:, cs]
        acc = None
        for j in range(CONV_W):
            shift = (CONV_LEFT - j) % xa.shape[0]
            xs = pltpu.roll(xa, shift, 0) if shift else xa
            term = xs[sub:sub + tt, :] * cw_ref[j:j + 1, cs]
            acc = term if acc is None else acc + term
        z = acc + cb_ref[:, cs]
        z_ref[:, cs] = z
        zb[:, cs] = z.astype(BF16)

    _gates_scan(zb, z_ref, wa_ref, wx_ref, ba_ref, bx_ref, lam_ref, a_s, b_s, hf_ref, carry, kw, k0s, reverse=False)


def _scan_bwd_kernel(z_ref, wa_ref, wx_ref, ba_ref, bx_ref, lam_ref, hf_ref, y_ref, o_ref, zb, a_s, b_s, h_s, carry,
                     *, kw, k0s):
    tn = V7X_MXU_WIDTH

    @pl.when(pl.program_id(1) == 0)
    def _():
        carry[...] = jnp.zeros_like(carry)

    zb[...] = z_ref[...].astype(BF16)
    _gates_scan(zb, z_ref, wa_ref, wx_ref, ba_ref, bx_ref, lam_ref, a_s, b_s, h_s, carry, kw, k0s, reverse=True)
    for n in range(len(k0s)):
        cs = slice(n * tn, (n + 1) * tn)
        o_ref[:, cs] = (y_ref[:, cs].astype(F32) * (hf_ref[:, cs] + h_s[:, cs])).astype(BF16)


def _scan_calls(zpre, y, conv_w, conv_b, wa, wx, ba, bx, lam, kw, k0s, batch, seq, ctx_len):
    r, dr = zpre.shape
    tt = ctx_len
    ns = seq // tt
    nx = batch * seq // tt
    g8 = tt // V7X_SUBLANES
    nblk8 = r // V7X_SUBLANES
    nt = len(k0s)

    def cur(reverse):
        def f(b, t):
            st = (ns - t) if reverse else (t - 1)
            return jnp.where(t == 0, nx + b, b * ns + st)
        return f

    def tile(reverse):
        return pl.BlockSpec((tt, dr), lambda b, t: (cur(reverse)(b, t), 0))

    fw = cur(False)
    halo_p = pl.BlockSpec((V7X_SUBLANES, dr), lambda b, t: (jnp.maximum(fw(b, t) * g8 - 1, 0), 0))
    halo_n = pl.BlockSpec((V7X_SUBLANES, dr), lambda b, t: (jnp.minimum(fw(b, t) * g8 + g8, nblk8 - 1), 0))
    vec = lambda d: pl.BlockSpec((None, 1, dr), lambda b, t: (d, 0, 0))
    wspec = lambda d: pl.BlockSpec((None, nt, kw, V7X_MXU_WIDTH), lambda b, t: (d, 0, 0, 0),
                                   pipeline_mode=pl.Buffered(1))
    ba, bx, lam = (v.reshape(2, 1, dr) for v in (ba, bx, lam))
    gate_specs = lambda d: [wspec(d), wspec(d), vec(d), vec(d), vec(d)]
    gate_args = [wa, wx, ba, bx, lam]
    work = [
        pltpu.VMEM((tt, dr), BF16),
        pltpu.VMEM((tt, dr), F32),
        pltpu.VMEM((tt, dr), F32),
    ]
    state = pltpu.VMEM((V7X_SUBLANES, dr), F32)
    hf, z = pl.pallas_call(
        functools.partial(_scan_fwd_kernel, kw=kw, k0s=k0s, ns=ns),
        grid=(batch, ns + 1),
        in_specs=[tile(False), halo_p, halo_n, pl.BlockSpec((CONV_W, dr), lambda b, t: (0, 0)),
                  pl.BlockSpec((1, dr), lambda b, t: (0, 0))] + gate_specs(0),
        out_specs=[tile(False), tile(False)],
        out_shape=[jax.ShapeDtypeStruct((r, dr), F32), jax.ShapeDtypeStruct((r, dr), F32)],
        scratch_shapes=[pltpu.VMEM((tt + 2 * V7X_SUBLANES, dr), F32)] + work + [state],
        compiler_params=_params("arbitrary", "arbitrary"),
        name="rec_scan_fwd",
    )(zpre, zpre, zpre, conv_w, conv_b.reshape(1, dr), *gate_args)
    return pl.pallas_call(
        functools.partial(_scan_bwd_kernel, kw=kw, k0s=k0s),
        grid=(batch, ns + 1),
        in_specs=[tile(True)] + gate_specs(1) + [tile(True), tile(True)],
        out_specs=tile(True),
        out_shape=jax.ShapeDtypeStruct((r, dr), BF16),
        scratch_shapes=work + [pltpu.VMEM((tt, dr), F32), state],
        compiler_params=_params("arbitrary", "arbitrary"),
        name="rec_scan_bwd",
    )(z, *gate_args, hf, y)


def _rope_tables(seq, pad_rows):
    rows = seq // GRID_W
    row = jnp.repeat(jnp.arange(rows, dtype=F32), GRID_W)
    col = jnp.tile(jnp.arange(GRID_W, dtype=F32), rows)
    axis_dim = HEAD_DIM // 2
    inv_freq = ROPE_BASE ** (-jnp.arange(0, axis_dim, 2, dtype=F32) / axis_dim)
    ang = jnp.concatenate([row[:, None] * inv_freq, col[:, None] * inv_freq], axis=-1)
    cos, sin = jnp.cos(ang), jnp.sin(ang)
    cos2 = jnp.concatenate([cos, cos], axis=-1)
    sin2 = jnp.concatenate([-sin, sin], axis=-1)
    cos2 = jnp.concatenate([cos2, jnp.ones((pad_rows, HEAD_DIM), F32)], axis=0)
    sin2 = jnp.concatenate([sin2, jnp.zeros((pad_rows, HEAD_DIM), F32)], axis=0)
    return cos2, sin2


def _qkv_kernel(u_ref, w_ref, cos_ref, sin_ref, o_ref, *, nq, nkv, scale):
    u = u_ref[...]
    c = cos_ref[...]
    s = sin_ref[...]
    for c0, c1 in _col_groups(o_ref.shape[1]):
        p = jnp.dot(u, w_ref[:, c0:c1], preferred_element_type=F32)
        for h0 in range(c0, c1, HEAD_DIM):
            t = p[:, h0 - c0:h0 - c0 + HEAD_DIM]
            if h0 < nq + nkv:
                t = t * c + pltpu.roll(t, HEAD_DIM // 2, 1) * s
            if h0 < nq:
                t = t * scale
            o_ref[:, h0:h0 + HEAD_DIM] = t.astype(BF16)


def _qkv_call(u, w, cos2, sin2, seq, rows_x, tm):
    r, d = u.shape
    n = w.shape[1]
    nkv = N_KV_HEADS * HEAD_DIM
    n_seq_tiles = seq // tm
    tab = pl.BlockSpec((tm, HEAD_DIM), lambda i: (jnp.where(i < rows_x // tm, i % n_seq_tiles, n_seq_tiles), 0))
    return pl.pallas_call(
        functools.partial(_qkv_kernel, nq=n - 2 * nkv, nkv=nkv, scale=HEAD_DIM ** -0.5 * LOG2_E),
        grid=(r // tm,),
        in_specs=[pl.BlockSpec((tm, d), lambda i: (i, 0)),
                  pl.BlockSpec((d, n), lambda i: (0, 0), pipeline_mode=pl.Buffered(1)), tab, tab],
        out_specs=pl.BlockSpec((tm, n), lambda i: (i, 0)),
        out_shape=jax.ShapeDtypeStruct((r, n), BF16),
        compiler_params=_params("arbitrary"),
        name="att_qkv",
    )(u, w, cos2, sin2)


def _attn_masks(gq, ctx_len):
    blk = WINDOW
    qi = lax.broadcasted_iota(jnp.int32, (gq * blk, blk), 0) % blk
    ki = lax.broadcasted_iota(jnp.int32, (gq * blk, blk), 1)
    on = jnp.zeros((gq * blk, blk), F32)
    off = jnp.full((gq * blk, blk), NEG_INF, F32)
    prev = jnp.where(ki >= qi, 0.0, NEG_INF).astype(F32)
    nxt = jnp.where(ki <= qi, 0.0, NEG_INF).astype(F32)
    ctx_on = jnp.zeros((gq * blk, ctx_len), F32)
    variants = ((off, on, nxt), (prev, on, nxt), (prev, on, off), (off, off, off))
    return jnp.stack([jnp.concatenate(v + (ctx_on,), axis=1) for v in variants])


def _attn_kernel(sink_ref, q_ref, kp_ref, kc_ref, kn_ref, vp_ref, vc_ref, vn_ref, kx_ref, vx_ref, bias, o_ref,
                 k_all, v_all, *, g):
    blk = WINDOW
    band = 3 * blk
    for dst, srcs in ((k_all, (kp_ref, kc_ref, kn_ref)), (v_all, (vp_ref, vc_ref, vn_ref))):
        for o, src in enumerate(srcs):
            dst[o * blk:(o + 1) * blk, :] = src[...]
    k_all[band:, :] = kx_ref[...]
    v_all[band:, :] = vx_ref[...]
    nt_dims = (((1,), (1,)), ((), ()))
    gq = bias.shape[0] // blk
    for h0 in range(0, N_KV_HEADS * g, gq):
        kh = h0 // g
        ks = slice(kh * HEAD_DIM, (kh + 1) * HEAD_DIM)
        heads = range(h0, h0 + gq)
        qs = jnp.concatenate([q_ref[:, h * HEAD_DIM:(h + 1) * HEAD_DIM] for h in heads], axis=0)
        sk = jnp.concatenate([jnp.full((blk, 1), sink_ref[h] * LOG2_E, F32) for h in heads], axis=0)
        s = lax.dot_general(qs, k_all[:, ks], nt_dims, preferred_element_type=F32) + bias[...]
        m = jnp.maximum(jnp.max(s, -1, keepdims=True), sk)
        p = jnp.exp2(s - m)
        denom = jnp.exp2(sk - m) + jnp.sum(p, -1, keepdims=True)
        o = jnp.dot(p.astype(BF16), v_all[:, ks], preferred_element_type=F32) / denom
        for gi, h in enumerate(heads):
            o_ref[:, h * HEAD_DIM:(h + 1) * HEAD_DIM] = o[gi * blk:(gi + 1) * blk].astype(BF16)


def _attn_call(qkv, sink, batch, seq, ctx_len, ctx_queries):
    r, n = qkv.shape
    nkv = N_KV_HEADS * HEAD_DIM
    d = n - 2 * nkv
    g = d // HEAD_DIM // N_KV_HEADS
    gq = g
    blk = WINDOW
    nb = seq // blk
    ncb = ctx_len // blk
    kcol = d // nkv
    x0 = batch * seq

    def qrow(b, j):
        return jnp.where(j < nb, b * nb + j, x0 // blk + b * ncb + (j - nb))

    def krow(b, j, off):
        return b * nb + jnp.clip(jnp.minimum(j, nb - 1) + off, 0, nb - 1)

    assert nb >= 2
    qspec = pl.BlockSpec((blk, d), lambda b, j: (qrow(b, j), 0))
    band = lambda off, col: pl.BlockSpec((blk, nkv), lambda b, j: (krow(b, j, off), col))
    ctxs = lambda col: pl.BlockSpec((ctx_len, nkv), lambda b, j: (x0 // ctx_len + b, col))
    nkeys = 3 * blk + ctx_len
    mask = pl.BlockSpec((None, gq * blk, nkeys),
                        lambda b, j: (jnp.where(j >= nb, 3, jnp.where(j == 0, 0, jnp.where(j == nb - 1, 2, 1))), 0, 0))
    return pl.pallas_call(
        functools.partial(_attn_kernel, g=g),
        grid=(batch, nb + ncb if ctx_queries else nb),
        in_specs=[
            pl.BlockSpec(memory_space=pltpu.SMEM),
            qspec,
            band(-1, kcol), band(0, kcol), band(1, kcol),
            band(-1, kcol + 1), band(0, kcol + 1), band(1, kcol + 1),
            ctxs(kcol), ctxs(kcol + 1),
            mask,
        ],
        out_specs=qspec,
        out_shape=jax.ShapeDtypeStruct((r if ctx_queries else x0, d), BF16),
        scratch_shapes=[
            pltpu.VMEM((nkeys, nkv), BF16),
            pltpu.VMEM((nkeys, nkv), BF16),
        ],
        compiler_params=_params("arbitrary", "arbitrary"),
        name="att_core",
    )(sink, qkv, qkv, qkv, qkv, qkv, qkv, qkv, qkv, qkv, _attn_masks(gq, ctx_len))


def kernel(x, c, ctx, c_ctx, mod_w, mod_b, ln_mix_g, ln_mix_b, ln_ffn_g, ln_ffn_b, ffn_w_gate, ffn_w_up, ffn_w_down,
           rec_w_in, rec_conv_w, rec_conv_b, rec_gate_a_w, rec_gate_a_b, rec_gate_x_w, rec_gate_x_b, rec_lambda,
           rec_w_out, att_w_qkv, att_sink, att_w_o):
    batch, seq, d = x.shape
    ctx_len = ctx.shape[1]
    depth = mod_w.shape[0]
    rows_x, rows_c = batch * seq, batch * ctx_len
    assert seq % ctx_len == 0 and ctx_len % WINDOW == 0 and seq % GRID_W == 0
    assert d % (N_KV_HEADS * HEAD_DIM) == 0
    alpha = (2.0 * depth) ** 0.25

    tm = _pick(seq, (512, 256, 128))
    while rows_c % tm:
        tm //= 2
    tm_big = 2 * tm if (seq % (2 * tm) == 0 and rows_c % (2 * tm) == 0) else tm

    def row_fn_for(t):
        return lambda i: jnp.minimum((i * t) // seq, batch)

    row_fn, row_fn_big = row_fn_for(tm), row_fn_for(tm_big)

    mr = -(-(batch + 1) // V7X_SUBLANES) * V7X_SUBLANES
    cs = jnp.zeros((mr, d), F32).at[:batch].set(c).at[batch].set(c_ctx)
    mods5 = _mods_call(cs, mod_w, mod_b).reshape(depth, mr, N_MOD, 1, d)

    cos2, sin2 = _rope_tables(seq, tm_big)
    dr = rec_w_out.shape[1]
    kw, k0s = _gate_window_plan(dr, dr // RNN_BLOCKS)
    wa_all = _gate_windows(0.5 * rec_gate_a_w, kw, k0s)
    wx_all = _gate_windows(0.5 * rec_gate_x_w, kw, k0s)
    ba_all, bx_all = 0.5 * rec_gate_a_b, 0.5 * rec_gate_x_b

    def mixer_f32(layer):
        stacks = (rec_w_in, rec_w_out) if layer % N_MIXERS == 0 else (att_w_qkv, att_w_o)
        return [(s, layer // N_MIXERS) for s in stacks]

    def ffn_f32(layer):
        return [(s, layer) for s in (ffn_w_gate, ffn_w_up, ffn_w_down)]

    mix_wts = (rec_w_in[0].astype(BF16), rec_w_out[0].astype(BF16))
    ffn_wts = None
    h = u = None
    for i in range(depth):
        j = i // N_MIXERS
        last = i == depth - 1
        rows = rows_x if last else rows_x + rows_c
        if i % N_MIXERS == 0:
            if i == 0:
                tm0 = max(tm // 2, WINDOW)
                first = (x.reshape(rows_x, d), ctx.reshape(rows_c, d), mods5, row_fn_for(tm0), ffn_f32(0))
                y, zpre, h, ffn_wts = _win_call(None, mix_wts[0], tm0, first)
            else:
                y, zpre = _win_call(u, mix_wts[0], tm)
            yh = _scan_calls(zpre, y, rec_conv_w[j], rec_conv_b[j], wa_all[j], wx_all[j], ba_all[j], bx_all[j],
                             rec_lambda[j], kw, k0s, batch, seq, ctx_len)
            h, u = _proj_ln_call(yh, mix_wts[1], h, rows, mods5, i, ln_mix_g[i], ln_mix_b[i], row_fn, tm, alpha,
                                 "rec_out")
        else:
            qkv = _qkv_call(u, mix_wts[0], cos2, sin2, seq, rows_x, tm_big)
            ao = _attn_call(qkv, att_sink[j], batch, seq, ctx_len, ctx_queries=not last)
            h, u = _proj_ln_call(ao, mix_wts[1], h, rows, mods5, i, ln_mix_g[i], ln_mix_b[i], row_fn, tm, alpha,
                                 "att_out")
        next_wts = None if last else ffn_f32(i + 1) + mixer_f32(i + 1)
        h, u, casts = _ffn_call(u, ffn_wts, h, rows, mods5, i, next_wts, ln_ffn_g[i], ln_ffn_b[i], row_fn_big, tm_big,
                                alpha)
        if not last:
            ffn_wts, mix_wts = casts[:3], casts[3:]
    return h.reshape(batch, seq, d)
```

```python
import functools

import jax
import jax.numpy as jnp
from jax import lax
from jax.experimental import pallas as pl
from jax.experimental.pallas import tpu as pltpu

HEAD_DIM = 128
N_KV_HEADS = 4
WINDOW = 128
GRID_W = 64
ROPE_BASE = 10000.0
RNN_BLOCKS = 16
CONV_W = 4
CONV_LEFT = 2
RG_C = 8.0
LN_EPS = 1e-5
NEG_INF = -1e30
N_MIXERS = 2
N_MOD = 6
LOG2_E = 1.4426950408889634

V7X_LANES = 128
V7X_SUBLANES = 8
V7X_MXU_WIDTH = 256
V7X_VMEM_BYTES = 64 * 1024 * 1024
VMEM_LIMIT_BYTES = V7X_VMEM_BYTES - 8 * 1024 * 1024

F32 = jnp.float32
BF16 = jnp.bfloat16
EPILOGUE_ROWS = 128
LN_ROWS = 128


def _pick(n, cands):
    for c in cands:
        if n % c == 0:
            return c
    raise ValueError(f"no tile in {cands} divides {n}")


def _params(*sem):
    return pltpu.CompilerParams(dimension_semantics=sem, vmem_limit_bytes=VMEM_LIMIT_BYTES)


def _mod_spec(layer, chunk, row_fn):
    return lambda d: pl.BlockSpec((None, None, None, 1, d), lambda *g: (layer, row_fn(*g), chunk, 0, 0))


def _cast_side_job(items, steps, step_of):
    in_specs, args, out_specs, out_shape = [], [], [], []
    for stack, layer in items:
        _, rows_w, cols_w = stack.shape
        slab = _pick(rows_w, [s for s in (16, 32, 64, 128, 256, 512, 1024) if rows_w // s <= steps] + [rows_w])
        slab_of = lambda *g, n=rows_w // slab: jnp.minimum(step_of(*g), n - 1)
        in_specs.append(pl.BlockSpec((None, slab, cols_w), lambda *g, f=slab_of, l=layer: (l, f(*g), 0)))
        args.append(stack)
        out_specs.append(pl.BlockSpec((slab, cols_w), lambda *g, f=slab_of: (f(*g), 0)))
        out_shape.append(jax.ShapeDtypeStruct((rows_w, cols_w), BF16))
    return in_specs, args, out_specs, out_shape


def _layer_norm(v, g, b):
    mu = jnp.mean(v, axis=-1, keepdims=True)
    d = v - mu
    var = jnp.mean(d * d, axis=-1, keepdims=True)
    return d * lax.rsqrt(var + LN_EPS) * g + b


def _deepnorm_epilogue(acc_ref, res_ref, gate_ref, lg_ref, lb_ref, h_ref, alpha, u_ref=None, sh_ref=None, sc_ref=None,
                       rows=None):
    r0, r1 = rows if rows is not None else (0, acc_ref.shape[0])
    ch = min(r1 - r0, LN_ROWS)
    gate, lg, lb = gate_ref[...], lg_ref[...], lb_ref[...]
    if u_ref is not None:
        sh, sc1 = sh_ref[...], 1.0 + sc_ref[...]
    for k in range((r1 - r0) // ch):
        rs = slice(r0 + k * ch, r0 + (k + 1) * ch)
        hn = _layer_norm(alpha * res_ref[rs, :] + gate * acc_ref[rs, :], lg, lb)
        h_ref[rs, :] = hn
        if u_ref is not None:
            u_ref[rs, :] = (hn * sc1 + sh).astype(BF16)


def _mods_kernel(cs_ref, w_ref, b_ref, o_ref):
    s = cs_ref[...]
    s = (s * jax.nn.sigmoid(s)).astype(BF16)
    o_ref[...] = jnp.dot(s, w_ref[...].astype(BF16), preferred_element_type=F32) + b_ref[...]


def _mods_call(cs, mod_w, mod_b):
    depth, d, n = mod_w.shape
    mr = cs.shape[0]
    tn = _pick(n, (1024, 512, 256, 128))
    return pl.pallas_call(
        _mods_kernel,
        grid=(depth, n // tn),
        in_specs=[
            pl.BlockSpec((mr, d), lambda l, j: (0, 0)),
            pl.BlockSpec((None, d, tn), lambda l, j: (l, 0, j)),
            pl.BlockSpec((None, 1, tn), lambda l, j: (l, 0, j)),
        ],
        out_specs=pl.BlockSpec((None, mr, tn), lambda l, j: (l, 0, j)),
        out_shape=jax.ShapeDtypeStruct((depth, mr, n), F32),
        compiler_params=_params("arbitrary", "arbitrary"),
        name="mods",
    )(cs, mod_w, mod_b.reshape(depth, 1, n))


def _proj_ln_kernel(a_ref, w_ref, res_ref, gate_ref, sh_ref, sc_ref, lg_ref, lb_ref, h_ref, u_ref, acc, *, alpha):
    tm = a_ref.shape[0]
    halves = 2 if tm % (2 * EPILOGUE_ROWS) == 0 else 1
    for hh in range(halves):
        rows = (hh * tm // halves, (hh + 1) * tm // halves)
        acc[rows[0]:rows[1], :] = jnp.dot(a_ref[rows[0]:rows[1], :], w_ref[...], preferred_element_type=F32)
        _deepnorm_epilogue(acc, res_ref, gate_ref, lg_ref, lb_ref, h_ref, alpha, u_ref, sh_ref, sc_ref, rows=rows)


def _proj_ln_call(a, w, res, rows, mods5, layer, ln_g, ln_b, row_fn, tm, alpha, name):
    r, k = rows, a.shape[1]
    d = w.shape[1]
    vec = pl.BlockSpec((1, d), lambda i: (0, 0))
    return pl.pallas_call(
        functools.partial(_proj_ln_kernel, alpha=alpha),
        grid=(r // tm,),
        in_specs=[
            pl.BlockSpec((tm, k), lambda i: (i, 0)),
            pl.BlockSpec((k, d), lambda i: (0, 0), pipeline_mode=pl.Buffered(1)),
            pl.BlockSpec((tm, d), lambda i: (i, 0)),
            _mod_spec(layer, 2, row_fn)(d),
            _mod_spec(layer, 3, row_fn)(d),
            _mod_spec(layer, 4, row_fn)(d),
            vec,
            vec,
        ],
        out_specs=[pl.BlockSpec((tm, d), lambda i: (i, 0)), pl.BlockSpec((tm, d), lambda i: (i, 0))],
        out_shape=[jax.ShapeDtypeStruct((r, d), F32), jax.ShapeDtypeStruct((r, d), BF16)],
        scratch_shapes=[pltpu.VMEM((tm, d), F32)],
        compiler_params=_params("arbitrary"),
        name=name,
    )(a, w, res, mods5, mods5, mods5, ln_g.reshape(1, d), ln_b.reshape(1, d))


def _ffn_kernel(*refs, alpha, with_u, n_m, n_epi, n_cast):
    if with_u:
        u_ref, wg_ref, wu_ref, wd_ref, res_ref, gate_ref, sh_ref, sc_ref, lg_ref, lb_ref = refs[:10]
        cast_in, (h_ref, un_ref) = refs[10:10 + n_cast], refs[10 + n_cast:12 + n_cast]
        cast_out, (acc, done) = refs[12 + n_cast:12 + 2 * n_cast], refs[12 + 2 * n_cast:]
    else:
        u_ref, wg_ref, wu_ref, wd_ref, res_ref, gate_ref, lg_ref, lb_ref, h_ref, acc, done = refs
        un_ref = sh_ref = sc_ref = None
        cast_in = cast_out = ()
    i, j = pl.program_id(0), pl.program_id(1)
    ch = res_ref.shape[0]

    def matmuls(first):
        for src, dst in zip(cast_in, cast_out):
            dst[...] = src[...].astype(BF16)
        u = u_ref[...]
        g = jnp.dot(u, wg_ref[...], preferred_element_type=F32)
        up = jnp.dot(u, wu_ref[...], preferred_element_type=F32)
        hid = (g * jax.nn.sigmoid(g) * up).astype(BF16)
        o = jnp.dot(hid, wd_ref[...], preferred_element_type=F32)
        if first:
            acc[...] = o
        else:
            acc[...] += o

    def epilogue():
        sub = min(ch, LN_ROWS)
        for k in range(ch // sub):
            rs = pl.ds(pl.multiple_of(j * ch + k * sub, sub), sub)
            ks = slice(k * sub, (k + 1) * sub)
            hn = _layer_norm(alpha * res_ref[ks, :] + gate_ref[...] * done[rs, :], lg_ref[...], lb_ref[...])
            h_ref[ks, :] = hn
            if with_u:
                un_ref[ks, :] = (hn * (1.0 + sc_ref[...]) + sh_ref[...]).astype(BF16)

    land = jnp.logical_and
    live = i < n_m
    epi = land(i > 0, j < n_epi)

    @pl.when(land(jnp.logical_not(live), j == 0))
    def _():
        done[...] = acc[...]

    @pl.when(land(land(live, epi), j == 0))
    def _():
        done[...] = acc[...]
        matmuls(True)
        epilogue()

    @pl.when(land(land(live, epi), j > 0))
    def _():
        matmuls(False)
        epilogue()

    @pl.when(land(land(live, jnp.logical_not(epi)), j == 0))
    def _():
        matmuls(True)

    @pl.when(land(land(live, jnp.logical_not(epi)), j > 0))
    def _():
        matmuls(False)

    @pl.when(land(jnp.logical_not(live), epi))
    def _():
        epilogue()


def _ffn_call(u, wts, res, rows, mods5, layer, next_wts, ln_g, ln_b, row_fn, tm, alpha):
    wg, wu, wd = wts
    r, d = rows, u.shape[1]
    hid = wg.shape[1]
    th = _pick(hid, (512, 256, 128))
    n_m, n_h = r // tm, hid // th
    ch = min(tm, EPILOGUE_ROWS)
    n_epi = tm // ch
    assert n_epi <= n_h
    with_u = next_wts is not None
    prev = lambda i: jnp.maximum(i - 1, 0)
    row2 = lambda i, j: row_fn(prev(i))
    jw = lambda i, j: jnp.where(i < n_m, j, n_h - 1)
    vec = pl.BlockSpec((1, d), lambda i, j: (0, 0))
    chunk = pl.BlockSpec((ch, d), lambda i, j: (prev(i) * n_epi + jnp.where(i == 0, 0, jnp.minimum(j, n_epi - 1)), 0))
    in_specs = [
        pl.BlockSpec((tm, d), lambda i, j: (jnp.minimum(i, n_m - 1), 0)),
        pl.BlockSpec((d, th), lambda i, j: (0, jw(i, j))),
        pl.BlockSpec((d, th), lambda i, j: (0, jw(i, j))),
        pl.BlockSpec((th, d), lambda i, j: (jw(i, j), 0)),
        chunk,
        _mod_spec(layer, 5, row2)(d),
    ]
    args = [u, wg, wu, wd, res, mods5]
    if with_u:
        in_specs += [_mod_spec(layer + 1, 0, row2)(d), _mod_spec(layer + 1, 1, row2)(d)]
        args += [mods5, mods5]
    in_specs += [vec, vec]
    args += [ln_g.reshape(1, d), ln_b.reshape(1, d)]
    out_specs = [chunk]
    out_shape = [jax.ShapeDtypeStruct((r, d), F32)]
    if with_u:
        out_specs.append(chunk)
        out_shape.append(jax.ShapeDtypeStruct((r, d), BF16))
        c_in, c_args, c_out, c_shape = _cast_side_job(next_wts, n_m * n_h, lambda i, j: i * n_h + j)
        in_specs += c_in
        args += c_args
        out_specs += c_out
        out_shape += c_shape
    n_cast = len(next_wts) if with_u else 0
    out = pl.pallas_call(
        functools.partial(_ffn_kernel, alpha=alpha, with_u=with_u, n_m=n_m, n_epi=n_epi, n_cast=n_cast),
        grid=(n_m + 1, n_h),
        in_specs=in_specs,
        out_specs=out_specs,
        out_shape=out_shape,
        scratch_shapes=[pltpu.VMEM((tm, d), F32), pltpu.VMEM((tm, d), F32)],
        compiler_params=_params("arbitrary", "arbitrary"),
        name="ffn",
    )(*args)
    return (out[0], out[1], tuple(out[2:])) if with_u else (out[0], None, None)


def _gelu_tanh(x):
    return x * (0.5 * (1.0 + jnp.tanh(0.7978845608028654 * (x + 0.044715 * (x * x * x)))))


def _col_groups(n, width=2 * V7X_MXU_WIDTH):
    return [(c, min(c + width, n)) for c in range(0, n, width)]


def _win_kernel(*refs, n_lat, n_cast):
    if n_lat is None:
        u_ref, w_ref, y_ref, z_ref = refs
        u = u_ref[...]
    else:
        x_ref, c_ref, sh_ref, sc_ref, w_ref = refs[:5]
        cast_in, (y_ref, z_ref, h_ref) = refs[5:5 + n_cast], refs[5 + n_cast:8 + n_cast]
        cast_out, u_s = refs[8 + n_cast:8 + 2 * n_cast], refs[8 + 2 * n_cast]

        def emit(src_ref):
            v = src_ref[...]
            h_ref[...] = v
            u_s[...] = (v * (1.0 + sc_ref[...]) + sh_ref[...]).astype(BF16)

        pl.when(pl.program_id(0) < n_lat)(lambda: emit(x_ref))
        pl.when(pl.program_id(0) >= n_lat)(lambda: emit(c_ref))
        u = u_s[...]
        for src, dst in zip(cast_in, cast_out):
            dst[...] = src[...].astype(BF16)
    dr = y_ref.shape[1]
    for c0, c1 in _col_groups(dr):
        y_ref[:, c0:c1] = _gelu_tanh(jnp.dot(u, w_ref[:, c0:c1], preferred_element_type=F32)).astype(BF16)
    for c0, c1 in _col_groups(dr):
        z_ref[:, c0:c1] = jnp.dot(u, w_ref[:, dr + c0:dr + c1], preferred_element_type=F32)


def _win_call(u, w_in, tm, first=None):
    d, n = w_in.shape
    dr = n // 2
    wspec = pl.BlockSpec((d, n), lambda i: (0, 0), pipeline_mode=pl.Buffered(1))
    tile = lambda width: pl.BlockSpec((tm, width), lambda i: (i, 0))
    if first is None:
        r, n_lat, n_cast = u.shape[0], None, 0
        in_specs, args = [tile(d), wspec], [u, w_in]
        extra_out, extra_shape, scratch = [], [], []
    else:
        x2, c2, mods5, row_fn, cast_items = first
        n_lat, n_cast = x2.shape[0] // tm, len(cast_items)
        r = x2.shape[0] + c2.shape[0]
        c_in, c_args, c_out, c_shape = _cast_side_job(cast_items, r // tm, lambda i: i)
        in_specs = [
            pl.BlockSpec((tm, d), lambda i: (jnp.minimum(i, n_lat - 1), 0)),
            pl.BlockSpec((tm, d), lambda i: (jnp.maximum(i - n_lat, 0), 0)),
            _mod_spec(0, 0, row_fn)(d),
            _mod_spec(0, 1, row_fn)(d),
            wspec,
        ] + c_in
        args = [x2, c2, mods5, mods5, w_in] + c_args
        extra_out, extra_shape = [tile(d)] + c_out, [jax.ShapeDtypeStruct((r, d), F32)] + c_shape
        scratch = [pltpu.VMEM((tm, d), BF16)]
    out = pl.pallas_call(
        functools.partial(_win_kernel, n_lat=n_lat, n_cast=n_cast),
        grid=(r // tm,),
        in_specs=in_specs,
        out_specs=[tile(dr), tile(dr)] + extra_out,
        out_shape=[jax.ShapeDtypeStruct((r, dr), BF16), jax.ShapeDtypeStruct((r, dr), F32)] + extra_shape,
        scratch_shapes=scratch,
        compiler_params=_params("arbitrary"),
        name="rec_in",
    )(*args)
    return out if first is None else (out[0], out[1], out[2], tuple(out[3:]))


def _gate_window_plan(dr, bw):
    tn = V7X_MXU_WIDTH
    spans = []
    for n in range(dr // tn):
        c0 = n * tn
        lo = (c0 // bw) * bw
        hi = ((c0 + tn - 1) // bw + 1) * bw
        spans.append(((lo // V7X_LANES) * V7X_LANES, -(-hi // V7X_LANES) * V7X_LANES))
    kw = max(h - l for l, h in spans)
    return kw, tuple(min(l, dr - kw) for l, _ in spans)


def _gate_windows(w, kw, k0s):
    lead, bw = w.shape[:-3], w.shape[-1]
    tn = V7X_MXU_WIDTH
    w = w.astype(BF16)
    zero = jnp.zeros((), BF16)
    keep = [(0, 0, 0)] * (len(lead) + 1)
    tiles = []
    for n, k0 in enumerate(k0s):
        c0 = n * tn
        strips, filled = [], 0
        for h in range(c0 // bw, (c0 + tn - 1) // bw + 1):
            wr, wc = h * bw - k0, h * bw - c0
            sr0, sr1 = max(0, -wr), bw - max(0, wr + bw - kw)
            sc0, sc1 = max(0, -wc), bw - max(0, wc + bw - tn)
            if max(wr, 0) > filled:
                strips.append(jnp.zeros((*lead, max(wr, 0) - filled, tn), BF16))
            col = max(wc, 0)
            strips.append(lax.pad(w[..., h, sr0:sr1, sc0:sc1], zero, keep + [(col, tn - col - (sc1 - sc0), 0)]))
            filled = max(wr, 0) + sr1 - sr0
        if filled < kw:
            strips.append(jnp.zeros((*lead, kw - filled, tn), BF16))
        tiles.append(jnp.concatenate(strips, axis=-2))
    return jnp.stack(tiles, axis=-3)


def _gates_scan(zb, z_ref, wa_ref, wx_ref, ba_ref, bx_ref, lam_ref, a_s, b_s, dst, carry, kw, k0s, reverse):
    tt = zb.shape[0]
    tn = V7X_MXU_WIDTH
    sub = V7X_SUBLANES
    ng = tt // sub
    row = lax.broadcasted_iota(jnp.int32, (sub, tn), 0)
    entry = sub - 1 if reverse else 0
    for n, k0 in enumerate(k0s):
        cs = slice(n * tn, (n + 1) * tn)
        zw = zb[:, k0:k0 + kw]
        ta = jnp.tanh(jnp.dot(zw, wa_ref[n], preferred_element_type=F32) + ba_ref[:, cs])
        tx = jnp.tanh(jnp.dot(zw, wx_ref[n], preferred_element_type=F32) + bx_ref[:, cs])
        nl = -lam_ref[:, cs]
        half = (0.5 * RG_C) * (jnp.maximum(nl, 0.0) + jnp.log1p(jnp.exp(-jnp.abs(nl))))
        q = ta * half + half
        a = jnp.exp(-q)
        a_s[:, cs] = a
        x = jnp.tanh(q) * (1.0 + a * a)
        root = jnp.where(x > 0.0, x * lax.rsqrt(x), 0.0)
        b_s[:, cs] = root * (0.5 * tx + 0.5) * z_ref[:, cs]

        h = carry[:, cs]
        for g in (range(ng - 1, -1, -1) if reverse else range(ng)):
            rs = slice(g * sub, (g + 1) * sub)
            a8 = a_s[rs, cs]
            b8 = b_s[rs, cs]
            b8 = b8 + jnp.where(row == entry, a8 * h, 0.0)
            a8 = jnp.where(row == entry, 0.0, a8)
            for s in (1, 2, 4):
                shift = sub - s if reverse else s
                b8 = a8 * pltpu.roll(b8, shift, 0) + b8
                if s < 4:
                    a8 = a8 * pltpu.roll(a8, shift, 0)
            dst[rs, cs] = b8
            h = jnp.broadcast_to(b8[0:1, :] if reverse else b8[sub - 1:sub, :], (sub, tn))
        carry[:, cs] = h


def _scan_fwd_kernel(zc_ref, zp_ref, zn_ref, cw_ref, cb_ref, wa_ref, wx_ref, ba_ref, bx_ref, lam_ref,
                     hf_ref, z_ref, xpad, zb, a_s, b_s, carry, *, kw, k0s, ns):
    tt = zc_ref.shape[0]
    tn = V7X_MXU_WIDTH
    sub = V7X_SUBLANES
    t = pl.program_id(1)

    @pl.when(t == 0)
    def _():
        carry[...] = jnp.zeros_like(carry)

    first = t <= 1
    last = jnp.logical_or(t == 0, t == ns)
    xpad[0:sub, :] = jnp.where(first, 0.0, zp_ref[...])
    xpad[sub:sub + tt, :] = zc_ref[...]
    xpad[sub + tt:2 * sub + tt, :] = jnp.where(last, 0.0, zn_ref[...])
    for n in range(len(k0s)):
        cs = slice(n * tn, (n + 1) * tn)
        xa = xpad[:, cs]
        acc = None
        for j in range(CONV_W):
            shift = (CONV_LEFT - j) % xa.shape[0]
            xs = pltpu.roll(xa, shift, 0) if shift else xa
            term = xs[sub:sub + tt, :] * cw_ref[j:j + 1, cs]
            acc = term if acc is None else acc + term
        z = acc + cb_ref[:, cs]
        z_ref[:, cs] = z
        zb[:, cs] = z.astype(BF16)

    _gates_scan(zb, z_ref, wa_ref, wx_ref, ba_ref, bx_ref, lam_ref, a_s, b_s, hf_ref, carry, kw, k0s, reverse=False)


def _scan_bwd_kernel(z_ref, wa_ref, wx_ref, ba_ref, bx_ref, lam_ref, hf_ref, y_ref, o_ref, zb, a_s, b_s, h_s, carry,
                     *, kw, k0s):
    tn = V7X_MXU_WIDTH

    @pl.when(pl.program_id(1) == 0)
    def _():
        carry[...] = jnp.zeros_like(carry)

    zb[...] = z_ref[...].astype(BF16)
    _gates_scan(zb, z_ref, wa_ref, wx_ref, ba_ref, bx_ref, lam_ref, a_s, b_s, h_s, carry, kw, k0s, reverse=True)
    for n in range(len(k0s)):
        cs = slice(n * tn, (n + 1) * tn)
        o_ref[:, cs] = (y_ref[:, cs].astype(F32) * (hf_ref[:, cs] + h_s[:, cs])).astype(BF16)


def _scan_calls(zpre, y, conv_w, conv_b, wa, wx, ba, bx, lam, kw, k0s, batch, seq, ctx_len):
    r, dr = zpre.shape
    tt = ctx_len
    ns = seq // tt
    nx = batch * seq // tt
    g8 = tt // V7X_SUBLANES
    nblk8 = r // V7X_SUBLANES
    nt = len(k0s)

    def cur(reverse):
        def f(b, t):
            st = (ns - t) if reverse else (t - 1)
            return jnp.where(t == 0, nx + b, b * ns + st)
        return f

    def tile(reverse):
        return pl.BlockSpec((tt, dr), lambda b, t: (cur(reverse)(b, t), 0))

    fw = cur(False)
    halo_p = pl.BlockSpec((V7X_SUBLANES, dr), lambda b, t: (jnp.maximum(fw(b, t) * g8 - 1, 0), 0))
    halo_n = pl.BlockSpec((V7X_SUBLANES, dr), lambda b, t: (jnp.minimum(fw(b, t) * g8 + g8, nblk8 - 1), 0))
    vec = lambda d: pl.BlockSpec((None, 1, dr), lambda b, t: (d, 0, 0))
    wspec = lambda d: pl.BlockSpec((None, nt, kw, V7X_MXU_WIDTH), lambda b, t: (d, 0, 0, 0),
                                   pipeline_mode=pl.Buffered(1))
    ba, bx, lam = (v.reshape(2, 1, dr) for v in (ba, bx, lam))
    gate_specs = lambda d: [wspec(d), wspec(d), vec(d), vec(d), vec(d)]
    gate_args = [wa, wx, ba, bx, lam]
    work = [
        pltpu.VMEM((tt, dr), BF16),
        pltpu.VMEM((tt, dr), F32),
        pltpu.VMEM((tt, dr), F32),
    ]
    state = pltpu.VMEM((V7X_SUBLANES, dr), F32)
    hf, z = pl.pallas_call(
        functools.partial(_scan_fwd_kernel, kw=kw, k0s=k0s, ns=ns),
        grid=(batch, ns + 1),
        in_specs=[tile(False), halo_p, halo_n, pl.BlockSpec((CONV_W, dr), lambda b, t: (0, 0)),
                  pl.BlockSpec((1, dr), lambda b, t: (0, 0))] + gate_specs(0),
        out_specs=[tile(False), tile(False)],
        out_shape=[jax.ShapeDtypeStruct((r, dr), F32), jax.ShapeDtypeStruct((r, dr), F32)],
        scratch_shapes=[pltpu.VMEM((tt + 2 * V7X_SUBLANES, dr), F32)] + work + [state],
        compiler_params=_params("arbitrary", "arbitrary"),
        name="rec_scan_fwd",
    )(zpre, zpre, zpre, conv_w, conv_b.reshape(1, dr), *gate_args)
    return pl.pallas_call(
        functools.partial(_scan_bwd_kernel, kw=kw, k0s=k0s),
        grid=(batch, ns + 1),
        in_specs=[tile(True)] + gate_specs(1) + [tile(True), tile(True)],
        out_specs=tile(True),
        out_shape=jax.ShapeDtypeStruct((r, dr), BF16),
        scratch_shapes=work + [pltpu.VMEM((tt, dr), F32), state],
        compiler_params=_params("arbitrary", "arbitrary"),
        name="rec_scan_bwd",
    )(z, *gate_args, hf, y)


def _rope_tables(seq, pad_rows):
    rows = seq // GRID_W
    row = jnp.repeat(jnp.arange(rows, dtype=F32), GRID_W)
    col = jnp.tile(jnp.arange(GRID_W, dtype=F32), rows)
    axis_dim = HEAD_DIM // 2
    inv_freq = ROPE_BASE ** (-jnp.arange(0, axis_dim, 2, dtype=F32) / axis_dim)
    ang = jnp.concatenate([row[:, None] * inv_freq, col[:, None] * inv_freq], axis=-1)
    cos, sin = jnp.cos(ang), jnp.sin(ang)
    cos2 = jnp.concatenate([cos, cos], axis=-1)
    sin2 = jnp.concatenate([-sin, sin], axis=-1)
    cos2 = jnp.concatenate([cos2, jnp.ones((pad_rows, HEAD_DIM), F32)], axis=0)
    sin2 = jnp.concatenate([sin2, jnp.zeros((pad_rows, HEAD_DIM), F32)], axis=0)
    return cos2, sin2


def _qkv_kernel(u_ref, w_ref, cos_ref, sin_ref, o_ref, *, nq, nkv, scale):
    u = u_ref[...]
    c = cos_ref[...]
    s = sin_ref[...]
    for c0, c1 in _col_groups(o_ref.shape[1]):
        p = jnp.dot(u, w_ref[:, c0:c1], preferred_element_type=F32)
        for h0 in range(c0, c1, HEAD_DIM):
            t = p[:, h0 - c0:h0 - c0 + HEAD_DIM]
            if h0 < nq + nkv:
                t = t * c + pltpu.roll(t, HEAD_DIM // 2, 1) * s
            if h0 < nq:
                t = t * scale
            o_ref[:, h0:h0 + HEAD_DIM] = t.astype(BF16)


def _qkv_call(u, w, cos2, sin2, seq, rows_x, tm):
    r, d = u.shape
    n = w.shape[1]
    nkv = N_KV_HEADS * HEAD_DIM
    n_seq_tiles = seq // tm
    tab = pl.BlockSpec((tm, HEAD_DIM), lambda i: (jnp.where(i < rows_x // tm, i % n_seq_tiles, n_seq_tiles), 0))
    return pl.pallas_call(
        functools.partial(_qkv_kernel, nq=n - 2 * nkv, nkv=nkv, scale=HEAD_DIM ** -0.5 * LOG2_E),
        grid=(r // tm,),
        in_specs=[pl.BlockSpec((tm, d), lambda i: (i, 0)),
                  pl.BlockSpec((d, n), lambda i: (0, 0), pipeline_mode=pl.Buffered(1)), tab, tab],
        out_specs=pl.BlockSpec((tm, n), lambda i: (i, 0)),
        out_shape=jax.ShapeDtypeStruct((r, n), BF16),
        compiler_params=_params("arbitrary"),
        name="att_qkv",
    )(u, w, cos2, sin2)


def _attn_masks(gq, ctx_len):
    blk = WINDOW
    qi = lax.broadcasted_iota(jnp.int32, (gq * blk, blk), 0) % blk
    ki = lax.broadcasted_iota(jnp.int32, (gq * blk, blk), 1)
    on = jnp.zeros((gq * blk, blk), F32)
    off = jnp.full((gq * blk, blk), NEG_INF, F32)
    prev = jnp.where(ki >= qi, 0.0, NEG_INF).astype(F32)
    nxt = jnp.where(ki <= qi, 0.0, NEG_INF).astype(F32)
    ctx_on = jnp.zeros((gq * blk, ctx_len), F32)
    variants = ((off, on, nxt), (prev, on, nxt), (prev, on, off), (off, off, off))
    return jnp.stack([jnp.concatenate(v + (ctx_on,), axis=1) for v in variants])


def _attn_kernel(sink_ref, q_ref, kp_ref, kc_ref, kn_ref, vp_ref, vc_ref, vn_ref, kx_ref, vx_ref, bias, o_ref,
                 k_all, v_all, *, g):
    blk = WINDOW
    band = 3 * blk
    for dst, srcs in ((k_all, (kp_ref, kc_ref, kn_ref)), (v_all, (vp_ref, vc_ref, vn_ref))):
        for o, src in enumerate(srcs):
            dst[o * blk:(o + 1) * blk, :] = src[...]
    k_all[band:, :] = kx_ref[...]
    v_all[band:, :] = vx_ref[...]
    nt_dims = (((1,), (1,)), ((), ()))
    gq = bias.shape[0] // blk
    for h0 in range(0, N_KV_HEADS * g, gq):
        kh = h0 // g
        ks = slice(kh * HEAD_DIM, (kh + 1) * HEAD_DIM)
        heads = range(h0, h0 + gq)
        qs = jnp.concatenate([q_ref[:, h * HEAD_DIM:(h + 1) * HEAD_DIM] for h in heads], axis=0)
        sk = jnp.concatenate([jnp.full((blk, 1), sink_ref[h] * LOG2_E, F32) for h in heads], axis=0)
        s = lax.dot_general(qs, k_all[:, ks], nt_dims, preferred_element_type=F32) + bias[...]
        m = jnp.maximum(jnp.max(s, -1, keepdims=True), sk)
        p = jnp.exp2(s - m)
        denom = jnp.exp2(sk - m) + jnp.sum(p, -1, keepdims=True)
        o = jnp.dot(p.astype(BF16), v_all[:, ks], preferred_element_type=F32) / denom
        for gi, h in enumerate(heads):
            o_ref[:, h * HEAD_DIM:(h + 1) * HEAD_DIM] = o[gi * blk:(gi + 1) * blk].astype(BF16)


def _attn_call(qkv, sink, batch, seq, ctx_len, ctx_queries):
    r, n = qkv.shape
    nkv = N_KV_HEADS * HEAD_DIM
    d = n - 2 * nkv
    g = d // HEAD_DIM // N_KV_HEADS
    gq = g
    blk = WINDOW
    nb = seq // blk
    ncb = ctx_len // blk
    kcol = d // nkv
    x0 = batch * seq

    def qrow(b, j):
        return jnp.where(j < nb, b * nb + j, x0 // blk + b * ncb + (j - nb))

    def krow(b, j, off):
        return b * nb + jnp.clip(jnp.minimum(j, nb - 1) + off, 0, nb - 1)

    assert nb >= 2
    qspec = pl.BlockSpec((blk, d), lambda b, j: (qrow(b, j), 0))
    band = lambda off, col: pl.BlockSpec((blk, nkv), lambda b, j: (krow(b, j, off), col))
    ctxs = lambda col: pl.BlockSpec((ctx_len, nkv), lambda b, j: (x0 // ctx_len + b, col))
    nkeys = 3 * blk + ctx_len
    mask = pl.BlockSpec((None, gq * blk, nkeys),
                        lambda b, j: (jnp.where(j >= nb, 3, jnp.where(j == 0, 0, jnp.where(j == nb - 1, 2, 1))), 0, 0))
    return pl.pallas_call(
        functools.partial(_attn_kernel, g=g),
        grid=(batch, nb + ncb if ctx_queries else nb),
        in_specs=[
            pl.BlockSpec(memory_space=pltpu.SMEM),
            qspec,
            band(-1, kcol), band(0, kcol), band(1, kcol),
            band(-1, kcol + 1), band(0, kcol + 1), band(1, kcol + 1),
            ctxs(kcol), ctxs(kcol + 1),
            mask,
        ],
        out_specs=qspec,
        out_shape=jax.ShapeDtypeStruct((r if ctx_queries else x0, d), BF16),
        scratch_shapes=[
            pltpu.VMEM((nkeys, nkv), BF16),
            pltpu.VMEM((nkeys, nkv), BF16),
        ],
        compiler_params=_params("arbitrary", "arbitrary"),
        name="att_core",
    )(sink, qkv, qkv, qkv, qkv, qkv, qkv, qkv, qkv, qkv, _attn_masks(gq, ctx_len))


def kernel(x, c, ctx, c_ctx, mod_w, mod_b, ln_mix_g, ln_mix_b, ln_ffn_g, ln_ffn_b, ffn_w_gate, ffn_w_up, ffn_w_down,
           rec_w_in, rec_conv_w, rec_conv_b, rec_gate_a_w, rec_gate_a_b, rec_gate_x_w, rec_gate_x_b, rec_lambda,
           rec_w_out, att_w_qkv, att_sink, att_w_o):
    batch, seq, d = x.shape
    ctx_len = ctx.shape[1]
    depth = mod_w.shape[0]
    rows_x, rows_c = batch * seq, batch * ctx_len
    assert seq % ctx_len == 0 and ctx_len % WINDOW == 0 and seq % GRID_W == 0
    assert d % (N_KV_HEADS * HEAD_DIM) == 0
    alpha = (2.0 * depth) ** 0.25

    tm = _pick(seq, (512, 256, 128))
    while rows_c % tm:
        tm //= 2
    tm_big = 2 * tm if (seq % (2 * tm) == 0 and rows_c % (2 * tm) == 0) else tm

    def row_fn_for(t):
        return lambda i: jnp.minimum((i * t) // seq, batch)

    row_fn, row_fn_big = row_fn_for(tm), row_fn_for(tm_big)

    mr = -(-(batch + 1) // V7X_SUBLANES) * V7X_SUBLANES
    cs = jnp.zeros((mr, d), F32).at[:batch].set(c).at[batch].set(c_ctx)
    mods5 = _mods_call(cs, mod_w, mod_b).reshape(depth, mr, N_MOD, 1, d)

    cos2, sin2 = _rope_tables(seq, tm_big)
    dr = rec_w_out.shape[1]
    kw, k0s = _gate_window_plan(dr, dr // RNN_BLOCKS)
    wa_all = _gate_windows(0.5 * rec_gate_a_w, kw, k0s)
    wx_all = _gate_windows(0.5 * rec_gate_x_w, kw, k0s)
    ba_all, bx_all = 0.5 * rec_gate_a_b, 0.5 * rec_gate_x_b

    def mixer_f32(layer):
        stacks = (rec_w_in, rec_w_out) if layer % N_MIXERS == 0 else (att_w_qkv, att_w_o)
        return [(s, layer // N_MIXERS) for s in stacks]

    def ffn_f32(layer):
        return [(s, layer) for s in (ffn_w_gate, ffn_w_up, ffn_w_down)]

    mix_wts = (rec_w_in[0].astype(BF16), rec_w_out[0].astype(BF16))
    ffn_wts = None
    h = u = None
    for i in range(depth):
        j = i // N_MIXERS
        last = i == depth - 1
        rows = rows_x if last else rows_x + rows_c
        if i % N_MIXERS == 0:
            if i == 0:
                tm0 = max(tm // 2, WINDOW)
                first = (x.reshape(rows_x, d), ctx.reshape(rows_c, d), mods5, row_fn_for(tm0), ffn_f32(0))
                y, zpre, h, ffn_wts = _win_call(None, mix_wts[0], tm0, first)
            else:
                y, zpre = _win_call(u, mix_wts[0], tm)
            yh = _scan_calls(zpre, y, rec_conv_w[j], rec_conv_b[j], wa_all[j], wx_all[j], ba_all[j], bx_all[j],
                             rec_lambda[j], kw, k0s, batch, seq, ctx_len)
            h, u = _proj_ln_call(yh, mix_wts[1], h, rows, mods5, i, ln_mix_g[i], ln_mix_b[i], row_fn, tm, alpha,
                                 "rec_out")
        else:
            qkv = _qkv_call(u, mix_wts[0], cos2, sin2, seq, rows_x, tm_big)
            ao = _attn_call(qkv, att_sink[j], batch, seq, ctx_len, ctx_queries=not last)
            h, u = _proj_ln_call(ao, mix_wts[1], h, rows, mods5, i, ln_mix_g[i], ln_mix_b[i], row_fn, tm, alpha,
                                 "att_out")
        next_wts = None if last else ffn_f32(i + 1) + mixer_f32(i + 1)
        h, u, casts = _ffn_call(u, ffn_wts, h, rows, mods5, i, next_wts, ln_ffn_g[i], ln_ffn_b[i], row_fn_big, tm_big,
                                alpha)
        if not last:
            ffn_wts, mix_wts = casts[:3], casts[3:]
    return h.reshape(batch, seq, d)
```

```python
import functools

import jax
import jax.numpy as jnp
from jax import lax
from jax.experimental import pallas as pl
from jax.experimental.pallas import tpu as pltpu

HEAD_DIM = 128
N_KV_HEADS = 4
WINDOW = 128
GRID_W = 64
ROPE_BASE = 10000.0
RNN_BLOCKS = 16
CONV_W = 4
CONV_LEFT = 2
RG_C = 8.0
LN_EPS = 1e-5
NEG_INF = -1e30
N_MIXERS = 2
N_MOD = 6
LOG2_E = 1.4426950408889634

V7X_LANES = 128
V7X_SUBLANES = 8
V7X_MXU_WIDTH = 256
V7X_VMEM_BYTES = 64 * 1024 * 1024
VMEM_LIMIT_BYTES = V7X_VMEM_BYTES - 8 * 1024 * 1024

F32 = jnp.float32
BF16 = jnp.bfloat16
EPILOGUE_ROWS = 128


def _pick(n, cands):
    for c in cands:
        if n % c == 0:
            return c
    raise ValueError(f"no tile in {cands} divides {n}")


def _params(*sem):
    return pltpu.CompilerParams(dimension_semantics=sem, vmem_limit_bytes=VMEM_LIMIT_BYTES)


def _mod_spec(layer, chunk, row_fn):
    return lambda d: pl.BlockSpec((None, None, None, 1, d), lambda *g: (layer, row_fn(*g), chunk, 0, 0))


def _cast_side_job(items, steps, step_of):
    in_specs, args, out_specs, out_shape = [], [], [], []
    for stack, layer in items:
        _, rows_w, cols_w = stack.shape
        slab = _pick(rows_w, [s for s in (16, 32, 64, 128, 256, 512, 1024) if rows_w // s <= steps] + [rows_w])
        slab_of = lambda *g, n=rows_w // slab: jnp.minimum(step_of(*g), n - 1)
        in_specs.append(pl.BlockSpec((None, slab, cols_w), lambda *g, f=slab_of, l=layer: (l, f(*g), 0)))
        args.append(stack)
        out_specs.append(pl.BlockSpec((slab, cols_w), lambda *g, f=slab_of: (f(*g), 0)))
        out_shape.append(jax.ShapeDtypeStruct((rows_w, cols_w), BF16))
    return in_specs, args, out_specs, out_shape


def _layer_norm(v, g, b):
    mu = jnp.mean(v, axis=-1, keepdims=True)
    d = v - mu
    var = jnp.mean(d * d, axis=-1, keepdims=True)
    return d * lax.rsqrt(var + LN_EPS) * g + b


def _deepnorm_epilogue(acc_ref, res_ref, gate_ref, lg_ref, lb_ref, h_ref, alpha, u_ref=None, sh_ref=None, sc_ref=None,
                       rows=None):
    r0, r1 = rows if rows is not None else (0, acc_ref.shape[0])
    ch = min(r1 - r0, EPILOGUE_ROWS)
    gate, lg, lb = gate_ref[...], lg_ref[...], lb_ref[...]
    if u_ref is not None:
        sh, sc1 = sh_ref[...], 1.0 + sc_ref[...]
    for k in range((r1 - r0) // ch):
        rs = slice(r0 + k * ch, r0 + (k + 1) * ch)
        hn = _layer_norm(alpha * res_ref[rs, :] + gate * acc_ref[rs, :], lg, lb)
        h_ref[rs, :] = hn
        if u_ref is not None:
            u_ref[rs, :] = (hn * sc1 + sh).astype(BF16)


def _mods_kernel(cs_ref, w_ref, b_ref, o_ref):
    s = cs_ref[...]
    s = (s * jax.nn.sigmoid(s)).astype(BF16)
    o_ref[...] = jnp.dot(s, w_ref[...].astype(BF16), preferred_element_type=F32) + b_ref[...]


def _mods_call(cs, mod_w, mod_b):
    depth, d, n = mod_w.shape
    mr = cs.shape[0]
    tn = _pick(n, (2048, 1024, 512, 256, 128))
    return pl.pallas_call(
        _mods_kernel,
        grid=(depth, n // tn),
        in_specs=[
            pl.BlockSpec((mr, d), lambda l, j: (0, 0)),
            pl.BlockSpec((None, d, tn), lambda l, j: (l, 0, j)),
            pl.BlockSpec((None, 1, tn), lambda l, j: (l, 0, j)),
        ],
        out_specs=pl.BlockSpec((None, mr, tn), lambda l, j: (l, 0, j)),
        out_shape=jax.ShapeDtypeStruct((depth, mr, n), F32),
        compiler_params=_params("arbitrary", "arbitrary"),
        name="mods",
    )(cs, mod_w, mod_b.reshape(depth, 1, n))


def _proj_ln_kernel(a_ref, w_ref, res_ref, gate_ref, sh_ref, sc_ref, lg_ref, lb_ref, h_ref, u_ref, acc, *, alpha):
    tm = a_ref.shape[0]
    halves = 2 if tm % (2 * EPILOGUE_ROWS) == 0 else 1
    for hh in range(halves):
        rows = (hh * tm // halves, (hh + 1) * tm // halves)
        acc[rows[0]:rows[1], :] = jnp.dot(a_ref[rows[0]:rows[1], :], w_ref[...], preferred_element_type=F32)
        _deepnorm_epilogue(acc, res_ref, gate_ref, lg_ref, lb_ref, h_ref, alpha, u_ref, sh_ref, sc_ref, rows=rows)


def _proj_ln_call(a, w, res, rows, mods5, layer, ln_g, ln_b, row_fn, tm, alpha, name):
    r, k = rows, a.shape[1]
    d = w.shape[1]
    vec = pl.BlockSpec((1, d), lambda i: (0, 0))
    return pl.pallas_call(
        functools.partial(_proj_ln_kernel, alpha=alpha),
        grid=(r // tm,),
        in_specs=[
            pl.BlockSpec((tm, k), lambda i: (i, 0)),
            pl.BlockSpec((k, d), lambda i: (0, 0), pipeline_mode=pl.Buffered(1)),
            pl.BlockSpec((tm, d), lambda i: (i, 0)),
            _mod_spec(layer, 2, row_fn)(d),
            _mod_spec(layer, 3, row_fn)(d),
            _mod_spec(layer, 4, row_fn)(d),
            vec,
            vec,
        ],
        out_specs=[pl.BlockSpec((tm, d), lambda i: (i, 0)), pl.BlockSpec((tm, d), lambda i: (i, 0))],
        out_shape=[jax.ShapeDtypeStruct((r, d), F32), jax.ShapeDtypeStruct((r, d), BF16)],
        scratch_shapes=[pltpu.VMEM((tm, d), F32)],
        compiler_params=_params("arbitrary"),
        name=name,
    )(a, w, res, mods5, mods5, mods5, ln_g.reshape(1, d), ln_b.reshape(1, d))


def _ffn_kernel(*refs, alpha, with_u, n_m, n_epi, n_cast):
    if with_u:
        u_ref, wg_ref, wu_ref, wd_ref, res_ref, gate_ref, sh_ref, sc_ref, lg_ref, lb_ref = refs[:10]
        cast_in, (h_ref, un_ref) = refs[10:10 + n_cast], refs[10 + n_cast:12 + n_cast]
        cast_out, (acc, done) = refs[12 + n_cast:12 + 2 * n_cast], refs[12 + 2 * n_cast:]
    else:
        u_ref, wg_ref, wu_ref, wd_ref, res_ref, gate_ref, lg_ref, lb_ref, h_ref, acc, done = refs
        un_ref = sh_ref = sc_ref = None
        cast_in = cast_out = ()
    i, j = pl.program_id(0), pl.program_id(1)
    ch = res_ref.shape[0]

    def matmuls(first):
        for src, dst in zip(cast_in, cast_out):
            dst[...] = src[...].astype(BF16)
        u = u_ref[...]
        g = jnp.dot(u, wg_ref[...], preferred_element_type=F32)
        up = jnp.dot(u, wu_ref[...], preferred_element_type=F32)
        hid = (g * jax.nn.sigmoid(g) * up).astype(BF16)
        o = jnp.dot(hid, wd_ref[...], preferred_element_type=F32)
        if first:
            acc[...] = o
        else:
            acc[...] += o

    def epilogue():
        rs = pl.ds(pl.multiple_of(j * ch, ch), ch)
        hn = _layer_norm(alpha * res_ref[...] + gate_ref[...] * done[rs, :], lg_ref[...], lb_ref[...])
        h_ref[...] = hn
        if with_u:
            un_ref[...] = (hn * (1.0 + sc_ref[...]) + sh_ref[...]).astype(BF16)

    land = jnp.logical_and
    live = i < n_m
    epi = land(i > 0, j < n_epi)

    @pl.when(land(jnp.logical_not(live), j == 0))
    def _():
        done[...] = acc[...]

    @pl.when(land(land(live, epi), j == 0))
    def _():
        done[...] = acc[...]
        matmuls(True)
        epilogue()

    @pl.when(land(land(live, epi), j > 0))
    def _():
        matmuls(False)
        epilogue()

    @pl.when(land(land(live, jnp.logical_not(epi)), j == 0))
    def _():
        matmuls(True)

    @pl.when(land(land(live, jnp.logical_not(epi)), j > 0))
    def _():
        matmuls(False)

    @pl.when(land(jnp.logical_not(live), epi))
    def _():
        epilogue()


def _ffn_call(u, wts, res, rows, mods5, layer, next_wts, ln_g, ln_b, row_fn, tm, alpha):
    wg, wu, wd = wts
    r, d = rows, u.shape[1]
    hid = wg.shape[1]
    th = _pick(hid, (512, 256, 128))
    n_m, n_h = r // tm, hid // th
    ch = min(tm, EPILOGUE_ROWS)
    n_epi = tm // ch
    assert n_epi <= n_h
    with_u = next_wts is not None
    prev = lambda i: jnp.maximum(i - 1, 0)
    row2 = lambda i, j: row_fn(prev(i))
    jw = lambda i, j: jnp.where(i < n_m, j, n_h - 1)
    vec = pl.BlockSpec((1, d), lambda i, j: (0, 0))
    chunk = pl.BlockSpec((ch, d), lambda i, j: (prev(i) * n_epi + jnp.where(i == 0, 0, jnp.minimum(j, n_epi - 1)), 0))
    in_specs = [
        pl.BlockSpec((tm, d), lambda i, j: (jnp.minimum(i, n_m - 1), 0)),
        pl.BlockSpec((d, th), lambda i, j: (0, jw(i, j))),
        pl.BlockSpec((d, th), lambda i, j: (0, jw(i, j))),
        pl.BlockSpec((th, d), lambda i, j: (jw(i, j), 0)),
        chunk,
        _mod_spec(layer, 5, row2)(d),
    ]
    args = [u, wg, wu, wd, res, mods5]
    if with_u:
        in_specs += [_mod_spec(layer + 1, 0, row2)(d), _mod_spec(layer + 1, 1, row2)(d)]
        args += [mods5, mods5]
    in_specs += [vec, vec]
    args += [ln_g.reshape(1, d), ln_b.reshape(1, d)]
    out_specs = [chunk]
    out_shape = [jax.ShapeDtypeStruct((r, d), F32)]
    if with_u:
        out_specs.append(chunk)
        out_shape.append(jax.ShapeDtypeStruct((r, d), BF16))
        c_in, c_args, c_out, c_shape = _cast_side_job(next_wts, n_m * n_h, lambda i, j: i * n_h + j)
        in_specs += c_in
        args += c_args
        out_specs += c_out
        out_shape += c_shape
    n_cast = len(next_wts) if with_u else 0
    out = pl.pallas_call(
        functools.partial(_ffn_kernel, alpha=alpha, with_u=with_u, n_m=n_m, n_epi=n_epi, n_cast=n_cast),
        grid=(n_m + 1, n_h),
        in_specs=in_specs,
        out_specs=out_specs,
        out_shape=out_shape,
        scratch_shapes=[pltpu.VMEM((tm, d), F32), pltpu.VMEM((tm, d), F32)],
        compiler_params=_params("arbitrary", "arbitrary"),
        name="ffn",
    )(*args)
    return (out[0], out[1], tuple(out[2:])) if with_u else (out[0], None, None)


def _gelu_tanh(x):
    return x * (0.5 * (1.0 + jnp.tanh(0.7978845608028654 * (x + 0.044715 * (x * x * x)))))


def _col_groups(n, width=2 * V7X_MXU_WIDTH):
    return [(c, min(c + width, n)) for c in range(0, n, width)]


def _win_kernel(*refs, n_lat, n_cast):
    if n_lat is None:
        u_ref, w_ref, y_ref, z_ref = refs
        u = u_ref[...]
    else:
        x_ref, c_ref, sh_ref, sc_ref, w_ref = refs[:5]
        cast_in, (y_ref, z_ref, h_ref) = refs[5:5 + n_cast], refs[5 + n_cast:8 + n_cast]
        cast_out, u_s = refs[8 + n_cast:8 + 2 * n_cast], refs[8 + 2 * n_cast]

        def emit(src_ref):
            v = src_ref[...]
            h_ref[...] = v
            u_s[...] = (v * (1.0 + sc_ref[...]) + sh_ref[...]).astype(BF16)

        pl.when(pl.program_id(0) < n_lat)(lambda: emit(x_ref))
        pl.when(pl.program_id(0) >= n_lat)(lambda: emit(c_ref))
        u = u_s[...]
        for src, dst in zip(cast_in, cast_out):
            dst[...] = src[...].astype(BF16)
    dr = y_ref.shape[1]
    for c0, c1 in _col_groups(dr):
        y_ref[:, c0:c1] = _gelu_tanh(jnp.dot(u, w_ref[:, c0:c1], preferred_element_type=F32)).astype(BF16)
    for c0, c1 in _col_groups(dr):
        z_ref[:, c0:c1] = jnp.dot(u, w_ref[:, dr + c0:dr + c1], preferred_element_type=F32)


def _win_call(u, w_in, tm, first=None):
    d, n = w_in.shape
    dr = n // 2
    wspec = pl.BlockSpec((d, n), lambda i: (0, 0), pipeline_mode=pl.Buffered(1))
    tile = lambda width: pl.BlockSpec((tm, width), lambda i: (i, 0))
    if first is None:
        r, n_lat, n_cast = u.shape[0], None, 0
        in_specs, args = [tile(d), wspec], [u, w_in]
        extra_out, extra_shape, scratch = [], [], []
    else:
        x2, c2, mods5, row_fn, cast_items = first
        n_lat, n_cast = x2.shape[0] // tm, len(cast_items)
        r = x2.shape[0] + c2.shape[0]
        c_in, c_args, c_out, c_shape = _cast_side_job(cast_items, r // tm, lambda i: i)
        in_specs = [
            pl.BlockSpec((tm, d), lambda i: (jnp.minimum(i, n_lat - 1), 0)),
            pl.BlockSpec((tm, d), lambda i: (jnp.maximum(i - n_lat, 0), 0)),
            _mod_spec(0, 0, row_fn)(d),
            _mod_spec(0, 1, row_fn)(d),
            wspec,
        ] + c_in
        args = [x2, c2, mods5, mods5, w_in] + c_args
        extra_out, extra_shape = [tile(d)] + c_out, [jax.ShapeDtypeStruct((r, d), F32)] + c_shape
        scratch = [pltpu.VMEM((tm, d), BF16)]
    out = pl.pallas_call(
        functools.partial(_win_kernel, n_lat=n_lat, n_cast=n_cast),
        grid=(r // tm,),
        in_specs=in_specs,
        out_specs=[tile(dr), tile(dr)] + extra_out,
        out_shape=[jax.ShapeDtypeStruct((r, dr), BF16), jax.ShapeDtypeStruct((r, dr), F32)] + extra_shape,
        scratch_shapes=scratch,
        compiler_params=_params("arbitrary"),
        name="rec_in",
    )(*args)
    return out if first is None else (out[0], out[1], out[2], tuple(out[3:]))


def _gate_window_plan(dr, bw):
    tn = V7X_MXU_WIDTH
    spans = []
    for n in range(dr // tn):
        c0 = n * tn
        lo = (c0 // bw) * bw
        hi = ((c0 + tn - 1) // bw + 1) * bw
        spans.append(((lo // V7X_LANES) * V7X_LANES, -(-hi // V7X_LANES) * V7X_LANES))
    kw = max(h - l for l, h in spans)
    return kw, tuple(min(l, dr - kw) for l, _ in spans)


def _gate_windows(w, kw, k0s):
    lead, bw = w.shape[:-3], w.shape[-1]
    tn = V7X_MXU_WIDTH
    w = w.astype(BF16)
    zero = jnp.zeros((), BF16)
    keep = [(0, 0, 0)] * (len(lead) + 1)
    tiles = []
    for n, k0 in enumerate(k0s):
        c0 = n * tn
        strips, filled = [], 0
        for h in range(c0 // bw, (c0 + tn - 1) // bw + 1):
            wr, wc = h * bw - k0, h * bw - c0
            sr0, sr1 = max(0, -wr), bw - max(0, wr + bw - kw)
            sc0, sc1 = max(0, -wc), bw - max(0, wc + bw - tn)
            if max(wr, 0) > filled:
                strips.append(jnp.zeros((*lead, max(wr, 0) - filled, tn), BF16))
            col = max(wc, 0)
            strips.append(lax.pad(w[..., h, sr0:sr1, sc0:sc1], zero, keep + [(col, tn - col - (sc1 - sc0), 0)]))
            filled = max(wr, 0) + sr1 - sr0
        if filled < kw:
            strips.append(jnp.zeros((*lead, kw - filled, tn), BF16))
        tiles.append(jnp.concatenate(strips, axis=-2))
    return jnp.stack(tiles, axis=-3)


def _gates_scan(zb, z_ref, wa_ref, wx_ref, ba_ref, bx_ref, lam_ref, a_s, b_s, dst, carry, kw, k0s, reverse):
    tt = zb.shape[0]
    tn = V7X_MXU_WIDTH
    sub = V7X_SUBLANES
    ng = tt // sub
    row = lax.broadcasted_iota(jnp.int32, (sub, tn), 0)
    entry = sub - 1 if reverse else 0
    for n, k0 in enumerate(k0s):
        cs = slice(n * tn, (n + 1) * tn)
        zw = zb[:, k0:k0 + kw]
        ta = jnp.tanh(jnp.dot(zw, wa_ref[n], preferred_element_type=F32) + ba_ref[:, cs])
        tx = jnp.tanh(jnp.dot(zw, wx_ref[n], preferred_element_type=F32) + bx_ref[:, cs])
        nl = -lam_ref[:, cs]
        half = (0.5 * RG_C) * (jnp.maximum(nl, 0.0) + jnp.log1p(jnp.exp(-jnp.abs(nl))))
        q = ta * half + half
        a = jnp.exp(-q)
        a_s[:, cs] = a
        x = jnp.tanh(q) * (1.0 + a * a)
        root = jnp.where(x > 0.0, x * lax.rsqrt(x), 0.0)
        b_s[:, cs] = root * (0.5 * tx + 0.5) * z_ref[:, cs]

        h = carry[:, cs]
        for g in (range(ng - 1, -1, -1) if reverse else range(ng)):
            rs = slice(g * sub, (g + 1) * sub)
            a8 = a_s[rs, cs]
            b8 = b_s[rs, cs]
            b8 = b8 + jnp.where(row == entry, a8 * h, 0.0)
            a8 = jnp.where(row == entry, 0.0, a8)
            for s in (1, 2, 4):
                shift = sub - s if reverse else s
                b8 = a8 * pltpu.roll(b8, shift, 0) + b8
                if s < 4:
                    a8 = a8 * pltpu.roll(a8, shift, 0)
            dst[rs, cs] = b8
            h = jnp.broadcast_to(b8[0:1, :] if reverse else b8[sub - 1:sub, :], (sub, tn))
        carry[:, cs] = h


def _scan_fwd_kernel(zc_ref, zp_ref, zn_ref, cw_ref, cb_ref, wa_ref, wx_ref, ba_ref, bx_ref, lam_ref,
                     hf_ref, z_ref, xpad, zb, a_s, b_s, carry, *, kw, k0s, ns):
    tt = zc_ref.shape[0]
    tn = V7X_MXU_WIDTH
    sub = V7X_SUBLANES
    t = pl.program_id(1)

    @pl.when(t == 0)
    def _():
        carry[...] = jnp.zeros_like(carry)

    first = t <= 1
    last = jnp.logical_or(t == 0, t == ns)
    xpad[0:sub, :] = jnp.where(first, 0.0, zp_ref[...])
    xpad[sub:sub + tt, :] = zc_ref[...]
    xpad[sub + tt:2 * sub + tt, :] = jnp.where(last, 0.0, zn_ref[...])
    for n in range(len(k0s)):
        cs = slice(n * tn, (n + 1) * tn)
        xa = xpad[:, cs]
        acc = None
        for j in range(CONV_W):
            shift = (CONV_LEFT - j) % xa.shape[0]
            xs = pltpu.roll(xa, shift, 0) if shift else xa
            term = xs[sub:sub + tt, :] * cw_ref[j:j + 1, cs]
            acc = term if acc is None else acc + term
        z = acc + cb_ref[:, cs]
        z_ref[:, cs] = z
        zb[:, cs] = z.astype(BF16)

    _gates_scan(zb, z_ref, wa_ref, wx_ref, ba_ref, bx_ref, lam_ref, a_s, b_s, hf_ref, carry, kw, k0s, reverse=False)


def _scan_bwd_kernel(z_ref, wa_ref, wx_ref, ba_ref, bx_ref, lam_ref, hf_ref, y_ref, o_ref, zb, a_s, b_s, h_s, carry,
                     *, kw, k0s):
    tn = V7X_MXU_WIDTH

    @pl.when(pl.program_id(1) == 0)
    def _():
        carry[...] = jnp.zeros_like(carry)

    zb[...] = z_ref[...].astype(BF16)
    _gates_scan(zb, z_ref, wa_ref, wx_ref, ba_ref, bx_ref, lam_ref, a_s, b_s, h_s, carry, kw, k0s, reverse=True)
    for n in range(len(k0s)):
        cs = slice(n * tn, (n + 1) * tn)
        o_ref[:, cs] = (y_ref[:, cs].astype(F32) * (hf_ref[:, cs] + h_s[:, cs])).astype(BF16)


def _scan_calls(zpre, y, conv_w, conv_b, wa, wx, ba, bx, lam, kw, k0s, batch, seq, ctx_len):
    r, dr = zpre.shape
    tt = ctx_len
    ns = seq // tt
    nx = batch * seq // tt
    g8 = tt // V7X_SUBLANES
    nblk8 = r // V7X_SUBLANES
    nt = len(k0s)

    def cur(reverse):
        def f(b, t):
            st = (ns - t) if reverse else (t - 1)
            return jnp.where(t == 0, nx + b, b * ns + st)
        return f

    def tile(reverse):
        return pl.BlockSpec((tt, dr), lambda b, t: (cur(reverse)(b, t), 0))

    fw = cur(False)
    halo_p = pl.BlockSpec((V7X_SUBLANES, dr), lambda b, t: (jnp.maximum(fw(b, t) * g8 - 1, 0), 0))
    halo_n = pl.BlockSpec((V7X_SUBLANES, dr), lambda b, t: (jnp.minimum(fw(b, t) * g8 + g8, nblk8 - 1), 0))
    vec = lambda d: pl.BlockSpec((None, 1, dr), lambda b, t: (d, 0, 0))
    wspec = lambda d: pl.BlockSpec((None, nt, kw, V7X_MXU_WIDTH), lambda b, t: (d, 0, 0, 0),
                                   pipeline_mode=pl.Buffered(1))
    ba, bx, lam = (v.reshape(2, 1, dr) for v in (ba, bx, lam))
    gate_specs = lambda d: [wspec(d), wspec(d), vec(d), vec(d), vec(d)]
    gate_args = [wa, wx, ba, bx, lam]
    work = [
        pltpu.VMEM((tt, dr), BF16),
        pltpu.VMEM((tt, dr), F32),
        pltpu.VMEM((tt, dr), F32),
    ]
    state = pltpu.VMEM((V7X_SUBLANES, dr), F32)
    hf, z = pl.pallas_call(
        functools.partial(_scan_fwd_kernel, kw=kw, k0s=k0s, ns=ns),
        grid=(batch, ns + 1),
        in_specs=[tile(False), halo_p, halo_n, pl.BlockSpec((CONV_W, dr), lambda b, t: (0, 0)),
                  pl.BlockSpec((1, dr), lambda b, t: (0, 0))] + gate_specs(0),
        out_specs=[tile(False), tile(False)],
        out_shape=[jax.ShapeDtypeStruct((r, dr), F32), jax.ShapeDtypeStruct((r, dr), F32)],
        scratch_shapes=[pltpu.VMEM((tt + 2 * V7X_SUBLANES, dr), F32)] + work + [state],
        compiler_params=_params("arbitrary", "arbitrary"),
        name="rec_scan_fwd",
    )(zpre, zpre, zpre, conv_w, conv_b.reshape(1, dr), *gate_args)
    return pl.pallas_call(
        functools.partial(_scan_bwd_kernel, kw=kw, k0s=k0s),
        grid=(batch, ns + 1),
        in_specs=[tile(True)] + gate_specs(1) + [tile(True), tile(True)],
        out_specs=tile(True),
        out_shape=jax.ShapeDtypeStruct((r, dr), BF16),
        scratch_shapes=work + [pltpu.VMEM((tt, dr), F32), state],
        compiler_params=_params("arbitrary", "arbitrary"),
        name="rec_scan_bwd",
    )(z, *gate_args, hf, y)


def _rope_tables(seq, pad_rows):
    rows = seq // GRID_W
    row = jnp.repeat(jnp.arange(rows, dtype=F32), GRID_W)
    col = jnp.tile(jnp.arange(GRID_W, dtype=F32), rows)
    axis_dim = HEAD_DIM // 2
    inv_freq = ROPE_BASE ** (-jnp.arange(0, axis_dim, 2, dtype=F32) / axis_dim)
    ang = jnp.concatenate([row[:, None] * inv_freq, col[:, None] * inv_freq], axis=-1)
    cos, sin = jnp.cos(ang), jnp.sin(ang)
    cos2 = jnp.concatenate([cos, cos], axis=-1)
    sin2 = jnp.concatenate([-sin, sin], axis=-1)
    cos2 = jnp.concatenate([cos2, jnp.ones((pad_rows, HEAD_DIM), F32)], axis=0)
    sin2 = jnp.concatenate([sin2, jnp.zeros((pad_rows, HEAD_DIM), F32)], axis=0)
    return cos2, sin2


def _qkv_kernel(u_ref, w_ref, cos_ref, sin_ref, o_ref, *, nq, nkv, scale):
    u = u_ref[...]
    c = cos_ref[...]
    s = sin_ref[...]
    for c0, c1 in _col_groups(o_ref.shape[1]):
        p = jnp.dot(u, w_ref[:, c0:c1], preferred_element_type=F32)
        for h0 in range(c0, c1, HEAD_DIM):
            t = p[:, h0 - c0:h0 - c0 + HEAD_DIM]
            if h0 < nq + nkv:
                t = t * c + pltpu.roll(t, HEAD_DIM // 2, 1) * s
            if h0 < nq:
                t = t * scale
            o_ref[:, h0:h0 + HEAD_DIM] = t.astype(BF16)


def _qkv_call(u, w, cos2, sin2, seq, rows_x, tm):
    r, d = u.shape
    n = w.shape[1]
    nkv = N_KV_HEADS * HEAD_DIM
    n_seq_tiles = seq // tm
    tab = pl.BlockSpec((tm, HEAD_DIM), lambda i: (jnp.where(i < rows_x // tm, i % n_seq_tiles, n_seq_tiles), 0))
    return pl.pallas_call(
        functools.partial(_qkv_kernel, nq=n - 2 * nkv, nkv=nkv, scale=HEAD_DIM ** -0.5 * LOG2_E),
        grid=(r // tm,),
        in_specs=[pl.BlockSpec((tm, d), lambda i: (i, 0)),
                  pl.BlockSpec((d, n), lambda i: (0, 0), pipeline_mode=pl.Buffered(1)), tab, tab],
        out_specs=pl.BlockSpec((tm, n), lambda i: (i, 0)),
        out_shape=jax.ShapeDtypeStruct((r, n), BF16),
        compiler_params=_params("arbitrary"),
        name="att_qkv",
    )(u, w, cos2, sin2)


def _attn_masks(gq, ctx_len):
    blk = WINDOW
    qi = lax.broadcasted_iota(jnp.int32, (gq * blk, blk), 0) % blk
    ki = lax.broadcasted_iota(jnp.int32, (gq * blk, blk), 1)
    on = jnp.zeros((gq * blk, blk), F32)
    off = jnp.full((gq * blk, blk), NEG_INF, F32)
    prev = jnp.where(ki >= qi, 0.0, NEG_INF).astype(F32)
    nxt = jnp.where(ki <= qi, 0.0, NEG_INF).astype(F32)
    ctx_on = jnp.zeros((gq * blk, ctx_len), F32)
    variants = ((off, on, nxt), (prev, on, nxt), (prev, on, off), (off, off, off))
    return jnp.stack([jnp.concatenate(v + (ctx_on,), axis=1) for v in variants])


def _attn_kernel(sink_ref, q_ref, kp_ref, kc_ref, kn_ref, vp_ref, vc_ref, vn_ref, kx_ref, vx_ref, bias, o_ref,
                 k_all, v_all, *, g):
    blk = WINDOW
    band = 3 * blk
    for dst, srcs in ((k_all, (kp_ref, kc_ref, kn_ref)), (v_all, (vp_ref, vc_ref, vn_ref))):
        for o, src in enumerate(srcs):
            dst[o * blk:(o + 1) * blk, :] = src[...]
    k_all[band:, :] = kx_ref[...]
    v_all[band:, :] = vx_ref[...]
    nt_dims = (((1,), (1,)), ((), ()))
    gq = bias.shape[0] // blk
    for h0 in range(0, N_KV_HEADS * g, gq):
        kh = h0 // g
        ks = slice(kh * HEAD_DIM, (kh + 1) * HEAD_DIM)
        heads = range(h0, h0 + gq)
        qs = jnp.concatenate([q_ref[:, h * HEAD_DIM:(h + 1) * HEAD_DIM] for h in heads], axis=0)
        sk = jnp.concatenate([jnp.full((blk, 1), sink_ref[h] * LOG2_E, F32) for h in heads], axis=0)
        s = lax.dot_general(qs, k_all[:, ks], nt_dims, preferred_element_type=F32) + bias[...]
        m = jnp.maximum(jnp.max(s, -1, keepdims=True), sk)
        p = jnp.exp2(s - m)
        denom = jnp.exp2(sk - m) + jnp.sum(p, -1, keepdims=True)
        o = jnp.dot(p.astype(BF16), v_all[:, ks], preferred_element_type=F32) / denom
        for gi, h in enumerate(heads):
            o_ref[:, h * HEAD_DIM:(h + 1) * HEAD_DIM] = o[gi * blk:(gi + 1) * blk].astype(BF16)


def _attn_call(qkv, sink, batch, seq, ctx_len, ctx_queries):
    r, n = qkv.shape
    nkv = N_KV_HEADS * HEAD_DIM
    d = n - 2 * nkv
    g = d // HEAD_DIM // N_KV_HEADS
    gq = g
    blk = WINDOW
    nb = seq // blk
    ncb = ctx_len // blk
    kcol = d // nkv
    x0 = batch * seq

    def qrow(b, j):
        return jnp.where(j < nb, b * nb + j, x0 // blk + b * ncb + (j - nb))

    def krow(b, j, off):
        return b * nb + jnp.clip(jnp.minimum(j, nb - 1) + off, 0, nb - 1)

    assert nb >= 2
    qspec = pl.BlockSpec((blk, d), lambda b, j: (qrow(b, j), 0))
    band = lambda off, col: pl.BlockSpec((blk, nkv), lambda b, j: (krow(b, j, off), col))
    ctxs = lambda col: pl.BlockSpec((ctx_len, nkv), lambda b, j: (x0 // ctx_len + b, col))
    nkeys = 3 * blk + ctx_len
    mask = pl.BlockSpec((None, gq * blk, nkeys),
                        lambda b, j: (jnp.where(j >= nb, 3, jnp.where(j == 0, 0, jnp.where(j == nb - 1, 2, 1))), 0, 0))
    return pl.pallas_call(
        functools.partial(_attn_kernel, g=g),
        grid=(batch, nb + ncb if ctx_queries else nb),
        in_specs=[
            pl.BlockSpec(memory_space=pltpu.SMEM),
            qspec,
            band(-1, kcol), band(0, kcol), band(1, kcol),
            band(-1, kcol + 1), band(0, kcol + 1), band(1, kcol + 1),
            ctxs(kcol), ctxs(kcol + 1),
            mask,
        ],
        out_specs=qspec,
        out_shape=jax.ShapeDtypeStruct((r if ctx_queries else x0, d), BF16),
        scratch_shapes=[
            pltpu.VMEM((nkeys, nkv), BF16),
            pltpu.VMEM((nkeys, nkv), BF16),
        ],
        compiler_params=_params("arbitrary", "arbitrary"),
        name="att_core",
    )(sink, qkv, qkv, qkv, qkv, qkv, qkv, qkv, qkv, qkv, _attn_masks(gq, ctx_len))


def kernel(x, c, ctx, c_ctx, mod_w, mod_b, ln_mix_g, ln_mix_b, ln_ffn_g, ln_ffn_b, ffn_w_gate, ffn_w_up, ffn_w_down,
           rec_w_in, rec_conv_w, rec_conv_b, rec_gate_a_w, rec_gate_a_b, rec_gate_x_w, rec_gate_x_b, rec_lambda,
           rec_w_out, att_w_qkv, att_sink, att_w_o):
    batch, seq, d = x.shape
    ctx_len = ctx.shape[1]
    depth = mod_w.shape[0]
    rows_x, rows_c = batch * seq, batch * ctx_len
    assert seq % ctx_len == 0 and ctx_len % WINDOW == 0 and seq % GRID_W == 0
    assert d % (N_KV_HEADS * HEAD_DIM) == 0
    alpha = (2.0 * depth) ** 0.25

    tm = _pick(seq, (512, 256, 128))
    while rows_c % tm:
        tm //= 2
    tm_big = 2 * tm if (seq % (2 * tm) == 0 and rows_c % (2 * tm) == 0) else tm

    def row_fn_for(t):
        return lambda i: jnp.minimum((i * t) // seq, batch)

    row_fn, row_fn_big = row_fn_for(tm), row_fn_for(tm_big)

    mr = -(-(batch + 1) // V7X_SUBLANES) * V7X_SUBLANES
    cs = jnp.zeros((mr, d), F32).at[:batch].set(c).at[batch].set(c_ctx)
    mods5 = _mods_call(cs, mod_w, mod_b).reshape(depth, mr, N_MOD, 1, d)

    cos2, sin2 = _rope_tables(seq, tm_big)
    dr = rec_w_out.shape[1]
    kw, k0s = _gate_window_plan(dr, dr // RNN_BLOCKS)
    wa_all = _gate_windows(0.5 * rec_gate_a_w, kw, k0s)
    wx_all = _gate_windows(0.5 * rec_gate_x_w, kw, k0s)
    ba_all, bx_all = 0.5 * rec_gate_a_b, 0.5 * rec_gate_x_b

    def mixer_f32(layer):
        stacks = (rec_w_in, rec_w_out) if layer % N_MIXERS == 0 else (att_w_qkv, att_w_o)
        return [(s, layer // N_MIXERS) for s in stacks]

    def ffn_f32(layer):
        return [(s, layer) for s in (ffn_w_gate, ffn_w_up, ffn_w_down)]

    mix_wts = (rec_w_in[0].astype(BF16), rec_w_out[0].astype(BF16))
    ffn_wts = None
    h = u = None
    for i in range(depth):
        j = i // N_MIXERS
        last = i == depth - 1
        rows = rows_x if last else rows_x + rows_c
        if i % N_MIXERS == 0:
            if i == 0:
                tm0 = max(tm // 2, WINDOW)
                first = (x.reshape(rows_x, d), ctx.reshape(rows_c, d), mods5, row_fn_for(tm0), ffn_f32(0))
                y, zpre, h, ffn_wts = _win_call(None, mix_wts[0], tm0, first)
            else:
                y, zpre = _win_call(u, mix_wts[0], tm)
            yh = _scan_calls(zpre, y, rec_conv_w[j], rec_conv_b[j], wa_all[j], wx_all[j], ba_all[j], bx_all[j],
                             rec_lambda[j], kw, k0s, batch, seq, ctx_len)
            h, u = _proj_ln_call(yh, mix_wts[1], h, rows, mods5, i, ln_mix_g[i], ln_mix_b[i], row_fn, tm, alpha,
                                 "rec_out")
        else:
            qkv = _qkv_call(u, mix_wts[0], cos2, sin2, seq, rows_x, tm_big)
            ao = _attn_call(qkv, att_sink[j], batch, seq, ctx_len, ctx_queries=not last)
            h, u = _proj_ln_call(ao, mix_wts[1], h, rows, mods5, i, ln_mix_g[i], ln_mix_b[i], row_fn, tm, alpha,
                                 "att_out")
        next_wts = None if last else ffn_f32(i + 1) + mixer_f32(i + 1)
        h, u, casts = _ffn_call(u, ffn_wts, h, rows, mods5, i, next_wts, ln_ffn_g[i], ln_ffn_b[i], row_fn_big, tm_big,
                                alpha)
        if not last:
            ffn_wts, mix_wts = casts[:3], casts[3:]
    return h.reshape(batch, seq, d)
```

```python
import functools

import jax
import jax.numpy as jnp
from jax import lax
from jax.experimental import pallas as pl
from jax.experimental.pallas import tpu as pltpu

HEAD_DIM = 128
N_KV_HEADS = 4
WINDOW = 128
GRID_W = 64
ROPE_BASE = 10000.0
RNN_BLOCKS = 16
CONV_W = 4
CONV_LEFT = 2
RG_C = 8.0
LN_EPS = 1e-5
NEG_INF = -1e30
N_MIXERS = 2
N_MOD = 6
LOG2_E = 1.4426950408889634

V7X_LANES = 128
V7X_SUBLANES = 8
V7X_MXU_WIDTH = 256
V7X_VMEM_BYTES = 64 * 1024 * 1024
VMEM_LIMIT_BYTES = V7X_VMEM_BYTES - 8 * 1024 * 1024

F32 = jnp.float32
BF16 = jnp.bfloat16
EPILOGUE_ROWS = 128


def _pick(n, cands):
    for c in cands:
        if n % c == 0:
            return c
    raise ValueError(f"no tile in {cands} divides {n}")


def _params(*sem):
    return pltpu.CompilerParams(dimension_semantics=sem, vmem_limit_bytes=VMEM_LIMIT_BYTES)


def _mod_spec(layer, chunk, row_fn):
    return lambda d: pl.BlockSpec((None, None, None, 1, d), lambda *g: (layer, row_fn(*g), chunk, 0, 0))


def _cast_side_job(items, steps, step_of):
    in_specs, args, out_specs, out_shape = [], [], [], []
    for stack, layer in items:
        _, rows_w, cols_w = stack.shape
        slab = _pick(rows_w, [s for s in (16, 32, 64, 128, 256, 512, 1024) if rows_w // s <= steps] + [rows_w])
        slab_of = lambda *g, n=rows_w // slab: jnp.minimum(step_of(*g), n - 1)
        in_specs.append(pl.BlockSpec((None, slab, cols_w), lambda *g, f=slab_of, l=layer: (l, f(*g), 0)))
        args.append(stack)
        out_specs.append(pl.BlockSpec((slab, cols_w), lambda *g, f=slab_of: (f(*g), 0)))
        out_shape.append(jax.ShapeDtypeStruct((rows_w, cols_w), BF16))
    return in_specs, args, out_specs, out_shape


def _layer_norm(v, g, b):
    mu = jnp.mean(v, axis=-1, keepdims=True)
    d = v - mu
    var = jnp.mean(d * d, axis=-1, keepdims=True)
    return d * lax.rsqrt(var + LN_EPS) * g + b


def _deepnorm_epilogue(acc_ref, res_ref, gate_ref, lg_ref, lb_ref, h_ref, alpha, u_ref=None, sh_ref=None, sc_ref=None,
                       rows=None):
    r0, r1 = rows if rows is not None else (0, acc_ref.shape[0])
    ch = min(r1 - r0, EPILOGUE_ROWS)
    gate, lg, lb = gate_ref[...], lg_ref[...], lb_ref[...]
    if u_ref is not None:
        sh, sc1 = sh_ref[...], 1.0 + sc_ref[...]
    for k in range((r1 - r0) // ch):
        rs = slice(r0 + k * ch, r0 + (k + 1) * ch)
        hn = _layer_norm(alpha * res_ref[rs, :] + gate * acc_ref[rs, :], lg, lb)
        h_ref[rs, :] = hn
        if u_ref is not None:
            u_ref[rs, :] = (hn * sc1 + sh).astype(BF16)


def _mods_kernel(cs_ref, w_ref, b_ref, o_ref):
    s = cs_ref[...]
    s = (s * jax.nn.sigmoid(s)).astype(BF16)
    o_ref[...] = jnp.dot(s, w_ref[...].astype(BF16), preferred_element_type=F32) + b_ref[...]


def _mods_call(cs, mod_w, mod_b):
    depth, d, n = mod_w.shape
    mr = cs.shape[0]
    tn = _pick(n, (2048, 1024, 512, 256, 128))
    return pl.pallas_call(
        _mods_kernel,
        grid=(depth, n // tn),
        in_specs=[
            pl.BlockSpec((mr, d), lambda l, j: (0, 0)),
            pl.BlockSpec((None, d, tn), lambda l, j: (l, 0, j)),
            pl.BlockSpec((None, 1, tn), lambda l, j: (l, 0, j)),
        ],
        out_specs=pl.BlockSpec((None, mr, tn), lambda l, j: (l, 0, j)),
        out_shape=jax.ShapeDtypeStruct((depth, mr, n), F32),
        compiler_params=_params("arbitrary", "arbitrary"),
        name="mods",
    )(cs, mod_w, mod_b.reshape(depth, 1, n))


def _proj_ln_kernel(a_ref, w_ref, res_ref, gate_ref, sh_ref, sc_ref, lg_ref, lb_ref, h_ref, u_ref, acc, *, alpha):
    tm = a_ref.shape[0]
    halves = 2 if tm % (2 * EPILOGUE_ROWS) == 0 else 1
    for hh in range(halves):
        rows = (hh * tm // halves, (hh + 1) * tm // halves)
        acc[rows[0]:rows[1], :] = jnp.dot(a_ref[rows[0]:rows[1], :], w_ref[...], preferred_element_type=F32)
        _deepnorm_epilogue(acc, res_ref, gate_ref, lg_ref, lb_ref, h_ref, alpha, u_ref, sh_ref, sc_ref, rows=rows)


def _proj_ln_call(a, w, res, rows, mods5, layer, ln_g, ln_b, row_fn, tm, alpha, name):
    r, k = rows, a.shape[1]
    d = w.shape[1]
    vec = pl.BlockSpec((1, d), lambda i: (0, 0))
    return pl.pallas_call(
        functools.partial(_proj_ln_kernel, alpha=alpha),
        grid=(r // tm,),
        in_specs=[
            pl.BlockSpec((tm, k), lambda i: (i, 0)),
            pl.BlockSpec((k, d), lambda i: (0, 0), pipeline_mode=pl.Buffered(1)),
            pl.BlockSpec((tm, d), lambda i: (i, 0)),
            _mod_spec(layer, 2, row_fn)(d),
            _mod_spec(layer, 3, row_fn)(d),
            _mod_spec(layer, 4, row_fn)(d),
            vec,
            vec,
        ],
        out_specs=[pl.BlockSpec((tm, d), lambda i: (i, 0)), pl.BlockSpec((tm, d), lambda i: (i, 0))],
        out_shape=[jax.ShapeDtypeStruct((r, d), F32), jax.ShapeDtypeStruct((r, d), BF16)],
        scratch_shapes=[pltpu.VMEM((tm, d), F32)],
        compiler_params=_params("arbitrary"),
        name=name,
    )(a, w, res, mods5, mods5, mods5, ln_g.reshape(1, d), ln_b.reshape(1, d))


def _ffn_kernel(*refs, alpha, with_u, n_m, n_epi, n_cast):
    if with_u:
        u_ref, wg_ref, wu_ref, wd_ref, res_ref, gate_ref, sh_ref, sc_ref, lg_ref, lb_ref = refs[:10]
        cast_in, (h_ref, un_ref) = refs[10:10 + n_cast], refs[10 + n_cast:12 + n_cast]
        cast_out, (acc, done) = refs[12 + n_cast:12 + 2 * n_cast], refs[12 + 2 * n_cast:]
    else:
        u_ref, wg_ref, wu_ref, wd_ref, res_ref, gate_ref, lg_ref, lb_ref, h_ref, acc, done = refs
        un_ref = sh_ref = sc_ref = None
        cast_in = cast_out = ()
    i, j = pl.program_id(0), pl.program_id(1)
    ch = res_ref.shape[0]

    def matmuls(first):
        for src, dst in zip(cast_in, cast_out):
            dst[...] = src[...].astype(BF16)
        u = u_ref[...]
        g = jnp.dot(u, wg_ref[...], preferred_element_type=F32)
        up = jnp.dot(u, wu_ref[...], preferred_element_type=F32)
        hid = (g * jax.nn.sigmoid(g) * up).astype(BF16)
        o = jnp.dot(hid, wd_ref[...], preferred_element_type=F32)
        if first:
            acc[...] = o
        else:
            acc[...] += o

    def epilogue():
        rs = pl.ds(pl.multiple_of(j * ch, ch), ch)
        hn = _layer_norm(alpha * res_ref[...] + gate_ref[...] * done[rs, :], lg_ref[...], lb_ref[...])
        h_ref[...] = hn
        if with_u:
            un_ref[...] = (hn * (1.0 + sc_ref[...]) + sh_ref[...]).astype(BF16)

    land = jnp.logical_and
    live = i < n_m
    epi = land(i > 0, j < n_epi)

    @pl.when(land(jnp.logical_not(live), j == 0))
    def _():
        done[...] = acc[...]

    @pl.when(land(land(live, epi), j == 0))
    def _():
        done[...] = acc[...]
        matmuls(True)
        epilogue()

    @pl.when(land(land(live, epi), j > 0))
    def _():
        matmuls(False)
        epilogue()

    @pl.when(land(land(live, jnp.logical_not(epi)), j == 0))
    def _():
        matmuls(True)

    @pl.when(land(land(live, jnp.logical_not(epi)), j > 0))
    def _():
        matmuls(False)

    @pl.when(land(jnp.logical_not(live), epi))
    def _():
        epilogue()


def _ffn_call(u, wts, res, rows, mods5, layer, next_wts, ln_g, ln_b, row_fn, tm, alpha):
    wg, wu, wd = wts
    r, d = rows, u.shape[1]
    hid = wg.shape[1]
    th = _pick(hid, (512, 256, 128))
    n_m, n_h = r // tm, hid // th
    ch = min(tm, EPILOGUE_ROWS)
    n_epi = tm // ch
    assert n_epi <= n_h
    with_u = next_wts is not None
    prev = lambda i: jnp.maximum(i - 1, 0)
    row2 = lambda i, j: row_fn(prev(i))
    jw = lambda i, j: jnp.where(i < n_m, j, n_h - 1)
    vec = pl.BlockSpec((1, d), lambda i, j: (0, 0))
    chunk = pl.BlockSpec((ch, d), lambda i, j: (prev(i) * n_epi + jnp.where(i == 0, 0, jnp.minimum(j, n_epi - 1)), 0))
    in_specs = [
        pl.BlockSpec((tm, d), lambda i, j: (jnp.minimum(i, n_m - 1), 0)),
        pl.BlockSpec((d, th), lambda i, j: (0, jw(i, j))),
        pl.BlockSpec((d, th), lambda i, j: (0, jw(i, j))),
        pl.BlockSpec((th, d), lambda i, j: (jw(i, j), 0)),
        chunk,
        _mod_spec(layer, 5, row2)(d),
    ]
    args = [u, wg, wu, wd, res, mods5]
    if with_u:
        in_specs += [_mod_spec(layer + 1, 0, row2)(d), _mod_spec(layer + 1, 1, row2)(d)]
        args += [mods5, mods5]
    in_specs += [vec, vec]
    args += [ln_g.reshape(1, d), ln_b.reshape(1, d)]
    out_specs = [chunk]
    out_shape = [jax.ShapeDtypeStruct((r, d), F32)]
    if with_u:
        out_specs.append(chunk)
        out_shape.append(jax.ShapeDtypeStruct((r, d), BF16))
        c_in, c_args, c_out, c_shape = _cast_side_job(next_wts, n_m * n_h, lambda i, j: i * n_h + j)
        in_specs += c_in
        args += c_args
        out_specs += c_out
        out_shape += c_shape
    n_cast = len(next_wts) if with_u else 0
    out = pl.pallas_call(
        functools.partial(_ffn_kernel, alpha=alpha, with_u=with_u, n_m=n_m, n_epi=n_epi, n_cast=n_cast),
        grid=(n_m + 1, n_h),
        in_specs=in_specs,
        out_specs=out_specs,
        out_shape=out_shape,
        scratch_shapes=[pltpu.VMEM((tm, d), F32), pltpu.VMEM((tm, d), F32)],
        compiler_params=_params("arbitrary", "arbitrary"),
        name="ffn",
    )(*args)
    return (out[0], out[1], tuple(out[2:])) if with_u else (out[0], None, None)


def _gelu_tanh(x):
    return x * (0.5 * (1.0 + jnp.tanh(0.7978845608028654 * (x + 0.044715 * (x * x * x)))))


def _col_groups(n, width=2 * V7X_MXU_WIDTH):
    return [(c, min(c + width, n)) for c in range(0, n, width)]


def _win_kernel(*refs, n_lat, n_cast):
    if n_lat is None:
        u_ref, w_ref, y_ref, z_ref = refs
        u = u_ref[...]
    else:
        x_ref, c_ref, sh_ref, sc_ref, w_ref = refs[:5]
        cast_in, (y_ref, z_ref, h_ref) = refs[5:5 + n_cast], refs[5 + n_cast:8 + n_cast]
        cast_out, u_s = refs[8 + n_cast:8 + 2 * n_cast], refs[8 + 2 * n_cast]

        def emit(src_ref):
            v = src_ref[...]
            h_ref[...] = v
            u_s[...] = (v * (1.0 + sc_ref[...]) + sh_ref[...]).astype(BF16)

        pl.when(pl.program_id(0) < n_lat)(lambda: emit(x_ref))
        pl.when(pl.program_id(0) >= n_lat)(lambda: emit(c_ref))
        u = u_s[...]
        for src, dst in zip(cast_in, cast_out):
            dst[...] = src[...].astype(BF16)
    dr = y_ref.shape[1]
    for c0, c1 in _col_groups(dr):
        y_ref[:, c0:c1] = _gelu_tanh(jnp.dot(u, w_ref[:, c0:c1], preferred_element_type=F32)).astype(BF16)
    for c0, c1 in _col_groups(dr):
        z_ref[:, c0:c1] = jnp.dot(u, w_ref[:, dr + c0:dr + c1], preferred_element_type=F32)


def _win_call(u, w_in, tm, first=None):
    d, n = w_in.shape
    dr = n // 2
    wspec = pl.BlockSpec((d, n), lambda i: (0, 0), pipeline_mode=pl.Buffered(1))
    tile = lambda width: pl.BlockSpec((tm, width), lambda i: (i, 0))
    if first is None:
        r, n_lat, n_cast = u.shape[0], None, 0
        in_specs, args = [tile(d), wspec], [u, w_in]
        extra_out, extra_shape, scratch = [], [], []
    else:
        x2, c2, mods5, row_fn, cast_items = first
        n_lat, n_cast = x2.shape[0] // tm, len(cast_items)
        r = x2.shape[0] + c2.shape[0]
        c_in, c_args, c_out, c_shape = _cast_side_job(cast_items, r // tm, lambda i: i)
        in_specs = [
            pl.BlockSpec((tm, d), lambda i: (jnp.minimum(i, n_lat - 1), 0)),
            pl.BlockSpec((tm, d), lambda i: (jnp.maximum(i - n_lat, 0), 0)),
            _mod_spec(0, 0, row_fn)(d),
            _mod_spec(0, 1, row_fn)(d),
            wspec,
        ] + c_in
        args = [x2, c2, mods5, mods5, w_in] + c_args
        extra_out, extra_shape = [tile(d)] + c_out, [jax.ShapeDtypeStruct((r, d), F32)] + c_shape
        scratch = [pltpu.VMEM((tm, d), BF16)]
    out = pl.pallas_call(
        functools.partial(_win_kernel, n_lat=n_lat, n_cast=n_cast),
        grid=(r // tm,),
        in_specs=in_specs,
        out_specs=[tile(dr), tile(dr)] + extra_out,
        out_shape=[jax.ShapeDtypeStruct((r, dr), BF16), jax.ShapeDtypeStruct((r, dr), F32)] + extra_shape,
        scratch_shapes=scratch,
        compiler_params=_params("arbitrary"),
        name="rec_in",
    )(*args)
    return out if first is None else (out[0], out[1], out[2], tuple(out[3:]))


def _gate_window_plan(dr, bw):
    tn = V7X_MXU_WIDTH
    spans = []
    for n in range(dr // tn):
        c0 = n * tn
        lo = (c0 // bw) * bw
        hi = ((c0 + tn - 1) // bw + 1) * bw
        spans.append(((lo // V7X_LANES) * V7X_LANES, -(-hi // V7X_LANES) * V7X_LANES))
    kw = max(h - l for l, h in spans)
    return kw, tuple(min(l, dr - kw) for l, _ in spans)


def _gate_windows(w, kw, k0s):
    lead, bw = w.shape[:-3], w.shape[-1]
    tn = V7X_MXU_WIDTH
    w = w.astype(BF16)
    zero = jnp.zeros((), BF16)
    keep = [(0, 0, 0)] * (len(lead) + 1)
    tiles = []
    for n, k0 in enumerate(k0s):
        c0 = n * tn
        strips, filled = [], 0
        for h in range(c0 // bw, (c0 + tn - 1) // bw + 1):
            wr, wc = h * bw - k0, h * bw - c0
            sr0, sr1 = max(0, -wr), bw - max(0, wr + bw - kw)
            sc0, sc1 = max(0, -wc), bw - max(0, wc + bw - tn)
            if max(wr, 0) > filled:
                strips.append(jnp.zeros((*lead, max(wr, 0) - filled, tn), BF16))
            col = max(wc, 0)
            strips.append(lax.pad(w[..., h, sr0:sr1, sc0:sc1], zero, keep + [(col, tn - col - (sc1 - sc0), 0)]))
            filled = max(wr, 0) + sr1 - sr0
        if filled < kw:
            strips.append(jnp.zeros((*lead, kw - filled, tn), BF16))
        tiles.append(jnp.concatenate(strips, axis=-2))
    return jnp.stack(tiles, axis=-3)


def _gates_scan(zb, z_ref, wa_ref, wx_ref, ba_ref, bx_ref, lam_ref, a_s, b_s, dst, carry, kw, k0s, reverse):
    tt = zb.shape[0]
    tn = V7X_MXU_WIDTH
    sub = V7X_SUBLANES
    ng = tt // sub
    row = lax.broadcasted_iota(jnp.int32, (sub, tn), 0)
    entry = sub - 1 if reverse else 0
    for n, k0 in enumerate(k0s):
        cs = slice(n * tn, (n + 1) * tn)
        zw = zb[:, k0:k0 + kw]
        ta = jnp.tanh(jnp.dot(zw, wa_ref[n], preferred_element_type=F32) + ba_ref[:, cs])
        tx = jnp.tanh(jnp.dot(zw, wx_ref[n], preferred_element_type=F32) + bx_ref[:, cs])
        nl = -lam_ref[:, cs]
        half = (0.5 * RG_C) * (jnp.maximum(nl, 0.0) + jnp.log1p(jnp.exp(-jnp.abs(nl))))
        q = ta * half + half
        a = jnp.exp(-q)
        a_s[:, cs] = a
        x = jnp.tanh(q) * (1.0 + a * a)
        root = jnp.where(x > 0.0, x * lax.rsqrt(x), 0.0)
        b_s[:, cs] = root * (0.5 * tx + 0.5) * z_ref[:, cs]

        h = carry[:, cs]
        for g in (range(ng - 1, -1, -1) if reverse else range(ng)):
            rs = slice(g * sub, (g + 1) * sub)
            a8 = a_s[rs, cs]
            b8 = b_s[rs, cs]
            b8 = b8 + jnp.where(row == entry, a8 * h, 0.0)
            a8 = jnp.where(row == entry, 0.0, a8)
            for s in (1, 2, 4):
                shift = sub - s if reverse else s
                b8 = a8 * pltpu.roll(b8, shift, 0) + b8
                if s < 4:
                    a8 = a8 * pltpu.roll(a8, shift, 0)
            dst[rs, cs] = b8
            h = jnp.broadcast_to(b8[0:1, :] if reverse else b8[sub - 1:sub, :], (sub, tn))
        carry[:, cs] = h


def _scan_fwd_kernel(zc_ref, zp_ref, zn_ref, cw_ref, cb_ref, wa_ref, wx_ref, ba_ref, bx_ref, lam_ref,
                     hf_ref, z_ref, xpad, zb, a_s, b_s, carry, *, kw, k0s, ns):
    tt = zc_ref.shape[0]
    tn = V7X_MXU_WIDTH
    sub = V7X_SUBLANES
    t = pl.program_id(1)

    @pl.when(t == 0)
    def _():
        carry[...] = jnp.zeros_like(carry)

    first = t <= 1
    last = jnp.logical_or(t == 0, t == ns)
    xpad[0:sub, :] = jnp.where(first, 0.0, zp_ref[...])
    xpad[sub:sub + tt, :] = zc_ref[...]
    xpad[sub + tt:2 * sub + tt, :] = jnp.where(last, 0.0, zn_ref[...])
    for n in range(len(k0s)):
        cs = slice(n * tn, (n + 1) * tn)
        xa = xpad[:, cs]
        acc = None
        for j in range(CONV_W):
            shift = (CONV_LEFT - j) % xa.shape[0]
            xs = pltpu.roll(xa, shift, 0) if shift else xa
            term = xs[sub:sub + tt, :] * cw_ref[j:j + 1, cs]
            acc = term if acc is None else acc + term
        z = acc + cb_ref[:, cs]
        z_ref[:, cs] = z
        zb[:, cs] = z.astype(BF16)

    _gates_scan(zb, z_ref, wa_ref, wx_ref, ba_ref, bx_ref, lam_ref, a_s, b_s, hf_ref, carry, kw, k0s, reverse=False)


def _scan_bwd_kernel(z_ref, wa_ref, wx_ref, ba_ref, bx_ref, lam_ref, hf_ref, y_ref, o_ref, zb, a_s, b_s, h_s, carry,
                     *, kw, k0s):
    tn = V7X_MXU_WIDTH

    @pl.when(pl.program_id(1) == 0)
    def _():
        carry[...] = jnp.zeros_like(carry)

    zb[...] = z_ref[...].astype(BF16)
    _gates_scan(zb, z_ref, wa_ref, wx_ref, ba_ref, bx_ref, lam_ref, a_s, b_s, h_s, carry, kw, k0s, reverse=True)
    for n in range(len(k0s)):
        cs = slice(n * tn, (n + 1) * tn)
        o_ref[:, cs] = (y_ref[:, cs].astype(F32) * (hf_ref[:, cs] + h_s[:, cs])).astype(BF16)


def _scan_calls(zpre, y, conv_w, conv_b, wa, wx, ba, bx, lam, kw, k0s, batch, seq, ctx_len):
    r, dr = zpre.shape
    tt = ctx_len
    ns = seq // tt
    nx = batch * seq // tt
    g8 = tt // V7X_SUBLANES
    nblk8 = r // V7X_SUBLANES
    nt = len(k0s)

    def cur(reverse):
        def f(b, t):
            st = (ns - t) if reverse else (t - 1)
            return jnp.where(t == 0, nx + b, b * ns + st)
        return f

    def tile(reverse):
        return pl.BlockSpec((tt, dr), lambda b, t: (cur(reverse)(b, t), 0))

    fw = cur(False)
    halo_p = pl.BlockSpec((V7X_SUBLANES, dr), lambda b, t: (jnp.maximum(fw(b, t) * g8 - 1, 0), 0))
    halo_n = pl.BlockSpec((V7X_SUBLANES, dr), lambda b, t: (jnp.minimum(fw(b, t) * g8 + g8, nblk8 - 1), 0))
    vec = lambda d: pl.BlockSpec((None, 1, dr), lambda b, t: (d, 0, 0))
    wspec = lambda d: pl.BlockSpec((None, nt, kw, V7X_MXU_WIDTH), lambda b, t: (d, 0, 0, 0),
                                   pipeline_mode=pl.Buffered(1))
    ba, bx, lam = (v.reshape(2, 1, dr) for v in (ba, bx, lam))
    gate_specs = lambda d: [wspec(d), wspec(d), vec(d), vec(d), vec(d)]
    gate_args = [wa, wx, ba, bx, lam]
    work = [
        pltpu.VMEM((tt, dr), BF16),
        pltpu.VMEM((tt, dr), F32),
        pltpu.VMEM((tt, dr), F32),
    ]
    state = pltpu.VMEM((V7X_SUBLANES, dr), F32)
    hf, z = pl.pallas_call(
        functools.partial(_scan_fwd_kernel, kw=kw, k0s=k0s, ns=ns),
        grid=(batch, ns + 1),
        in_specs=[tile(False), halo_p, halo_n, pl.BlockSpec((CONV_W, dr), lambda b, t: (0, 0)),
                  pl.BlockSpec((1, dr), lambda b, t: (0, 0))] + gate_specs(0),
        out_specs=[tile(False), tile(False)],
        out_shape=[jax.ShapeDtypeStruct((r, dr), F32), jax.ShapeDtypeStruct((r, dr), F32)],
        scratch_shapes=[pltpu.VMEM((tt + 2 * V7X_SUBLANES, dr), F32)] + work + [state],
        compiler_params=_params("arbitrary", "arbitrary"),
        name="rec_scan_fwd",
    )(zpre, zpre, zpre, conv_w, conv_b.reshape(1, dr), *gate_args)
    return pl.pallas_call(
        functools.partial(_scan_bwd_kernel, kw=kw, k0s=k0s),
        grid=(batch, ns + 1),
        in_specs=[tile(True)] + gate_specs(1) + [tile(True), tile(True)],
        out_specs=tile(True),
        out_shape=jax.ShapeDtypeStruct((r, dr), BF16),
        scratch_shapes=work + [pltpu.VMEM((tt, dr), F32), state],
        compiler_params=_params("arbitrary", "arbitrary"),
        name="rec_scan_bwd",
    )(z, *gate_args, hf, y)


def _rope_tables(seq, pad_rows):
    rows = seq // GRID_W
    row = jnp.repeat(jnp.arange(rows, dtype=F32), GRID_W)
    col = jnp.tile(jnp.arange(GRID_W, dtype=F32), rows)
    axis_dim = HEAD_DIM // 2
    inv_freq = ROPE_BASE ** (-jnp.arange(0, axis_dim, 2, dtype=F32) / axis_dim)
    ang = jnp.concatenate([row[:, None] * inv_freq, col[:, None] * inv_freq], axis=-1)
    cos, sin = jnp.cos(ang), jnp.sin(ang)
    cos2 = jnp.concatenate([cos, cos], axis=-1)
    sin2 = jnp.concatenate([-sin, sin], axis=-1)
    cos2 = jnp.concatenate([cos2, jnp.ones((pad_rows, HEAD_DIM), F32)], axis=0)
    sin2 = jnp.concatenate([sin2, jnp.zeros((pad_rows, HEAD_DIM), F32)], axis=0)
    return cos2, sin2


def _qkv_kernel(u_ref, w_ref, cos_ref, sin_ref, o_ref, *, nq, nkv, scale):
    u = u_ref[...]
    c = cos_ref[...]
    s = sin_ref[...]
    for c0, c1 in _col_groups(o_ref.shape[1]):
        p = jnp.dot(u, w_ref[:, c0:c1], preferred_element_type=F32)
        for h0 in range(c0, c1, HEAD_DIM):
            t = p[:, h0 - c0:h0 - c0 + HEAD_DIM]
            if h0 < nq + nkv:
                t = t * c + pltpu.roll(t, HEAD_DIM // 2, 1) * s
            if h0 < nq:
                t = t * scale
            o_ref[:, h0:h0 + HEAD_DIM] = t.astype(BF16)


def _qkv_call(u, w, cos2, sin2, seq, rows_x, tm):
    r, d = u.shape
    n = w.shape[1]
    nkv = N_KV_HEADS * HEAD_DIM
    n_seq_tiles = seq // tm
    tab = pl.BlockSpec((tm, HEAD_DIM), lambda i: (jnp.where(i < rows_x // tm, i % n_seq_tiles, n_seq_tiles), 0))
    return pl.pallas_call(
        functools.partial(_qkv_kernel, nq=n - 2 * nkv, nkv=nkv, scale=HEAD_DIM ** -0.5 * LOG2_E),
        grid=(r // tm,),
        in_specs=[pl.BlockSpec((tm, d), lambda i: (i, 0)),
                  pl.BlockSpec((d, n), lambda i: (0, 0), pipeline_mode=pl.Buffered(1)), tab, tab],
        out_specs=pl.BlockSpec((tm, n), lambda i: (i, 0)),
        out_shape=jax.ShapeDtypeStruct((r, n), BF16),
        compiler_params=_params("arbitrary"),
        name="att_qkv",
    )(u, w, cos2, sin2)


def _attn_masks(gq, ctx_len):
    blk = WINDOW
    qi = lax.broadcasted_iota(jnp.int32, (gq * blk, blk), 0) % blk
    ki = lax.broadcasted_iota(jnp.int32, (gq * blk, blk), 1)
    on = jnp.zeros((gq * blk, blk), F32)
    off = jnp.full((gq * blk, blk), NEG_INF, F32)
    prev = jnp.where(ki >= qi, 0.0, NEG_INF).astype(F32)
    nxt = jnp.where(ki <= qi, 0.0, NEG_INF).astype(F32)
    ctx_on = jnp.zeros((gq * blk, ctx_len), F32)
    variants = ((off, on, nxt), (prev, on, nxt), (prev, on, off), (off, off, off))
    return jnp.stack([jnp.concatenate(v + (ctx_on,), axis=1) for v in variants])


def _attn_kernel(sink_ref, q_ref, kp_ref, kc_ref, kn_ref, vp_ref, vc_ref, vn_ref, kx_ref, vx_ref, bias, o_ref,
                 k_all, v_all, *, g):
    blk = WINDOW
    band = 3 * blk
    for dst, srcs in ((k_all, (kp_ref, kc_ref, kn_ref)), (v_all, (vp_ref, vc_ref, vn_ref))):
        for o, src in enumerate(srcs):
            dst[o * blk:(o + 1) * blk, :] = src[...]
    k_all[band:, :] = kx_ref[...]
    v_all[band:, :] = vx_ref[...]
    nt_dims = (((1,), (1,)), ((), ()))
    gq = bias.shape[0] // blk
    for h0 in range(0, N_KV_HEADS * g, gq):
        kh = h0 // g
        ks = slice(kh * HEAD_DIM, (kh + 1) * HEAD_DIM)
        heads = range(h0, h0 + gq)
        qs = jnp.concatenate([q_ref[:, h * HEAD_DIM:(h + 1) * HEAD_DIM] for h in heads], axis=0)
        sk = jnp.concatenate([jnp.full((blk, 1), sink_ref[h] * LOG2_E, F32) for h in heads], axis=0)
        s = lax.dot_general(qs, k_all[:, ks], nt_dims, preferred_element_type=F32) + bias[...]
        n_half = 2 if gq % 2 == 0 else 1
        hr = gq * blk // n_half
        for hf in range(n_half):
            rs = slice(hf * hr, (hf + 1) * hr)
            s_h, sk_h = s[rs], sk[rs]
            m = jnp.maximum(jnp.max(s_h, -1, keepdims=True), sk_h)
            p = jnp.exp2(s_h - m)
            denom = jnp.exp2(sk_h - m) + jnp.sum(p, -1, keepdims=True)
            o = jnp.dot(p.astype(BF16), v_all[:, ks], preferred_element_type=F32) / denom
            for gi in range(hr // blk):
                h = h0 + hf * (hr // blk) + gi
                o_ref[:, h * HEAD_DIM:(h + 1) * HEAD_DIM] = o[gi * blk:(gi + 1) * blk].astype(BF16)


def _attn_call(qkv, sink, batch, seq, ctx_len, ctx_queries):
    r, n = qkv.shape
    nkv = N_KV_HEADS * HEAD_DIM
    d = n - 2 * nkv
    g = d // HEAD_DIM // N_KV_HEADS
    gq = g
    blk = WINDOW
    nb = seq // blk
    ncb = ctx_len // blk
    kcol = d // nkv
    x0 = batch * seq

    def qrow(b, j):
        return jnp.where(j < nb, b * nb + j, x0 // blk + b * ncb + (j - nb))

    def krow(b, j, off):
        return b * nb + jnp.clip(jnp.minimum(j, nb - 1) + off, 0, nb - 1)

    assert nb >= 2
    qspec = pl.BlockSpec((blk, d), lambda b, j: (qrow(b, j), 0))
    band = lambda off, col: pl.BlockSpec((blk, nkv), lambda b, j: (krow(b, j, off), col))
    ctxs = lambda col: pl.BlockSpec((ctx_len, nkv), lambda b, j: (x0 // ctx_len + b, col))
    nkeys = 3 * blk + ctx_len
    mask = pl.BlockSpec((None, gq * blk, nkeys),
                        lambda b, j: (jnp.where(j >= nb, 3, jnp.where(j == 0, 0, jnp.where(j == nb - 1, 2, 1))), 0, 0))
    return pl.pallas_call(
        functools.partial(_attn_kernel, g=g),
        grid=(batch, nb + ncb if ctx_queries else nb),
        in_specs=[
            pl.BlockSpec(memory_space=pltpu.SMEM),
            qspec,
            band(-1, kcol), band(0, kcol), band(1, kcol),
            band(-1, kcol + 1), band(0, kcol + 1), band(1, kcol + 1),
            ctxs(kcol), ctxs(kcol + 1),
            mask,
        ],
        out_specs=qspec,
        out_shape=jax.ShapeDtypeStruct((r if ctx_queries else x0, d), BF16),
        scratch_shapes=[
            pltpu.VMEM((nkeys, nkv), BF16),
            pltpu.VMEM((nkeys, nkv), BF16),
        ],
        compiler_params=_params("arbitrary", "arbitrary"),
        name="att_core",
    )(sink, qkv, qkv, qkv, qkv, qkv, qkv, qkv, qkv, qkv, _attn_masks(gq, ctx_len))


def kernel(x, c, ctx, c_ctx, mod_w, mod_b, ln_mix_g, ln_mix_b, ln_ffn_g, ln_ffn_b, ffn_w_gate, ffn_w_up, ffn_w_down,
           rec_w_in, rec_conv_w, rec_conv_b, rec_gate_a_w, rec_gate_a_b, rec_gate_x_w, rec_gate_x_b, rec_lambda,
           rec_w_out, att_w_qkv, att_sink, att_w_o):
    batch, seq, d = x.shape
    ctx_len = ctx.shape[1]
    depth = mod_w.shape[0]
    rows_x, rows_c = batch * seq, batch * ctx_len
    assert seq % ctx_len == 0 and ctx_len % WINDOW == 0 and seq % GRID_W == 0
    assert d % (N_KV_HEADS * HEAD_DIM) == 0
    alpha = (2.0 * depth) ** 0.25

    tm = _pick(seq, (512, 256, 128))
    while rows_c % tm:
        tm //= 2
    tm_big = 2 * tm if (seq % (2 * tm) == 0 and rows_c % (2 * tm) == 0) else tm

    def row_fn_for(t):
        return lambda i: jnp.minimum((i * t) // seq, batch)

    row_fn, row_fn_big = row_fn_for(tm), row_fn_for(tm_big)

    mr = -(-(batch + 1) // V7X_SUBLANES) * V7X_SUBLANES
    cs = jnp.zeros((mr, d), F32).at[:batch].set(c).at[batch].set(c_ctx)
    mods5 = _mods_call(cs, mod_w, mod_b).reshape(depth, mr, N_MOD, 1, d)

    cos2, sin2 = _rope_tables(seq, tm_big)
    dr = rec_w_out.shape[1]
    kw, k0s = _gate_window_plan(dr, dr // RNN_BLOCKS)
    wa_all = _gate_windows(0.5 * rec_gate_a_w, kw, k0s)
    wx_all = _gate_windows(0.5 * rec_gate_x_w, kw, k0s)
    ba_all, bx_all = 0.5 * rec_gate_a_b, 0.5 * rec_gate_x_b

    def mixer_f32(layer):
        stacks = (rec_w_in, rec_w_out) if layer % N_MIXERS == 0 else (att_w_qkv, att_w_o)
        return [(s, layer // N_MIXERS) for s in stacks]

    def ffn_f32(layer):
        return [(s, layer) for s in (ffn_w_gate, ffn_w_up, ffn_w_down)]

    mix_wts = (rec_w_in[0].astype(BF16), rec_w_out[0].astype(BF16))
    ffn_wts = None
    h = u = None
    for i in range(depth):
        j = i // N_MIXERS
        last = i == depth - 1
        rows = rows_x if last else rows_x + rows_c
        if i % N_MIXERS == 0:
            if i == 0:
                tm0 = max(tm // 2, WINDOW)
                first = (x.reshape(rows_x, d), ctx.reshape(rows_c, d), mods5, row_fn_for(tm0), ffn_f32(0))
                y, zpre, h, ffn_wts = _win_call(None, mix_wts[0], tm0, first)
            else:
                y, zpre = _win_call(u, mix_wts[0], tm)
            yh = _scan_calls(zpre, y, rec_conv_w[j], rec_conv_b[j], wa_all[j], wx_all[j], ba_all[j], bx_all[j],
                             rec_lambda[j], kw, k0s, batch, seq, ctx_len)
            h, u = _proj_ln_call(yh, mix_wts[1], h, rows, mods5, i, ln_mix_g[i], ln_mix_b[i], row_fn, tm, alpha,
                                 "rec_out")
        else:
            qkv = _qkv_call(u, mix_wts[0], cos2, sin2, seq, rows_x, tm_big)
            ao = _attn_call(qkv, att_sink[j], batch, seq, ctx_len, ctx_queries=not last)
            h, u = _proj_ln_call(ao, mix_wts[1], h, rows, mods5, i, ln_mix_g[i], ln_mix_b[i], row_fn, tm, alpha,
                                 "att_out")
        next_wts = None if last else ffn_f32(i + 1) + mixer_f32(i + 1)
        h, u, casts = _ffn_call(u, ffn_wts, h, rows, mods5, i, next_wts, ln_ffn_g[i], ln_ffn_b[i], row_fn_big, tm_big,
                                alpha)
        if not last:
            ffn_wts, mix_wts = casts[:3], casts[3:]
    return h.reshape(batch, seq, d)
```
